```python
import math
import jax
import jax.numpy as jnp
from jax import lax
import numpy as np

D_MODEL = 2048
BATCH = 4
SEQ = 2048
DEPTH = 4
DEC_BATCH = 128
DEC_SEQ = 8
PAST_LEN = 16384
PAGE_SIZE = 128

N_EVEN = (DEPTH + 1) // 2
N_ODD = DEPTH // 2

D_A = D_MODEL // 2
HGRN_EXPAND = 128
H_A = D_A // HGRN_EXPAND
DK_A = HGRN_EXPAND
DV_A = D_A // H_A
D_B = D_MODEL // 2
N_B = 64
H_B = D_B // N_B
R_W = 64
R_A = 64
R_G = 128
D_RWKV_PROJ = 3 * D_B + R_W + R_A + R_G
D_IN_EVEN = 4 * D_A + D_RWKV_PROJ
D_C = D_MODEL // 2
H_C = 4
DK_C = D_C // H_C
DV_C = D_C // H_C
D_D = D_MODEL // 2
H_D = 4
BW_D = D_D // H_D
CONV_W = 4
LRU_C = 8.0
D_IN_ODD = 4 * D_C + 2 * D_D
D_FF = ((8 * D_MODEL + 3 * 256 - 1) // (3 * 256)) * 256

CHUNK = 64
RMS_EPS = 1e-6
GN_EPS = 64e-5
TINY = 1e-30

kernel_name = 'hgrn2_rwkv7_retnet_rglru_hybrid_step'


def rms_norm(x, w, eps=RMS_EPS):
    xf = x.astype(jnp.float32)
    y = xf * lax.rsqrt(jnp.mean(xf * xf, axis=-1, keepdims=True) + eps)
    return (y * w.astype(jnp.float32)).astype(x.dtype)


def split_cols(z, widths):
    idx = np.cumsum(widths)[:-1].tolist()
    return jnp.split(z, idx, axis=-1)


def to_heads(z, nh):
    b, t, _ = z.shape
    return z.reshape(b, t, nh, -1).transpose(0, 2, 1, 3)


def from_heads(o):
    b, h, t, d = o.shape
    return o.transpose(0, 2, 1, 3).reshape(b, t, h * d)


def head_rms(o):
    return o * lax.rsqrt(jnp.mean(o * o, axis=-1, keepdims=True) + RMS_EPS)


def chunked_gated_linear(q, k, v, log_decay, s0):
    bsz, nh, T, _ = q.shape
    dv = v.shape[-1]
    c = math.gcd(T, CHUNK)
    nc = T // c
    f32 = jnp.float32

    def blocks(a):
        a = a.astype(f32).reshape(bsz, nh, nc, c, a.shape[-1])
        return jnp.moveaxis(a, 2, 0)

    causal = jnp.tril(jnp.ones((c, c), dtype=bool))
    scalar_decay = log_decay.shape[-1] == 1

    def step(s, blk):
        qc, kc, vc, gc = blk
        b = jnp.cumsum(gc, axis=2)
        b_last = b[:, :, -1:, :]
        if scalar_decay:
            diff = b[:, :, :, None, 0] - b[:, :, None, :, 0]
            dec = jnp.where(causal, jnp.exp(jnp.where(causal, diff, 0.0)), 0.0)
            att = jnp.einsum('bhtd,bhsd->bhts', qc, kc) * dec
        else:
            diff = b[:, :, :, None, :] - b[:, :, None, :, :]
            cm = causal[:, :, None]
            dec = jnp.where(cm, jnp.exp(jnp.where(cm, diff, 0.0)), 0.0)
            att = jnp.einsum('bhtd,bhsd,bhtsd->bhts', qc, kc, dec)
        o = jnp.einsum('bhts,bhsv->bhtv', att, vc) + jnp.einsum('bhtd,bhdv->bhtv', qc * jnp.exp(b), s)
        s = jnp.exp(b_last[:, :, 0, :, None]) * s + jnp.einsum('bhsd,bhsv->bhdv', kc * jnp.exp(b_last - b), vc)
        return s, o

    s_fin, o = lax.scan(step, s0.astype(f32), (blocks(q), blocks(k), blocks(v), blocks(log_decay)))
    o = jnp.moveaxis(o, 0, 2).reshape(bsz, nh, T, dv)
    return o, s_fin


def rwkv7_scan(r, w, k, v, kk, a, s0):
    xs = tuple(jnp.moveaxis(t, 1, 0) for t in (r, w, k, v, kk, a))

    def step(s, inp):
        r_t, w_t, k_t, v_t, kk_t, a_t = inp
        sa = jnp.einsum('bhij,bhj->bhi', s, -kk_t)
        s = (s * w_t[:, :, None, :] + sa[..., None] * (kk_t * a_t)[:, :, None, :]
             + v_t[..., None] * k_t[:, :, None, :])
        return s, jnp.einsum('bhij,bhj->bhi', s, r_t)

    s_fin, ys = lax.scan(step, s0.astype(jnp.float32), xs)
    return jnp.moveaxis(ys, 0, 1), s_fin


def linear_scan(a, b, h0):
    b = b.at[:, 0].add(a[:, 0] * h0)

    def combine(l, r):
        a_l, b_l = l
        a_r, b_r = r
        return a_l * a_r, a_r * b_l + b_r

    _, h = lax.associative_scan(combine, (a, b), axis=1)
    return h


def hgrn_lower_bounds(lb_raw):
    pr = jax.nn.softmax(lb_raw.astype(jnp.float32), axis=0)
    return jnp.cumsum(pr, axis=0) - pr[0]


def retention_rotary(pos):
    angle = 1.0 / (10000.0 ** jnp.linspace(0.0, 1.0, DK_C // 2, dtype=jnp.float32))
    angle = jnp.repeat(angle, 2)
    ang = pos.astype(jnp.float32)[:, None] * angle[None, :]
    return jnp.sin(ang), jnp.cos(ang)


def theta_shift(x, sin, cos):
    x1 = x[..., ::2]
    x2 = x[..., 1::2]
    rot = jnp.stack((-x2, x1), axis=-1).reshape(x.shape)
    return x * cos[None, :, None, :] + rot * sin[None, :, None, :]


def swiglu(h, w_in, w_out):
    u = jnp.einsum('btd,df->btf', h, w_in)
    gt, up = jnp.split(u, 2, axis=-1)
    return jnp.einsum('btf,fd->btd', jax.nn.silu(gt) * up, w_out)


def even_mixer(h, lb, s_hgrn, s_rwkv, s_shift, p, j):
    f32 = jnp.float32
    bsz, T, _ = h.shape
    z = jnp.einsum('btd,de->bte', h, p['w_in_even'][j])
    z_a, z_b = z[..., :4 * D_A], z[..., 4 * D_A:]
    q, f_logit, i, g = split_cols(z_a, [D_A] * 4)
    zf = f_logit.astype(f32)
    fg = lb + (1.0 - lb) * jax.nn.sigmoid(zf)
    log_f = jnp.log(jnp.maximum(fg, TINY))
    k_a = (1.0 - lb) * jax.nn.sigmoid(-zf)
    o_a, s_hgrn_new = chunked_gated_linear(to_heads(jax.nn.silu(q), H_A), to_heads(k_a, H_A),
                                           to_heads(i, H_A), to_heads(log_f, H_A), s_hgrn)
    o_a = head_rms(o_a) * p['hgrn_norm_w'][j][:, None, :]
    o_a = from_heads(o_a) * jax.nn.silu(g.astype(f32))
    prev = jnp.concatenate([s_shift[:, None, :].astype(z_b.dtype), z_b[:, :-1]], axis=1)
    zs = z_b + (prev - z_b) * p['rwkv_mu'][j]
    shift_new = z_b[:, -1]
    r, kb, v, wl, al, gl = split_cols(zs, [D_B, D_B, D_B, R_W, R_A, R_G])
    w_log = -jnp.exp(-jax.nn.softplus(-(p['rwkv_w0'][j] + jnp.tanh(wl) @ p['rwkv_w2'][j]).astype(f32)) - 0.5)
    decay = jnp.exp(w_log)
    a = jax.nn.sigmoid((p['rwkv_a0'][j] + al @ p['rwkv_a2'][j]).astype(f32))
    gate = (jax.nn.sigmoid(gl) @ p['rwkv_g2'][j]).astype(f32)
    hs = (bsz, T, H_B, N_B)
    r = r.astype(f32).reshape(hs)
    kb = kb.astype(f32).reshape(hs)
    v = v.astype(f32).reshape(hs)
    a = a.reshape(hs)
    decay = decay.reshape(hs)
    kk = kb * p['rwkv_k_k'][j].reshape(H_B, N_B)
    kk = kk / jnp.maximum(jnp.sqrt(jnp.sum(kk * kk, axis=-1, keepdims=True)), 1e-12)
    kb = kb * (1.0 + (a - 1.0) * p['rwkv_k_a'][j].reshape(H_B, N_B))
    y, s_rwkv_new = rwkv7_scan(r, decay, kb, v, kk, a, s_rwkv)
    mean = jnp.mean(y, axis=-1, keepdims=True)
    var = jnp.mean(jnp.square(y - mean), axis=-1, keepdims=True)
    y = ((y - mean) * lax.rsqrt(var + GN_EPS) * p['rwkv_ln_w'][j].reshape(H_B, N_B)
         + p['rwkv_ln_b'][j].reshape(H_B, N_B))
    y = y + jnp.sum(r * kb * p['rwkv_r_k'][j], axis=-1, keepdims=True) * v
    o_b = y.reshape(bsz, T, D_B) * gate
    o = jnp.concatenate([o_a, o_b], axis=-1).astype(h.dtype)
    out = jnp.einsum('bte,ed->btd', o, p['w_out_even'][j])
    return out, s_hgrn_new, s_rwkv_new, shift_new


def odd_mixer(h, pos, s_ret, s_conv, s_h, p, j):
    f32 = jnp.float32
    bsz, T, _ = h.shape
    z = jnp.einsum('btd,de->bte', h, p['w_in_odd'][j])
    q, k, v, g, y_br, x_br = split_cols(z, [D_C] * 4 + [D_D] * 2)
    sin, cos = retention_rotary(pos)
    qh = theta_shift(q.astype(f32).reshape(bsz, T, H_C, DK_C), sin, cos).transpose(0, 2, 1, 3)
    kh = (theta_shift(k.astype(f32).reshape(bsz, T, H_C, DK_C), sin, cos) * DK_C ** -0.5).transpose(0, 2, 1, 3)
    log_gamma = jnp.log1p(-jnp.exp2(-5.0 - jnp.arange(H_C, dtype=f32)))
    lg = jnp.broadcast_to(log_gamma[None, :, None, None], (bsz, H_C, T, 1))
    o_c, s_ret_new = chunked_gated_linear(qh, kh, to_heads(v, H_C), lg, s_ret)
    o_c = from_heads(head_rms(o_c)) * jax.nn.silu(g.astype(f32))
    buf = jnp.concatenate([s_conv.astype(x_br.dtype), x_br], axis=1)
    conv_new = buf[:, -(CONV_W - 1):]
    cw = p['conv_w'][j]
    xc = p['conv_b'][j] + sum(cw[m] * buf[:, m:m + T] for m in range(CONV_W))
    xc = xc.astype(f32)
    xb = xc.reshape(bsz, T, H_D, BW_D)
    rg = jax.nn.sigmoid(jnp.einsum('bthi,hij->bthj', xb, p['rglru_wa'][j]).reshape(bsz, T, D_D) + p['rglru_ba'][j])
    ig = jax.nn.sigmoid(jnp.einsum('bthi,hij->bthj', xb, p['rglru_wx'][j]).reshape(bsz, T, D_D) + p['rglru_bx'][j])
    log_a = -LRU_C * rg * jax.nn.softplus(-p['rglru_lambda'][j].astype(f32))
    a = jnp.exp(log_a)
    mult = jnp.where((pos == 0)[None, :, None], 1.0, jnp.sqrt(-jnp.expm1(2.0 * log_a)))
    hseq = linear_scan(a, mult * ig * xc, s_h.astype(f32))
    h_new = hseq[:, -1]
    o_d = hseq * jax.nn.gelu(y_br.astype(f32), approximate=True)
    o = jnp.concatenate([o_c, o_d], axis=-1).astype(h.dtype)
    out = jnp.einsum('bte,ed->btd', o, p['w_out_odd'][j])
    return out, s_ret_new, conv_new, h_new


def run_trunk(x, pos0, st_hgrn, st_rwkv, st_shift, st_ret, st_conv, st_h, p):
    T = x.shape[1]
    pos = pos0 + jnp.arange(T, dtype=jnp.int32)
    lbs = hgrn_lower_bounds(p['hgrn_lb_raw'])
    n_hgrn, n_rwkv, n_shift, n_ret, n_conv, n_h = [], [], [], [], [], []
    for l in range(DEPTH):
        hn = rms_norm(x, p['norm_mix_pre'][l])
        j = l // 2
        if l % 2 == 0:
            mix, sa, sb, sh = even_mixer(hn, lbs[j], st_hgrn[j], st_rwkv[j], st_shift[j], p, j)
            n_hgrn.append(sa.astype(st_hgrn.dtype))
            n_rwkv.append(sb.astype(st_rwkv.dtype))
            n_shift.append(sh.astype(st_shift.dtype))
        else:
            mix, sc, scv, shh = odd_mixer(hn, pos, st_ret[j], st_conv[j], st_h[j], p, j)
            n_ret.append(sc.astype(st_ret.dtype))
            n_conv.append(scv.astype(st_conv.dtype))
            n_h.append(shh.astype(st_h.dtype))
        x = x + rms_norm(mix, p['norm_mix_post'][l])
        hn = rms_norm(x, p['norm_ffn_pre'][l])
        x = x + rms_norm(swiglu(hn, p['w_ffn_in'][l], p['w_ffn_out'][l]), p['norm_ffn_post'][l])
    return x, (jnp.stack(n_hgrn), jnp.stack(n_rwkv), jnp.stack(n_shift),
               jnp.stack(n_ret), jnp.stack(n_conv), jnp.stack(n_h))


def zero_state(s, batch):
    return jnp.zeros((s.shape[0], batch) + s.shape[2:], s.dtype)


def setup_inputs(seed: int = 0) -> dict:
    key = jax.random.key(seed)
    keys = list(jax.random.split(key, 48))
    f32 = jnp.float32

    def nk():
        return keys.pop()

    def dense(shape, fan_in, s=1.0):
        return jax.random.normal(nk(), shape, f32) * (s * fan_in ** -0.5)

    def gain(shape):
        return 1.0 + 0.05 * jax.random.normal(nk(), shape, f32)

    def small(shape, s=0.02):
        return s * jax.random.normal(nk(), shape, f32)

    u = jax.random.uniform(nk(), (N_ODD, D_D), f32, 0.81, 0.998)
    sl = u ** (1.0 / LRU_C)
    lam = jnp.log(sl) - jnp.log1p(-sl)
    return {
        'x_prompt': jax.random.normal(nk(), (BATCH, SEQ, D_MODEL), f32),
        'x_sample': jax.random.normal(nk(), (DEC_BATCH, DEC_SEQ, D_MODEL), f32),
        'state_hgrn': 0.5 * jax.random.normal(nk(), (N_EVEN, DEC_BATCH, H_A, DK_A, DV_A), f32),
        'state_rwkv': 0.3 * jax.random.normal(nk(), (N_EVEN, DEC_BATCH, H_B, N_B, N_B), f32),
        'state_rwkv_shift': jax.random.normal(nk(), (N_EVEN, DEC_BATCH, D_RWKV_PROJ), f32),
        'state_ret': 0.1 * jax.random.normal(nk(), (N_ODD, DEC_BATCH, H_C, DK_C, DV_C), f32),
        'state_rglru_conv': jax.random.normal(nk(), (N_ODD, DEC_BATCH, CONV_W - 1, D_D), f32),
        'state_rglru_h': 0.5 * jax.random.normal(nk(), (N_ODD, DEC_BATCH, D_D), f32),
        'norm_mix_pre': gain((DEPTH, D_MODEL)),
        'norm_mix_post': gain((DEPTH, D_MODEL)),
        'norm_ffn_pre': gain((DEPTH, D_MODEL)),
        'norm_ffn_post': gain((DEPTH, D_MODEL)),
        'w_in_even': dense((N_EVEN, D_MODEL, D_IN_EVEN), D_MODEL),
        'w_out_even': dense((N_EVEN, D_A + D_B, D_MODEL), D_A + D_B),
        'hgrn_lb_raw': jax.random.normal(nk(), (N_EVEN, D_A), f32),
        'hgrn_norm_w': gain((N_EVEN, H_A, DV_A)),
        'rwkv_mu': jax.random.uniform(nk(), (N_EVEN, D_RWKV_PROJ), f32),
        'rwkv_w0': jax.random.uniform(nk(), (N_EVEN, D_B), f32, -5.0, 1.0),
        'rwkv_w2': dense((N_EVEN, R_W, D_B), R_W, 0.5),
        'rwkv_a0': small((N_EVEN, D_B), 0.1),
        'rwkv_a2': dense((N_EVEN, R_A, D_B), R_A, 0.5),
        'rwkv_g2': dense((N_EVEN, R_G, D_B), R_G),
        'rwkv_k_k': 0.85 + small((N_EVEN, D_B), 0.05),
        'rwkv_k_a': gain((N_EVEN, D_B)),
        'rwkv_r_k': small((N_EVEN, H_B, N_B), 0.1),
        'rwkv_ln_w': gain((N_EVEN, D_B)),
        'rwkv_ln_b': small((N_EVEN, D_B)),
        'w_in_odd': dense((N_ODD, D_MODEL, D_IN_ODD), D_MODEL),
        'w_out_odd': dense((N_ODD, D_C + D_D, D_MODEL), D_C + D_D),
        'conv_w': dense((N_ODD, CONV_W, D_D), CONV_W),
        'conv_b': small((N_ODD, D_D)),
        'rglru_wa': dense((N_ODD, H_D, BW_D, BW_D), BW_D),
        'rglru_ba': small((N_ODD, D_D)),
        'rglru_wx': dense((N_ODD, H_D, BW_D, BW_D), BW_D),
        'rglru_bx': small((N_ODD, D_D)),
        'rglru_lambda': lam,
        'w_ffn_in': dense((DEPTH, D_MODEL, 2 * D_FF), D_MODEL),
        'w_ffn_out': dense((DEPTH, D_FF, D_MODEL), D_FF),
    }


def reference(x_prompt, x_sample, state_hgrn, state_rwkv, state_rwkv_shift, state_ret,
              state_rglru_conv, state_rglru_h, norm_mix_pre, norm_mix_post, norm_ffn_pre,
              norm_ffn_post, w_in_even, w_out_even, hgrn_lb_raw, hgrn_norm_w, rwkv_mu, rwkv_w0,
              rwkv_w2, rwkv_a0, rwkv_a2, rwkv_g2, rwkv_k_k, rwkv_k_a, rwkv_r_k, rwkv_ln_w,
              rwkv_ln_b, w_in_odd, w_out_odd, conv_w, conv_b, rglru_wa, rglru_ba, rglru_wx,
              rglru_bx, rglru_lambda, w_ffn_in, w_ffn_out):
    p = dict(norm_mix_pre=norm_mix_pre, norm_mix_post=norm_mix_post, norm_ffn_pre=norm_ffn_pre,
             norm_ffn_post=norm_ffn_post, w_in_even=w_in_even, w_out_even=w_out_even,
             hgrn_lb_raw=hgrn_lb_raw, hgrn_norm_w=hgrn_norm_w, rwkv_mu=rwkv_mu, rwkv_w0=rwkv_w0,
             rwkv_w2=rwkv_w2, rwkv_a0=rwkv_a0, rwkv_a2=rwkv_a2, rwkv_g2=rwkv_g2,
             rwkv_k_k=rwkv_k_k, rwkv_k_a=rwkv_k_a, rwkv_r_k=rwkv_r_k, rwkv_ln_w=rwkv_ln_w,
             rwkv_ln_b=rwkv_ln_b, w_in_odd=w_in_odd, w_out_odd=w_out_odd, conv_w=conv_w,
             conv_b=conv_b, rglru_wa=rglru_wa, rglru_ba=rglru_ba, rglru_wx=rglru_wx,
             rglru_bx=rglru_bx, rglru_lambda=rglru_lambda, w_ffn_in=w_ffn_in, w_ffn_out=w_ffn_out)
    bp = x_prompt.shape[0]
    y_prompt, st_p = run_trunk(x_prompt, 0,
                               zero_state(state_hgrn, bp), zero_state(state_rwkv, bp),
                               zero_state(state_rwkv_shift, bp), zero_state(state_ret, bp),
                               zero_state(state_rglru_conv, bp), zero_state(state_rglru_h, bp), p)
    y_sample, st_s = run_trunk(x_sample, PAST_LEN, state_hgrn, state_rwkv, state_rwkv_shift,
                               state_ret, state_rglru_conv, state_rglru_h, p)
    hgrn_p, rwkv_p, shift_p, ret_p, conv_p, h_p = st_p
    hgrn_s, rwkv_s, shift_s, ret_s, conv_s, h_s = st_s
    return (y_prompt, y_sample, hgrn_p, hgrn_s, rwkv_p, rwkv_s, shift_p, shift_s,
            ret_p, ret_s, conv_p, conv_s, h_p, h_s)
```

```python
import functools
import math

import jax
import jax.numpy as jnp
from jax import lax
from jax.experimental import pallas as pl
from jax.experimental.pallas import tpu as pltpu

F32 = jnp.float32
BF16 = jnp.bfloat16
HIGHEST = lax.Precision.HIGHEST

D_MODEL = 2048
DEPTH = 4
N_EVEN = (DEPTH + 1) // 2
N_ODD = DEPTH // 2
D_A = D_MODEL // 2
H_A = 8
DK_A = 128
DV_A = 128
D_B = D_MODEL // 2
N_B = 64
H_B = D_B // N_B
R_W = 64
R_A = 64
R_G = 128
D_RWKV_PROJ = 3 * D_B + R_W + R_A + R_G
D_C = D_MODEL // 2
H_C = 4
DK_C = D_C // H_C
DV_C = D_C // H_C
D_D = D_MODEL // 2
H_D = 4
BW_D = D_D // H_D
CONV_W = 4
LRU_C = 8.0
D_FF = ((8 * D_MODEL + 3 * 256 - 1) // (3 * 256)) * 256
RMS_EPS = 1e-6
GN_EPS = 64e-5
TINY = 1e-30
PAST_LEN = 16384

VMEM_LIMIT_BYTES = 56 * 1024 * 1024
LANES = 128


def _params(*sem):
    return pltpu.CompilerParams(dimension_semantics=sem, vmem_limit_bytes=VMEM_LIMIT_BYTES)


def _rms(x, w):
    return x * lax.rsqrt(jnp.mean(x * x, axis=-1, keepdims=True) + RMS_EPS) * w


def _sigmoid(x):
    return jax.nn.sigmoid(x)


def _silu(x):
    return x * jax.nn.sigmoid(x)


def _dot(a, b):
    return jnp.dot(a, b, preferred_element_type=F32)


def _dot_nt(a, b):
    return lax.dot_general(a, b, (((1,), (1,)), ((), ())), preferred_element_type=F32)


def _dot_tn(a, b):
    return lax.dot_general(a, b, (((0,), (0,)), ((), ())), preferred_element_type=F32)


def _dot32(a, b):
    return jnp.dot(a, b, preferred_element_type=F32, precision=HIGHEST)


def _dot32_nt(a, b):
    return lax.dot_general(a, b, (((1,), (1,)), ((), ())), preferred_element_type=F32, precision=HIGHEST)


def _dot32_tn(a, b):
    return lax.dot_general(a, b, (((0,), (0,)), ((), ())), preferred_element_type=F32, precision=HIGHEST)


def _lower_tri(n, strict):
    r = lax.broadcasted_iota(jnp.int32, (n, n), 0)
    c = lax.broadcasted_iota(jnp.int32, (n, n), 1)
    return (r > c) if strict else (r >= c)


def _rmsnorm_kernel(x_ref, w_ref, o_ref):
    o_ref[...] = _rms(x_ref[...], w_ref[...]).astype(o_ref.dtype)


def _rmsnorm_bf16(x, w, tm=512):
    m, d = x.shape
    return pl.pallas_call(
        _rmsnorm_kernel,
        grid=(m // tm,),
        in_specs=[pl.BlockSpec((tm, d), lambda i: (i, 0)), pl.BlockSpec((1, d), lambda i: (0, 0))],
        out_specs=pl.BlockSpec((tm, d), lambda i: (i, 0)),
        out_shape=jax.ShapeDtypeStruct((m, d), BF16),
        compiler_params=_params("parallel"),
        name="rmsnorm",
    )(x, w.reshape(1, d))


def _matmul_kernel(a_ref, b_ref, o_ref):
    o_ref[...] = _dot(a_ref[...], b_ref[...])


def _matmul(a, b, tn, tm=1024):
    m, k = a.shape
    n = b.shape[1]
    tm = min(tm, m)
    return pl.pallas_call(
        _matmul_kernel,
        grid=(m // tm, n // tn),
        in_specs=[pl.BlockSpec((tm, k), lambda i, j: (i, 0)), pl.BlockSpec((k, tn), lambda i, j: (0, j))],
        out_specs=pl.BlockSpec((tm, tn), lambda i, j: (i, j)),
        out_shape=jax.ShapeDtypeStruct((m, n), F32),
        compiler_params=_params("parallel", "parallel"),
        name="in_proj",
    )(a, b)


def _outproj_kernel(oa_ref, ob_ref, wa_ref, wb_ref, x_ref, post_ref, pre_ref, xo_ref, hn_ref):
    mix = _dot(oa_ref[...], wa_ref[...]) + _dot(ob_ref[...], wb_ref[...])
    xn = x_ref[...] + _rms(mix, post_ref[...])
    xo_ref[...] = xn
    hn_ref[...] = _rms(xn, pre_ref[...]).astype(hn_ref.dtype)


def _outproj(oa, ob, w_out, x, post_w, pre_w, tm=512):
    m, d = x.shape
    ka, kb = oa.shape[1], ob.shape[1]
    row = lambda i: (i, 0)
    fixed = lambda i: (0, 0)
    return pl.pallas_call(
        _outproj_kernel,
        grid=(m // tm,),
        in_specs=[pl.BlockSpec((tm, ka), row), pl.BlockSpec((tm, kb), row),
                  pl.BlockSpec((ka, d), fixed), pl.BlockSpec((kb, d), lambda i: (1, 0)),
                  pl.BlockSpec((tm, d), row), pl.BlockSpec((1, d), fixed), pl.BlockSpec((1, d), fixed)],
        out_specs=[pl.BlockSpec((tm, d), row), pl.BlockSpec((tm, d), row)],
        out_shape=[jax.ShapeDtypeStruct((m, d), F32), jax.ShapeDtypeStruct((m, d), BF16)],
        compiler_params=_params("parallel"),
        name="out_proj",
    )(oa, ob, w_out, w_out, x, post_w.reshape(1, d), pre_w.reshape(1, d))


def _ffn_kernel(h_ref, wg_ref, wu_ref, wo_ref, x_ref, post_ref, pre_ref, xo_ref, hn_ref, acc_ref):
    j = pl.program_id(1)

    @pl.when(j == 0)
    def _():
        acc_ref[...] = jnp.zeros_like(acc_ref)

    h = h_ref[...]
    gate = _dot(h, wg_ref[...])
    up = _dot(h, wu_ref[...])
    act = (_silu(gate) * up).astype(BF16)
    acc_ref[...] += _dot(act, wo_ref[...])

    @pl.when(j == pl.num_programs(1) - 1)
    def _():
        xn = x_ref[...] + _rms(acc_ref[...], post_ref[...])
        xo_ref[...] = xn
        hn_ref[...] = _rms(xn, pre_ref[...]).astype(hn_ref.dtype)


def _ffn(hn, w_in, w_out, x, post_w, pre_w, tm=512, tf=512):
    m, d = x.shape
    nf = D_FF // tf
    row = lambda i, j: (i, 0)
    fixed = lambda i, j: (0, 0)
    return pl.pallas_call(
        _ffn_kernel,
        grid=(m // tm, nf),
        in_specs=[pl.BlockSpec((tm, d), row),
                  pl.BlockSpec((d, tf), lambda i, j: (0, j)),
                  pl.BlockSpec((d, tf), lambda i, j: (0, j + nf)),
                  pl.BlockSpec((tf, d), lambda i, j: (j, 0)),
                  pl.BlockSpec((tm, d), row), pl.BlockSpec((1, d), fixed), pl.BlockSpec((1, d), fixed)],
        out_specs=[pl.BlockSpec((tm, d), row), pl.BlockSpec((tm, d), row)],
        out_shape=[jax.ShapeDtypeStruct((m, d), F32), jax.ShapeDtypeStruct((m, d), BF16)],
        scratch_shapes=[pltpu.VMEM((tm, d), F32)],
        compiler_params=_params("parallel", "arbitrary"),
        name="ffn",
    )(hn, w_in, w_in, w_out, x, post_w.reshape(1, d), pre_w.reshape(1, d))


def _hgrn_kernel(q_ref, f_ref, i_ref, g_ref, lbraw_ref, nw_ref, s0_ref, o_ref, sout_ref, st_ref,
                 *, layer, sc, n_sub, bb):
    ci = pl.program_id(2)

    @pl.when(ci == 0)
    def _():
        for b in range(bb):
            st_ref[b] = s0_ref[b, 0].T

    raw = lbraw_ref[...]
    e = jnp.exp(raw - jnp.max(raw, axis=0, keepdims=True))
    pr = e / jnp.sum(e, axis=0, keepdims=True)
    lb = jnp.zeros((1, LANES), F32)
    for r in range(1, layer + 1):
        lb = lb + pr[r:r + 1, :]
    oml = 1.0 - lb
    nw = nw_ref[0]
    tri = _lower_tri(sc, strict=False).astype(F32)
    row = lax.broadcasted_iota(jnp.int32, (sc, LANES), 0)

    def body(idx, carry):
        b = idx // n_sub
        r0 = pl.multiple_of((idx % n_sub) * sc, sc)
        rows = pl.ds(r0, sc)
        q = q_ref[b, rows, :]
        zf = f_ref[b, rows, :]
        iv = i_ref[b, rows, :]
        g = g_ref[b, rows, :]
        fg = lb + oml * _sigmoid(zf)
        logf = jnp.log(jnp.maximum(fg, TINY))
        kk = oml * _sigmoid(-zf)
        qs = _silu(q)
        bc = _dot32(tri, logf)
        o_rows = []
        for t in range(sc):
            m = row <= t
            dec = jnp.where(m, jnp.exp(jnp.where(m, bc[t:t + 1, :] - bc, 0.0)), 0.0)
            a_col = jnp.sum(kk * dec * qs[t:t + 1, :], axis=-1, keepdims=True)
            o_rows.append(jnp.sum(a_col * iv, axis=0, keepdims=True))
        o_intra = jnp.concatenate(o_rows, axis=0)
        st = st_ref[b]
        qe = (qs * jnp.exp(bc)).astype(BF16)
        o = o_intra + _dot_nt(qe, st.astype(BF16))
        blast = bc[sc - 1:sc, :]
        kh = (kk * jnp.exp(blast - bc)).astype(BF16)
        st_ref[b] = st * jnp.exp(blast) + _dot_tn(iv.astype(BF16), kh)
        o = o * lax.rsqrt(jnp.mean(o * o, axis=-1, keepdims=True) + RMS_EPS) * nw
        o_ref[b, rows, :] = (o * _silu(g)).astype(o_ref.dtype)
        return carry

    lax.fori_loop(0, bb * n_sub, body, 0)

    @pl.when(ci == pl.num_programs(2) - 1)
    def _():
        for b in range(bb):
            sout_ref[b, 0] = st_ref[b].T


def _hgrn(z_a, lb_raw, norm_w, s0, layer, bb, tc, sc):
    bsz, t, _ = z_a.shape
    col = lambda k: (lambda b, h, c: (b, c, k * H_A + h))
    return pl.pallas_call(
        functools.partial(_hgrn_kernel, layer=layer, sc=sc, n_sub=tc // sc, bb=bb),
        grid=(bsz // bb, H_A, t // tc),
        in_specs=[pl.BlockSpec((bb, tc, DK_A), col(0)), pl.BlockSpec((bb, tc, DK_A), col(1)),
                  pl.BlockSpec((bb, tc, DV_A), col(2)), pl.BlockSpec((bb, tc, DV_A), col(3)),
                  pl.BlockSpec((N_EVEN, DK_A), lambda b, h, c: (0, h)),
                  pl.BlockSpec((1, 1, DV_A), lambda b, h, c: (h, 0, 0)),
                  pl.BlockSpec((bb, 1, DK_A, DV_A), lambda b, h, c: (b, h, 0, 0))],
        out_specs=[pl.BlockSpec((bb, tc, DV_A), lambda b, h, c: (b, c, h)),
                   pl.BlockSpec((bb, 1, DK_A, DV_A), lambda b, h, c: (b, h, 0, 0))],
        out_shape=[jax.ShapeDtypeStruct((bsz, t, D_A), BF16),
                   jax.ShapeDtypeStruct(s0.shape, F32)],
        scratch_shapes=[pltpu.VMEM((bb, DV_A, DK_A), F32)],
        compiler_params=_params("parallel", "parallel", "arbitrary"),
        name="hgrn2",
    )(z_a, z_a, z_a, z_a, lb_raw, norm_w.reshape(H_A, 1, DV_A), s0)


def _rwkv_kernel(z_ref, sh0_ref, s0_ref, mu_ref, w0_ref, w2_ref, a0_ref, a2_ref, g2_ref, kk_ref, ka_ref,
                 rk_ref, lnw_ref, lnb_ref, o_ref, sout_ref, shout_ref, s_ref, prev_ref, *, c):
    ci = pl.program_id(1)

    @pl.when(ci == 0)
    def _():
        s_ref[...] = s0_ref[0]
        prev_ref[...] = sh0_ref[0]

    zb = z_ref[0]
    row = lax.broadcasted_iota(jnp.int32, zb.shape, 0)
    prev = jnp.where(row == 0, prev_ref[...], pltpu.roll(zb, 1, axis=0))
    prev_ref[...] = zb[c - 1:c, :]
    zs = zb + (prev - zb) * mu_ref[...]
    r = zs[:, 0:D_B]
    kb = zs[:, D_B:2 * D_B]
    v = zs[:, 2 * D_B:3 * D_B]
    o0 = 3 * D_B
    wl = zs[:, o0:o0 + R_W]
    al = zs[:, o0 + R_W:o0 + R_W + R_A]
    gl = zs[:, o0 + R_W + R_A:o0 + R_W + R_A + R_G]
    wlin = w0_ref[...] + _dot(jnp.tanh(wl).astype(BF16), w2_ref[...])
    w_log = -math.exp(-0.5) * _sigmoid(wlin)
    a = _sigmoid(a0_ref[...] + _dot(al.astype(BF16), a2_ref[...]))
    gate = _dot(_sigmoid(gl).astype(BF16), g2_ref[...])
    kkr = kb * kk_ref[...]
    kmod = kb * (1.0 + (a - 1.0) * ka_ref[...])
    bonus_in = r * kmod * rk_ref[...]

    tri = _lower_tri(c, strict=False).astype(F32)
    bc = _dot32(tri, w_log)
    blast = bc[c - 1:c, :]
    gam = jnp.exp(bc)
    inv_gam = jnp.exp(-bc)
    gam_prev = jnp.exp(bc - w_log)
    to_end = jnp.exp(blast - bc)
    gam_last = jnp.exp(blast)
    r_t = r * gam

    strict = _lower_tri(c, strict=True)
    incl = _lower_tri(c, strict=False)
    n_steps = max(1, int(math.log2(c)))
    outs = []
    for h in range(H_B):
        sl = slice(h * N_B, (h + 1) * N_B)
        kk_h = kkr[:, sl]
        kk_h = kk_h / jnp.maximum(jnp.sqrt(jnp.sum(kk_h * kk_h, axis=-1, keepdims=True)), 1e-12)
        beta_h = kk_h * a[:, sl]
        xa = -kk_h * gam_prev[:, sl]
        xr = r_t[:, sl]
        yb = beta_h * inv_gam[:, sl]
        yk = kmod[:, sl] * inv_gam[:, sl]
        v_h = v[:, sl]
        s0 = s_ref[h]
        a_ab = jnp.where(strict, _dot32_nt(xa, yb), 0.0)
        a_ak = jnp.where(strict, _dot32_nt(xa, yk), 0.0)
        a_rb = jnp.where(incl, _dot32_nt(xr, yb), 0.0)
        a_rk = jnp.where(incl, _dot32_nt(xr, yk), 0.0)
        u = _dot32_nt(xa, s0) + _dot32(a_ak, v_h)
        p = a_ab
        for it in range(n_steps):
            u = u + _dot32(p, u)
            if it + 1 < n_steps:
                p = _dot32(p, p)
        y = _dot32_nt(xr, s0) + _dot32(a_rb, u) + _dot32(a_rk, v_h)
        s_ref[h] = (s0 * gam_last[:, sl] + _dot32_tn(u, beta_h * to_end[:, sl])
                    + _dot32_tn(v_h, kmod[:, sl] * to_end[:, sl]))
        mean = jnp.mean(y, axis=-1, keepdims=True)
        yc = y - mean
        var = jnp.mean(yc * yc, axis=-1, keepdims=True)
        yn = yc * lax.rsqrt(var + GN_EPS) * lnw_ref[:, sl] + lnb_ref[:, sl]
        outs.append(yn + jnp.sum(bonus_in[:, sl], axis=-1, keepdims=True) * v_h)
    o_ref[0] = (jnp.concatenate(outs, axis=-1) * gate).astype(o_ref.dtype)

    @pl.when(ci == pl.num_programs(1) - 1)
    def _():
        sout_ref[0] = s_ref[...]
        shout_ref[0] = zb[c - 1:c, :]


def _rwkv(z_b, s0, shift0, p, j, c):
    bsz, t, dz = z_b.shape
    vec = lambda x: x.reshape(1, -1)
    fixed2 = lambda b, ci: (0, 0)
    vspec = lambda n: pl.BlockSpec((1, n), fixed2)
    return pl.pallas_call(
        functools.partial(_rwkv_kernel, c=c),
        grid=(bsz, t // c),
        in_specs=[pl.BlockSpec((1, c, dz), lambda b, ci: (b, ci, 0)),
                  pl.BlockSpec((1, 1, dz), lambda b, ci: (b, 0, 0)),
                  pl.BlockSpec((1, H_B, N_B, N_B), lambda b, ci: (b, 0, 0, 0)),
                  vspec(dz), vspec(D_B),
                  pl.BlockSpec((R_W, D_B), fixed2), vspec(D_B),
                  pl.BlockSpec((R_A, D_B), fixed2), pl.BlockSpec((R_G, D_B), fixed2),
                  vspec(D_B), vspec(D_B), vspec(D_B), vspec(D_B), vspec(D_B)],
        out_specs=[pl.BlockSpec((1, c, D_B), lambda b, ci: (b, ci, 0)),
                   pl.BlockSpec((1, H_B, N_B, N_B), lambda b, ci: (b, 0, 0, 0)),
                   pl.BlockSpec((1, 1, dz), lambda b, ci: (b, 0, 0))],
        out_shape=[jax.ShapeDtypeStruct((bsz, t, D_B), BF16),
                   jax.ShapeDtypeStruct(s0.shape, F32),
                   jax.ShapeDtypeStruct((bsz, 1, dz), F32)],
        scratch_shapes=[pltpu.VMEM((H_B, N_B, N_B), F32), pltpu.VMEM((1, dz), F32)],
        compiler_params=_params("parallel", "arbitrary"),
        name="rwkv7",
    )(z_b, shift0.reshape(bsz, 1, dz), s0, vec(p['rwkv_mu'][j]), vec(p['rwkv_w0'][j]),
      p['rwkv_w2'][j].astype(BF16), vec(p['rwkv_a0'][j]), p['rwkv_a2'][j].astype(BF16),
      p['rwkv_g2'][j].astype(BF16), vec(p['rwkv_k_k'][j]), vec(p['rwkv_k_a'][j]),
      vec(p['rwkv_r_k'][j]), vec(p['rwkv_ln_w'][j]), vec(p['rwkv_ln_b'][j]))


def _ret_kernel(q_ref, k_ref, v_ref, g_ref, ang_ref, s0_ref, o_ref, sout_ref, s_ref, *, c, pos0):
    h = pl.program_id(1)
    ci = pl.program_id(2)

    @pl.when(ci == 0)
    def _():
        s_ref[...] = s0_ref[0, 0]

    rowi = lax.broadcasted_iota(jnp.int32, (c, DK_C), 0)
    lane = lax.broadcasted_iota(jnp.int32, (c, DK_C), 1)
    pos = (pos0 + ci * c + rowi).astype(F32)
    ang = pos * ang_ref[...]
    sin = jnp.sin(ang)
    cos = jnp.cos(ang)
    even = (lane % 2) == 0

    def rope(x):
        rot = jnp.where(even, -pltpu.roll(x, DK_C - 1, axis=1), pltpu.roll(x, 1, axis=1))
        return x * cos + rot * sin

    qh = rope(q_ref[0])
    kh = rope(k_ref[0]) * (DK_C ** -0.5)
    v = v_ref[0].astype(BF16)

    hv = jnp.zeros((1, 1), F32) + h.astype(F32)
    lg = jnp.log1p(-jnp.exp2(-5.0 - hv))
    tcol = lax.broadcasted_iota(jnp.int32, (c, 1), 0).astype(F32)
    bcol = (tcol + 1.0) * lg
    blast = float(c) * lg
    ti = lax.broadcasted_iota(jnp.int32, (c, c), 0)
    si = lax.broadcasted_iota(jnp.int32, (c, c), 1)
    causal = ti >= si
    dec = jnp.where(causal, jnp.exp(jnp.where(causal, (ti - si).astype(F32) * lg, 0.0)), 0.0)
    att = _dot_nt(qh.astype(BF16), kh.astype(BF16)) * dec
    s = s_ref[...]
    o = _dot(att.astype(BF16), v) + _dot((qh * jnp.exp(bcol)).astype(BF16), s.astype(BF16))
    s_new = jnp.exp(blast) * s + _dot_tn((kh * jnp.exp(blast - bcol)).astype(BF16), v)
    s_ref[...] = s_new
    o = o * lax.rsqrt(jnp.mean(o * o, axis=-1, keepdims=True) + RMS_EPS)
    o_ref[0] = (o * _silu(g_ref[0])).astype(o_ref.dtype)

    @pl.when(ci == pl.num_programs(2) - 1)
    def _():
        sout_ref[0, 0] = s_new


def _retention(z, s0, pos0, c):
    bsz, t, _ = z.shape
    angle = 1.0 / (10000.0 ** jnp.linspace(0.0, 1.0, DK_C // 2, dtype=F32))
    angle = jnp.repeat(angle, 2).reshape(1, DK_C)
    col = lambda k: (lambda b, h, ci: (b, ci, k * H_C + h))
    return pl.pallas_call(
        functools.partial(_ret_kernel, c=c, pos0=pos0),
        grid=(bsz, H_C, t // c),
        in_specs=[pl.BlockSpec((1, c, DK_C), col(0)), pl.BlockSpec((1, c, DK_C), col(1)),
                  pl.BlockSpec((1, c, DV_C), col(2)), pl.BlockSpec((1, c, DV_C), col(3)),
                  pl.BlockSpec((1, DK_C), lambda b, h, ci: (0, 0)),
                  pl.BlockSpec((1, 1, DK_C, DV_C), lambda b, h, ci: (b, h, 0, 0))],
        out_specs=[pl.BlockSpec((1, c, DV_C), lambda b, h, ci: (b, ci, h)),
                   pl.BlockSpec((1, 1, DK_C, DV_C), lambda b, h, ci: (b, h, 0, 0))],
        out_shape=[jax.ShapeDtypeStruct((bsz, t, D_C), BF16), jax.ShapeDtypeStruct(s0.shape, F32)],
        scratch_shapes=[pltpu.VMEM((DK_C, DV_C), F32)],
        compiler_params=_params("parallel", "parallel", "arbitrary"),
        name="retention",
    )(z, z, z, z, angle, s0)


def _rglru_kernel(y_ref, x_ref, conv0_ref, h0_ref, cw_ref, cb_ref, wa_ref, wx_ref, ba_ref, bx_ref, lam_ref,
                  o_ref, convout_ref, hout_ref, tail_ref, h_ref, *, tc, pos0):
    ci = pl.program_id(1)

    @pl.when(ci == 0)
    def _():
        tail_ref[...] = conv0_ref[0]
        h_ref[...] = h0_ref[0]

    x = x_ref[0]
    tail = tail_ref[...]
    row = lax.broadcasted_iota(jnp.int32, x.shape, 0)

    def shifted(d):
        r = pltpu.roll(x, d, axis=0)
        for m in range(d):
            r = jnp.where(row == m, tail[CONV_W - 1 - d + m:CONV_W - d + m, :], r)
        return r

    cw = cw_ref[...]
    xc = cb_ref[...] + cw[CONV_W - 1:CONV_W, :] * x
    for d in range(1, CONV_W):
        xc = xc + cw[CONV_W - 1 - d:CONV_W - d, :] * shifted(d)
    new_tail = jnp.concatenate([tail, x], axis=0)[tc:tc + CONV_W - 1, :]
    tail_ref[...] = new_tail

    ra, rx = [], []
    for hd in range(H_D):
        xh = xc[:, hd * BW_D:(hd + 1) * BW_D].astype(BF16)
        ra.append(_dot(xh, wa_ref[hd]))
        rx.append(_dot(xh, wx_ref[hd]))
    rg = _sigmoid(jnp.concatenate(ra, axis=-1) + ba_ref[...])
    ig = _sigmoid(jnp.concatenate(rx, axis=-1) + bx_ref[...])
    nlam = -lam_ref[...]
    softplus = jnp.maximum(nlam, 0.0) + jnp.log1p(jnp.exp(-jnp.abs(nlam)))
    log_a = -LRU_C * rg * softplus
    a = jnp.exp(log_a)
    mult = jnp.sqrt(-jnp.tanh(log_a) * (a * a + 1.0))
    pos = pos0 + ci * tc + row
    mult = jnp.where(pos == 0, 1.0, mult)
    bt = mult * ig * xc

    d = 1
    while d < tc:
        m = row >= d
        bt = jnp.where(m, a * pltpu.roll(bt, d, axis=0) + bt, bt)
        a = jnp.where(m, a * pltpu.roll(a, d, axis=0), a)
        d *= 2
    hseq = bt + a * h_ref[...]
    h_last = hseq[tc - 1:tc, :]
    h_ref[...] = h_last
    y = y_ref[0]
    gelu = 0.5 * y * (1.0 + jnp.tanh(math.sqrt(2.0 / math.pi) * (y + 0.044715 * (y * y * y))))
    o_ref[0] = (hseq * gelu).astype(o_ref.dtype)

    @pl.when(ci == pl.num_programs(1) - 1)
    def _():
        convout_ref[0] = new_tail
        hout_ref[0] = h_last


def _rglru(z, conv0, h0, p, j, pos0, tc):
    bsz, t, _ = z.shape
    vec = lambda x: x.reshape(1, -1)
    fixed2 = lambda b, ci: (0, 0)
    fixed3 = lambda b, ci: (0, 0, 0)
    vspec = pl.BlockSpec((1, D_D), fixed2)
    ycol = 4 * D_C // D_D
    return pl.pallas_call(
        functools.partial(_rglru_kernel, tc=tc, pos0=pos0),
        grid=(bsz, t // tc),
        in_specs=[pl.BlockSpec((1, tc, D_D), lambda b, ci: (b, ci, ycol)),
                  pl.BlockSpec((1, tc, D_D), lambda b, ci: (b, ci, ycol + 1)),
                  pl.BlockSpec((1, CONV_W - 1, D_D), lambda b, ci: (b, 0, 0)),
                  pl.BlockSpec((1, 1, D_D), lambda b, ci: (b, 0, 0)),
                  pl.BlockSpec((CONV_W, D_D), fixed2), vspec,
                  pl.BlockSpec((H_D, BW_D, BW_D), fixed3), pl.BlockSpec((H_D, BW_D, BW_D), fixed3),
                  vspec, vspec, vspec],
        out_specs=[pl.BlockSpec((1, tc, D_D), lambda b, ci: (b, ci, 0)),
                   pl.BlockSpec((1, CONV_W - 1, D_D), lambda b, ci: (b, 0, 0)),
                   pl.BlockSpec((1, 1, D_D), lambda b, ci: (b, 0, 0))],
        out_shape=[jax.ShapeDtypeStruct((bsz, t, D_D), BF16),
                   jax.ShapeDtypeStruct((bsz, CONV_W - 1, D_D), F32),
                   jax.ShapeDtypeStruct((bsz, 1, D_D), F32)],
        scratch_shapes=[pltpu.VMEM((CONV_W - 1, D_D), F32), pltpu.VMEM((1, D_D), F32)],
        compiler_params=_params("parallel", "arbitrary"),
        name="rglru",
    )(z, z, conv0, h0.reshape(bsz, 1, D_D), p['conv_w'][j], vec(p['conv_b'][j]),
      p['rglru_wa'][j].astype(BF16), p['rglru_wx'][j].astype(BF16),
      vec(p['rglru_ba'][j]), vec(p['rglru_bx'][j]), vec(p['rglru_lambda'][j]))


def _tiles(bsz, t):
    if t >= 256:
        return dict(hgrn=(1, 256, 16), rwkv=64, ret=128, lru=256)
    return dict(hgrn=(math.gcd(bsz, 16), t, min(t, 16)), rwkv=t, ret=t, lru=t)


def _run_trunk(x, pos0, st_hgrn, st_rwkv, st_shift, st_ret, st_conv, st_h, p, wb):
    bsz, t, d = x.shape
    m = bsz * t
    tl = _tiles(bsz, t)
    x2 = x.reshape(m, d)
    hn = _rmsnorm_bf16(x2, p['norm_mix_pre'][0], tm=min(512, m))
    n_hgrn, n_rwkv, n_shift, n_ret, n_conv, n_h = [], [], [], [], [], []
    for l in range(DEPTH):
        j = l // 2
        if l % 2 == 0:
            w_in = wb['w_in_even'][j]
            z_a = _matmul(hn, w_in[:, :4 * D_A], tn=1024).reshape(bsz, t, 4 * D_A)
            z_b = _matmul(hn, w_in[:, 4 * D_A:], tn=D_RWKV_PROJ // 2).reshape(bsz, t, D_RWKV_PROJ)
            bb, tc, sc = tl['hgrn']
            o_a, sa = _hgrn(z_a, p['hgrn_lb_raw'], p['hgrn_norm_w'][j], st_hgrn[j], j, bb, tc, sc)
            o_b, sb, sh = _rwkv(z_b, st_rwkv[j], st_shift[j], p, j, tl['rwkv'])
            n_hgrn.append(sa)
            n_rwkv.append(sb)
            n_shift.append(sh.reshape(bsz, D_RWKV_PROJ))
            w_out = wb['w_out_even'][j]
        else:
            z = _matmul(hn, wb['w_in_odd'][j], tn=1024).reshape(bsz, t, 4 * D_C + 2 * D_D)
            o_a, sc_new = _retention(z, st_ret[j], pos0, tl['ret'])
            o_b, scv, shh = _rglru(z, st_conv[j], st_h[j], p, j, pos0, tl['lru'])
            n_ret.append(sc_new)
            n_conv.append(scv)
            n_h.append(shh.reshape(bsz, D_D))
            w_out = wb['w_out_odd'][j]
        x2, hn = _outproj(o_a.reshape(m, -1), o_b.reshape(m, -1), w_out, x2,
                          p['norm_mix_post'][l], p['norm_ffn_pre'][l], tm=min(512, m))
        next_pre = p['norm_mix_pre'][(l + 1) % DEPTH]
        x2, hn = _ffn(hn, wb['w_ffn_in'][l], wb['w_ffn_out'][l], x2, p['norm_ffn_post'][l], next_pre,
                      tm=min(512, m))
    return x2.reshape(bsz, t, d), (jnp.stack(n_hgrn), jnp.stack(n_rwkv), jnp.stack(n_shift),
                                   jnp.stack(n_ret), jnp.stack(n_conv), jnp.stack(n_h))


def _zero_state(s, batch):
    return jnp.zeros((s.shape[0], batch) + s.shape[2:], s.dtype)


def kernel(x_prompt, x_sample, state_hgrn, state_rwkv, state_rwkv_shift, state_ret, state_rglru_conv, state_rglru_h, norm_mix_pre, norm_mix_post, norm_ffn_pre, norm_ffn_post, w_in_even, w_out_even, hgrn_lb_raw, hgrn_norm_w, rwkv_mu, rwkv_w0, rwkv_w2, rwkv_a0, rwkv_a2, rwkv_g2, rwkv_k_k, rwkv_k_a, rwkv_r_k, rwkv_ln_w, rwkv_ln_b, w_in_odd, w_out_odd, conv_w, conv_b, rglru_wa, rglru_ba, rglru_wx, rglru_bx, rglru_lambda, w_ffn_in, w_ffn_out):
    p = dict(norm_mix_pre=norm_mix_pre, norm_mix_post=norm_mix_post, norm_ffn_pre=norm_ffn_pre,
             norm_ffn_post=norm_ffn_post, hgrn_lb_raw=hgrn_lb_raw, hgrn_norm_w=hgrn_norm_w,
             rwkv_mu=rwkv_mu, rwkv_w0=rwkv_w0, rwkv_w2=rwkv_w2, rwkv_a0=rwkv_a0, rwkv_a2=rwkv_a2,
             rwkv_g2=rwkv_g2, rwkv_k_k=rwkv_k_k, rwkv_k_a=rwkv_k_a, rwkv_r_k=rwkv_r_k,
             rwkv_ln_w=rwkv_ln_w, rwkv_ln_b=rwkv_ln_b, conv_w=conv_w, conv_b=conv_b,
             rglru_wa=rglru_wa, rglru_ba=rglru_ba, rglru_wx=rglru_wx, rglru_bx=rglru_bx,
             rglru_lambda=rglru_lambda)
    wb = dict(w_in_even=w_in_even.astype(BF16), w_out_even=w_out_even.astype(BF16),
              w_in_odd=w_in_odd.astype(BF16), w_out_odd=w_out_odd.astype(BF16),
              w_ffn_in=w_ffn_in.astype(BF16), w_ffn_out=w_ffn_out.astype(BF16))
    bp = x_prompt.shape[0]
    y_prompt, st_p = _run_trunk(x_prompt, 0,
                                _zero_state(state_hgrn, bp), _zero_state(state_rwkv, bp),
                                _zero_state(state_rwkv_shift, bp), _zero_state(state_ret, bp),
                                _zero_state(state_rglru_conv, bp), _zero_state(state_rglru_h, bp), p, wb)
    y_sample, st_s = _run_trunk(x_sample, PAST_LEN, state_hgrn, state_rwkv, state_rwkv_shift,
                                state_ret, state_rglru_conv, state_rglru_h, p, wb)
    hgrn_p, rwkv_p, shift_p, ret_p, conv_p, h_p = st_p
    hgrn_s, rwkv_s, shift_s, ret_s, conv_s, h_s = st_s
    return (y_prompt, y_sample, hgrn_p, hgrn_s, rwkv_p, rwkv_s, shift_p, shift_s,
            ret_p, ret_s, conv_p, conv_s, h_p, h_s)
```

```python
import functools
import math

import jax
import jax.numpy as jnp
from jax import lax
from jax.experimental import pallas as pl
from jax.experimental.pallas import tpu as pltpu

F32 = jnp.float32
BF16 = jnp.bfloat16
HIGHEST = lax.Precision.HIGHEST

D_MODEL = 2048
DEPTH = 4
N_EVEN = (DEPTH + 1) // 2
N_ODD = DEPTH // 2
D_A = D_MODEL // 2
H_A = 8
DK_A = 128
DV_A = 128
D_B = D_MODEL // 2
N_B = 64
H_B = D_B // N_B
R_W = 64
R_A = 64
R_G = 128
D_RWKV_PROJ = 3 * D_B + R_W + R_A + R_G
D_C = D_MODEL // 2
H_C = 4
DK_C = D_C // H_C
DV_C = D_C // H_C
D_D = D_MODEL // 2
H_D = 4
BW_D = D_D // H_D
CONV_W = 4
LRU_C = 8.0
D_FF = ((8 * D_MODEL + 3 * 256 - 1) // (3 * 256)) * 256
RMS_EPS = 1e-6
GN_EPS = 64e-5
TINY = 1e-30
PAST_LEN = 16384

VMEM_LIMIT_BYTES = 56 * 1024 * 1024
LANES = 128


def _params(*sem):
    return pltpu.CompilerParams(dimension_semantics=sem, vmem_limit_bytes=VMEM_LIMIT_BYTES)


def _rms(x, w):
    return x * lax.rsqrt(jnp.mean(x * x, axis=-1, keepdims=True) + RMS_EPS) * w


def _sigmoid(x):
    return jax.nn.sigmoid(x)


def _silu(x):
    return x * jax.nn.sigmoid(x)


def _dot(a, b):
    return jnp.dot(a, b, preferred_element_type=F32)


def _dot_nt(a, b):
    return lax.dot_general(a, b, (((1,), (1,)), ((), ())), preferred_element_type=F32)


def _dot_tn(a, b):
    return lax.dot_general(a, b, (((0,), (0,)), ((), ())), preferred_element_type=F32)


def _dot32(a, b):
    return jnp.dot(a, b, preferred_element_type=F32, precision=HIGHEST)


def _dot32_nt(a, b):
    return lax.dot_general(a, b, (((1,), (1,)), ((), ())), preferred_element_type=F32, precision=HIGHEST)


def _dot32_tn(a, b):
    return lax.dot_general(a, b, (((0,), (0,)), ((), ())), preferred_element_type=F32, precision=HIGHEST)


def _lower_tri(n, strict):
    r = lax.broadcasted_iota(jnp.int32, (n, n), 0)
    c = lax.broadcasted_iota(jnp.int32, (n, n), 1)
    return (r > c) if strict else (r >= c)


def _rmsnorm_kernel(x_ref, w_ref, o_ref):
    o_ref[...] = _rms(x_ref[...], w_ref[...]).astype(o_ref.dtype)


def _rmsnorm_bf16(x, w, tm=512):
    m, d = x.shape
    return pl.pallas_call(
        _rmsnorm_kernel,
        grid=(m // tm,),
        in_specs=[pl.BlockSpec((tm, d), lambda i: (i, 0)), pl.BlockSpec((1, d), lambda i: (0, 0))],
        out_specs=pl.BlockSpec((tm, d), lambda i: (i, 0)),
        out_shape=jax.ShapeDtypeStruct((m, d), BF16),
        compiler_params=_params("parallel"),
        name="rmsnorm",
    )(x, w.reshape(1, d))


def _matmul_kernel(a_ref, b_ref, o_ref):
    o_ref[...] = _dot(a_ref[...], b_ref[...])


def _matmul(a, b, tn, tm=1024):
    m, k = a.shape
    n = b.shape[1]
    tm = min(tm, m)
    return pl.pallas_call(
        _matmul_kernel,
        grid=(m // tm, n // tn),
        in_specs=[pl.BlockSpec((tm, k), lambda i, j: (i, 0)), pl.BlockSpec((k, tn), lambda i, j: (0, j))],
        out_specs=pl.BlockSpec((tm, tn), lambda i, j: (i, j)),
        out_shape=jax.ShapeDtypeStruct((m, n), F32),
        compiler_params=_params("parallel", "parallel"),
        name="in_proj",
    )(a, b)


def _outproj_kernel(oa_ref, ob_ref, wa_ref, wb_ref, x_ref, post_ref, pre_ref, xo_ref, hn_ref):
    mix = _dot(oa_ref[...], wa_ref[...]) + _dot(ob_ref[...], wb_ref[...])
    xn = x_ref[...] + _rms(mix, post_ref[...])
    xo_ref[...] = xn
    hn_ref[...] = _rms(xn, pre_ref[...]).astype(hn_ref.dtype)


def _outproj(oa, ob, w_out, x, post_w, pre_w, tm=512):
    m, d = x.shape
    ka, kb = oa.shape[1], ob.shape[1]
    row = lambda i: (i, 0)
    fixed = lambda i: (0, 0)
    return pl.pallas_call(
        _outproj_kernel,
        grid=(m // tm,),
        in_specs=[pl.BlockSpec((tm, ka), row), pl.BlockSpec((tm, kb), row),
                  pl.BlockSpec((ka, d), fixed), pl.BlockSpec((kb, d), lambda i: (1, 0)),
                  pl.BlockSpec((tm, d), row), pl.BlockSpec((1, d), fixed), pl.BlockSpec((1, d), fixed)],
        out_specs=[pl.BlockSpec((tm, d), row), pl.BlockSpec((tm, d), row)],
        out_shape=[jax.ShapeDtypeStruct((m, d), F32), jax.ShapeDtypeStruct((m, d), BF16)],
        compiler_params=_params("parallel"),
        name="out_proj",
    )(oa, ob, w_out, w_out, x, post_w.reshape(1, d), pre_w.reshape(1, d))


def _ffn_kernel(h_ref, wg_ref, wu_ref, wo_ref, x_ref, post_ref, pre_ref, xo_ref, hn_ref, acc_ref):
    j = pl.program_id(1)

    @pl.when(j == 0)
    def _():
        acc_ref[...] = jnp.zeros_like(acc_ref)

    h = h_ref[...]
    gate = _dot(h, wg_ref[...])
    up = _dot(h, wu_ref[...])
    act = (_silu(gate) * up).astype(BF16)
    acc_ref[...] += _dot(act, wo_ref[...])

    @pl.when(j == pl.num_programs(1) - 1)
    def _():
        xn = x_ref[...] + _rms(acc_ref[...], post_ref[...])
        xo_ref[...] = xn
        hn_ref[...] = _rms(xn, pre_ref[...]).astype(hn_ref.dtype)


def _ffn(hn, w_in, w_out, x, post_w, pre_w, tm=512, tf=512):
    m, d = x.shape
    nf = D_FF // tf
    row = lambda i, j: (i, 0)
    fixed = lambda i, j: (0, 0)
    return pl.pallas_call(
        _ffn_kernel,
        grid=(m // tm, nf),
        in_specs=[pl.BlockSpec((tm, d), row),
                  pl.BlockSpec((d, tf), lambda i, j: (0, j)),
                  pl.BlockSpec((d, tf), lambda i, j: (0, j + nf)),
                  pl.BlockSpec((tf, d), lambda i, j: (j, 0)),
                  pl.BlockSpec((tm, d), row), pl.BlockSpec((1, d), fixed), pl.BlockSpec((1, d), fixed)],
        out_specs=[pl.BlockSpec((tm, d), row), pl.BlockSpec((tm, d), row)],
        out_shape=[jax.ShapeDtypeStruct((m, d), F32), jax.ShapeDtypeStruct((m, d), BF16)],
        scratch_shapes=[pltpu.VMEM((tm, d), F32)],
        compiler_params=_params("parallel", "arbitrary"),
        name="ffn",
    )(hn, w_in, w_in, w_out, x, post_w.reshape(1, d), pre_w.reshape(1, d))


def _hgrn_kernel(q_ref, f_ref, i_ref, g_ref, lbraw_ref, nw_ref, s0_ref, o_ref, sout_ref, st_ref,
                 *, layer, sc, n_sub, bb):
    ci = pl.program_id(1)

    @pl.when(ci == 0)
    def _():
        for b in range(bb):
            for h in range(H_A):
                st_ref[b, h] = s0_ref[b, h].T

    raw = lbraw_ref[...]
    e = jnp.exp(raw - jnp.max(raw, axis=0, keepdims=True))
    pr = e / jnp.sum(e, axis=0, keepdims=True)
    lb_all = jnp.zeros((1, D_A), F32)
    for r in range(1, layer + 1):
        lb_all = lb_all + pr[r:r + 1, :]
    nw_all = nw_ref[...]
    tri = _lower_tri(sc, strict=False).astype(F32)
    row = lax.broadcasted_iota(jnp.int32, (sc, LANES), 0)

    oml = 1.0 - lb_all
    heads = [slice(h * DK_A, (h + 1) * DK_A) for h in range(H_A)]

    def body(idx, carry):
        b = idx // n_sub
        rows = pl.ds(pl.multiple_of((idx % n_sub) * sc, sc), sc)
        zf = f_ref[b, rows, :]
        iv = i_ref[b, rows, :]
        fg = lb_all + oml * _sigmoid(zf)
        logf = jnp.log(jnp.maximum(fg, TINY))
        kk = oml * _sigmoid(-zf)
        qs = _silu(q_ref[b, rows, :])
        bc = _dot32(tri, logf)
        blast = bc[sc - 1:sc, :]
        qe = (qs * jnp.exp(bc)).astype(BF16)
        kh = (kk * jnp.exp(blast - bc)).astype(BF16)
        ivb = iv.astype(BF16)
        keep = jnp.exp(blast)
        sts = [st_ref[b, h] for h in range(H_A)]
        o_inter = [_dot_nt(qe[:, sl], sts[h].astype(BF16)) for h, sl in enumerate(heads)]
        upd = [_dot_tn(ivb[:, sl], kh[:, sl]) for sl in heads]
        for h, sl in enumerate(heads):
            st_ref[b, h] = sts[h] * keep[:, sl] + upd[h]
        outs = []
        for h, sl in enumerate(heads):
            bc_h, kk_h, qs_h, iv_h = bc[:, sl], kk[:, sl], qs[:, sl], iv[:, sl]
            o_rows = []
            for t in range(sc):
                m = row <= t
                dec = jnp.where(m, jnp.exp(jnp.where(m, bc_h[t:t + 1, :] - bc_h, 0.0)), 0.0)
                a_col = jnp.sum(kk_h * dec * qs_h[t:t + 1, :], axis=-1, keepdims=True)
                o_rows.append(jnp.sum(a_col * iv_h, axis=0, keepdims=True))
            o = jnp.concatenate(o_rows, axis=0) + o_inter[h]
            outs.append(o * lax.rsqrt(jnp.mean(o * o, axis=-1, keepdims=True) + RMS_EPS))
        o = jnp.concatenate(outs, axis=-1) * nw_all
        o_ref[b, rows, :] = (o * _silu(g_ref[b, rows, :])).astype(o_ref.dtype)
        return carry

    lax.fori_loop(0, bb * n_sub, body, 0)

    @pl.when(ci == pl.num_programs(1) - 1)
    def _():
        for b in range(bb):
            for h in range(H_A):
                sout_ref[b, h] = st_ref[b, h].T


def _stacked_state_call(kernel_fn, prev, state_out_index, in_specs, args, **kw):
    if prev is None:
        return pl.pallas_call(kernel_fn, in_specs=in_specs, **kw)(*args)

    def with_prev(prev_ref, *refs):
        del prev_ref
        kernel_fn(*refs)

    return pl.pallas_call(with_prev, in_specs=[pl.BlockSpec(memory_space=pl.ANY)] + in_specs,
                          input_output_aliases={0: state_out_index}, **kw)(prev, *args)


def _hgrn(z_a, lb_raw, norm_w, s_all, prev, layer, bb, tc, sc):
    bsz, t, _ = z_a.shape
    col = lambda k: (lambda b, c: (b, c, k))
    fixed2 = lambda b, c: (0, 0)
    state = pl.BlockSpec((None, bb, H_A, DK_A, DV_A), lambda b, c: (layer, b, 0, 0, 0))
    return _stacked_state_call(
        functools.partial(_hgrn_kernel, layer=layer, sc=sc, n_sub=tc // sc, bb=bb), prev, 1,
        [pl.BlockSpec((bb, tc, D_A), col(0)), pl.BlockSpec((bb, tc, D_A), col(1)),
         pl.BlockSpec((bb, tc, D_A), col(2)), pl.BlockSpec((bb, tc, D_A), col(3)),
         pl.BlockSpec((N_EVEN, D_A), fixed2), pl.BlockSpec((1, D_A), fixed2), state],
        (z_a, z_a, z_a, z_a, lb_raw, norm_w.reshape(1, D_A), s_all),
        grid=(bsz // bb, t // tc),
        out_specs=[pl.BlockSpec((bb, tc, D_A), lambda b, c: (b, c, 0)), state],
        out_shape=[jax.ShapeDtypeStruct((bsz, t, D_A), BF16),
                   jax.ShapeDtypeStruct(s_all.shape, F32)],
        scratch_shapes=[pltpu.VMEM((bb, H_A, DV_A, DK_A), F32)],
        compiler_params=_params("parallel", "arbitrary"),
        name="hgrn2",
    )


def _rwkv_kernel(z_ref, sh0_ref, s0_ref, mu_ref, w0_ref, w2_ref, a0_ref, a2_ref, g2_ref, kk_ref, ka_ref,
                 rk_ref, lnw_ref, lnb_ref, o_ref, sout_ref, shout_ref, s_ref, prev_ref, *, c):
    ci = pl.program_id(1)

    @pl.when(ci == 0)
    def _():
        s_ref[...] = s0_ref[0]
        prev_ref[...] = sh0_ref[0]

    zb = z_ref[0]
    row = lax.broadcasted_iota(jnp.int32, zb.shape, 0)
    prev = jnp.where(row == 0, prev_ref[...], pltpu.roll(zb, 1, axis=0))
    prev_ref[...] = zb[c - 1:c, :]
    zs = zb + (prev - zb) * mu_ref[...]
    r = zs[:, 0:D_B]
    kb = zs[:, D_B:2 * D_B]
    v = zs[:, 2 * D_B:3 * D_B]
    o0 = 3 * D_B
    wl = zs[:, o0:o0 + R_W]
    al = zs[:, o0 + R_W:o0 + R_W + R_A]
    gl = zs[:, o0 + R_W + R_A:o0 + R_W + R_A + R_G]
    wlin = w0_ref[...] + _dot(jnp.tanh(wl).astype(BF16), w2_ref[...])
    w_log = -math.exp(-0.5) * _sigmoid(wlin)
    a = _sigmoid(a0_ref[...] + _dot(al.astype(BF16), a2_ref[...]))
    gate = _dot(_sigmoid(gl).astype(BF16), g2_ref[...])
    kkr = kb * kk_ref[...]
    kmod = kb * (1.0 + (a - 1.0) * ka_ref[...])
    bonus_in = r * kmod * rk_ref[...]

    tri = _lower_tri(c, strict=False).astype(F32)
    bc = _dot32(tri, w_log)
    blast = bc[c - 1:c, :]
    gam = jnp.exp(bc)
    inv_gam = jnp.exp(-bc)
    gam_prev = jnp.exp(bc - w_log)
    to_end = jnp.exp(blast - bc)
    gam_last = jnp.exp(blast)
    seg = (lax.broadcasted_iota(jnp.int32, (LANES, LANES), 0) // N_B
           == lax.broadcasted_iota(jnp.int32, (LANES, LANES), 1) // N_B).astype(F32)
    n_tiles = D_B // LANES
    sq = kkr * kkr
    sq = jnp.concatenate([sq[:, i * LANES:(i + 1) * LANES] for i in range(n_tiles)], axis=0)
    ssq = _dot32(sq, seg)
    ssq = jnp.concatenate([ssq[i * c:(i + 1) * c] for i in range(n_tiles)], axis=1)
    kkn = kkr / jnp.maximum(jnp.sqrt(ssq), 1e-12)
    beta = kkn * a
    lhs = jnp.concatenate([-kkn * gam_prev, r * gam], axis=0).astype(BF16)
    rhs = jnp.concatenate([beta * inv_gam, kmod * inv_gam], axis=0).astype(BF16)
    end = jnp.concatenate([beta * to_end, kmod * to_end], axis=0).astype(BF16)
    vb = v.astype(BF16)

    ri = lax.broadcasted_iota(jnp.int32, (2 * c, 2 * c), 0)
    cj = lax.broadcasted_iota(jnp.int32, (2 * c, 2 * c), 1)
    keep = (ri % c + ri // c) > (cj % c)
    n_steps = max(1, int(math.log2(c)))
    heads = [slice(h * N_B, (h + 1) * N_B) for h in range(H_B)]
    hr = range(H_B)
    s0 = [s_ref[h] for h in hr]
    res = [_dot_nt(lhs[:, sl], jnp.concatenate([rhs[:, sl], s0[h].astype(BF16)], axis=0))
           for h, sl in enumerate(heads)]
    att = [jnp.where(keep, res[h][:, :2 * c], 0.0) for h in hr]
    from_s0 = [res[h][:, 2 * c:] for h in hr]
    u = [from_s0[h][:c] + _dot(att[h][:c, c:].astype(BF16), vb[:, sl]) for h, sl in enumerate(heads)]
    pb = [att[h][:c, :c].astype(BF16) for h in hr]
    for it in range(n_steps):
        u = [u[h] + _dot(pb[h], u[h].astype(BF16)) for h in hr]
        if it + 1 < n_steps:
            pb = [_dot(pb[h], pb[h]).astype(BF16) for h in hr]
    uv = [jnp.concatenate([u[h].astype(BF16), vb[:, sl]], axis=0) for h, sl in enumerate(heads)]
    ys = [from_s0[h][c:] + _dot(att[h][c:].astype(BF16), uv[h]) for h in hr]
    s_new = [_dot_tn(uv[h], end[:, sl]) for h, sl in enumerate(heads)]
    for h, sl in enumerate(heads):
        s_ref[h] = s0[h] * gam_last[:, sl] + s_new[h]
    outs = []
    for h, sl in enumerate(heads):
        y = ys[h]
        v_h = v[:, sl]
        mean = jnp.mean(y, axis=-1, keepdims=True)
        yc = y - mean
        var = jnp.mean(yc * yc, axis=-1, keepdims=True)
        yn = yc * lax.rsqrt(var + GN_EPS) * lnw_ref[:, sl] + lnb_ref[:, sl]
        outs.append(yn + jnp.sum(bonus_in[:, sl], axis=-1, keepdims=True) * v_h)
    o_ref[0] = (jnp.concatenate(outs, axis=-1) * gate).astype(o_ref.dtype)

    @pl.when(ci == pl.num_programs(1) - 1)
    def _():
        sout_ref[0] = s_ref[...]
        shout_ref[0] = zb[c - 1:c, :]


def _rwkv(z_b, s_all, prev, shift0, p, j, c):
    bsz, t, dz = z_b.shape
    vec = lambda x: x.reshape(1, -1)
    fixed2 = lambda b, ci: (0, 0)
    vspec = lambda n: pl.BlockSpec((1, n), fixed2)
    state = pl.BlockSpec((None, 1, H_B, N_B, N_B), lambda b, ci: (j, b, 0, 0, 0))
    return _stacked_state_call(
        functools.partial(_rwkv_kernel, c=c), prev, 1,
        [pl.BlockSpec((1, c, dz), lambda b, ci: (b, ci, 0)),
         pl.BlockSpec((1, 1, dz), lambda b, ci: (b, 0, 0)),
         state,
         vspec(dz), vspec(D_B),
         pl.BlockSpec((R_W, D_B), fixed2), vspec(D_B),
         pl.BlockSpec((R_A, D_B), fixed2), pl.BlockSpec((R_G, D_B), fixed2),
         vspec(D_B), vspec(D_B), vspec(D_B), vspec(D_B), vspec(D_B)],
        (z_b, shift0.reshape(bsz, 1, dz), s_all, vec(p['rwkv_mu'][j]), vec(p['rwkv_w0'][j]),
         p['rwkv_w2'][j].astype(BF16), vec(p['rwkv_a0'][j]), p['rwkv_a2'][j].astype(BF16),
         p['rwkv_g2'][j].astype(BF16), vec(p['rwkv_k_k'][j]), vec(p['rwkv_k_a'][j]),
         vec(p['rwkv_r_k'][j]), vec(p['rwkv_ln_w'][j]), vec(p['rwkv_ln_b'][j])),
        grid=(bsz, t // c),
        out_specs=[pl.BlockSpec((1, c, D_B), lambda b, ci: (b, ci, 0)),
                   state,
                   pl.BlockSpec((1, 1, dz), lambda b, ci: (b, 0, 0))],
        out_shape=[jax.ShapeDtypeStruct((bsz, t, D_B), BF16),
                   jax.ShapeDtypeStruct(s_all.shape, F32),
                   jax.ShapeDtypeStruct((bsz, 1, dz), F32)],
        scratch_shapes=[pltpu.VMEM((H_B, N_B, N_B), F32), pltpu.VMEM((1, dz), F32)],
        compiler_params=_params("parallel", "arbitrary"),
        name="rwkv7",
    )


def _ret_kernel(q_ref, k_ref, v_ref, g_ref, ang_ref, s0_ref, o_ref, sout_ref, s_ref, *, c, pos0):
    h = pl.program_id(1)
    ci = pl.program_id(2)

    @pl.when(ci == 0)
    def _():
        s_ref[...] = s0_ref[0, 0]

    rowi = lax.broadcasted_iota(jnp.int32, (c, DK_C), 0)
    lane = lax.broadcasted_iota(jnp.int32, (c, DK_C), 1)
    pos = (pos0 + ci * c + rowi).astype(F32)
    ang = pos * ang_ref[...]
    sin = jnp.sin(ang)
    cos = jnp.cos(ang)
    even = (lane % 2) == 0

    def rope(x):
        rot = jnp.where(even, -pltpu.roll(x, DK_C - 1, axis=1), pltpu.roll(x, 1, axis=1))
        return x * cos + rot * sin

    qh = rope(q_ref[0])
    kh = rope(k_ref[0]) * (DK_C ** -0.5)
    v = v_ref[0].astype(BF16)

    hv = jnp.zeros((1, 1), F32) + h.astype(F32)
    lg = jnp.log1p(-jnp.exp2(-5.0 - hv))
    tcol = lax.broadcasted_iota(jnp.int32, (c, 1), 0).astype(F32)
    bcol = (tcol + 1.0) * lg
    blast = float(c) * lg
    ti = lax.broadcasted_iota(jnp.int32, (c, c), 0)
    si = lax.broadcasted_iota(jnp.int32, (c, c), 1)
    causal = ti >= si
    dec = jnp.where(causal, jnp.exp(jnp.where(causal, (ti - si).astype(F32) * lg, 0.0)), 0.0)
    att = _dot_nt(qh.astype(BF16), kh.astype(BF16)) * dec
    s = s_ref[...]
    o = _dot(att.astype(BF16), v) + _dot((qh * jnp.exp(bcol)).astype(BF16), s.astype(BF16))
    s_new = jnp.exp(blast) * s + _dot_tn((kh * jnp.exp(blast - bcol)).astype(BF16), v)
    s_ref[...] = s_new
    o = o * lax.rsqrt(jnp.mean(o * o, axis=-1, keepdims=True) + RMS_EPS)
    o_ref[0] = (o * _silu(g_ref[0])).astype(o_ref.dtype)

    @pl.when(ci == pl.num_programs(2) - 1)
    def _():
        sout_ref[0, 0] = s_new


def _retention(z, s_all, prev, j, pos0, c):
    bsz, t, _ = z.shape
    angle = 1.0 / (10000.0 ** jnp.linspace(0.0, 1.0, DK_C // 2, dtype=F32))
    angle = jnp.repeat(angle, 2).reshape(1, DK_C)
    col = lambda k: (lambda b, h, ci: (b, ci, k * H_C + h))
    state = pl.BlockSpec((None, 1, 1, DK_C, DV_C), lambda b, h, ci: (j, b, h, 0, 0))
    return _stacked_state_call(
        functools.partial(_ret_kernel, c=c, pos0=pos0), prev, 1,
        [pl.BlockSpec((1, c, DK_C), col(0)), pl.BlockSpec((1, c, DK_C), col(1)),
         pl.BlockSpec((1, c, DV_C), col(2)), pl.BlockSpec((1, c, DV_C), col(3)),
         pl.BlockSpec((1, DK_C), lambda b, h, ci: (0, 0)), state],
        (z, z, z, z, angle, s_all),
        grid=(bsz, H_C, t // c),
        out_specs=[pl.BlockSpec((1, c, DV_C), lambda b, h, ci: (b, ci, h)), state],
        out_shape=[jax.ShapeDtypeStruct((bsz, t, D_C), BF16), jax.ShapeDtypeStruct(s_all.shape, F32)],
        scratch_shapes=[pltpu.VMEM((DK_C, DV_C), F32)],
        compiler_params=_params("parallel", "parallel", "arbitrary"),
        name="retention",
    )


def _rglru_kernel(y_ref, x_ref, conv0_ref, h0_ref, cw_ref, cb_ref, wa_ref, wx_ref, ba_ref, bx_ref, lam_ref,
                  o_ref, convout_ref, hout_ref, tail_ref, h_ref, *, tc, pos0):
    ci = pl.program_id(1)

    @pl.when(ci == 0)
    def _():
        tail_ref[...] = conv0_ref[0]
        h_ref[...] = h0_ref[0]

    x = x_ref[0]
    tail = tail_ref[...]
    row = lax.broadcasted_iota(jnp.int32, x.shape, 0)

    def shifted(d):
        r = pltpu.roll(x, d, axis=0)
        for m in range(d):
            r = jnp.where(row == m, tail[CONV_W - 1 - d + m:CONV_W - d + m, :], r)
        return r

    cw = cw_ref[...]
    xc = cb_ref[...] + cw[CONV_W - 1:CONV_W, :] * x
    for d in range(1, CONV_W):
        xc = xc + cw[CONV_W - 1 - d:CONV_W - d, :] * shifted(d)
    new_tail = jnp.concatenate([tail, x], axis=0)[tc:tc + CONV_W - 1, :]
    tail_ref[...] = new_tail

    ra, rx = [], []
    for hd in range(H_D):
        xh = xc[:, hd * BW_D:(hd + 1) * BW_D].astype(BF16)
        ra.append(_dot(xh, wa_ref[hd]))
        rx.append(_dot(xh, wx_ref[hd]))
    rg = _sigmoid(jnp.concatenate(ra, axis=-1) + ba_ref[...])
    ig = _sigmoid(jnp.concatenate(rx, axis=-1) + bx_ref[...])
    nlam = -lam_ref[...]
    softplus = jnp.maximum(nlam, 0.0) + jnp.log1p(jnp.exp(-jnp.abs(nlam)))
    log_a = -LRU_C * rg * softplus
    a = jnp.exp(log_a)
    mult = jnp.sqrt(-jnp.tanh(log_a) * (a * a + 1.0))
    pos = pos0 + ci * tc + row
    mult = jnp.where(pos == 0, 1.0, mult)
    bt = mult * ig * xc

    d = 1
    while d < tc:
        m = row >= d
        bt = jnp.where(m, a * pltpu.roll(bt, d, axis=0) + bt, bt)
        a = jnp.where(m, a * pltpu.roll(a, d, axis=0), a)
        d *= 2
    hseq = bt + a * h_ref[...]
    h_last = hseq[tc - 1:tc, :]
    h_ref[...] = h_last
    y = y_ref[0]
    gelu = 0.5 * y * (1.0 + jnp.tanh(math.sqrt(2.0 / math.pi) * (y + 0.044715 * (y * y * y))))
    o_ref[0] = (hseq * gelu).astype(o_ref.dtype)

    @pl.when(ci == pl.num_programs(1) - 1)
    def _():
        convout_ref[0] = new_tail
        hout_ref[0] = h_last


def _rglru(z, conv0, h0, p, j, pos0, tc):
    bsz, t, _ = z.shape
    vec = lambda x: x.reshape(1, -1)
    fixed2 = lambda b, ci: (0, 0)
    fixed3 = lambda b, ci: (0, 0, 0)
    vspec = pl.BlockSpec((1, D_D), fixed2)
    ycol = 4 * D_C // D_D
    return pl.pallas_call(
        functools.partial(_rglru_kernel, tc=tc, pos0=pos0),
        grid=(bsz, t // tc),
        in_specs=[pl.BlockSpec((1, tc, D_D), lambda b, ci: (b, ci, ycol)),
                  pl.BlockSpec((1, tc, D_D), lambda b, ci: (b, ci, ycol + 1)),
                  pl.BlockSpec((1, CONV_W - 1, D_D), lambda b, ci: (b, 0, 0)),
                  pl.BlockSpec((1, 1, D_D), lambda b, ci: (b, 0, 0)),
                  pl.BlockSpec((CONV_W, D_D), fixed2), vspec,
                  pl.BlockSpec((H_D, BW_D, BW_D), fixed3), pl.BlockSpec((H_D, BW_D, BW_D), fixed3),
                  vspec, vspec, vspec],
        out_specs=[pl.BlockSpec((1, tc, D_D), lambda b, ci: (b, ci, 0)),
                   pl.BlockSpec((1, CONV_W - 1, D_D), lambda b, ci: (b, 0, 0)),
                   pl.BlockSpec((1, 1, D_D), lambda b, ci: (b, 0, 0))],
        out_shape=[jax.ShapeDtypeStruct((bsz, t, D_D), BF16),
                   jax.ShapeDtypeStruct((bsz, CONV_W - 1, D_D), F32),
                   jax.ShapeDtypeStruct((bsz, 1, D_D), F32)],
        scratch_shapes=[pltpu.VMEM((CONV_W - 1, D_D), F32), pltpu.VMEM((1, D_D), F32)],
        compiler_params=_params("parallel", "arbitrary"),
        name="rglru",
    )(z, z, conv0, h0.reshape(bsz, 1, D_D), p['conv_w'][j], vec(p['conv_b'][j]),
      p['rglru_wa'][j].astype(BF16), p['rglru_wx'][j].astype(BF16),
      vec(p['rglru_ba'][j]), vec(p['rglru_bx'][j]), vec(p['rglru_lambda'][j]))


def _tiles(bsz, t):
    if t >= 256:
        return dict(hgrn=(1, 256, 16), rwkv=64, ret=128, lru=256)
    return dict(hgrn=(math.gcd(bsz, 4), t, min(t, 16)), rwkv=t, ret=t, lru=t)


def _run_trunk(x, pos0, st_hgrn, st_rwkv, st_shift, st_ret, st_conv, st_h, p, wb):
    bsz, t, d = x.shape
    m = bsz * t
    tl = _tiles(bsz, t)
    x2 = x.reshape(m, d)
    hn = _rmsnorm_bf16(x2, p['norm_mix_pre'][0], tm=min(512, m))
    n_hgrn, n_rwkv, n_ret = None, None, None
    n_shift, n_conv, n_h = [], [], []
    for l in range(DEPTH):
        j = l // 2
        if l % 2 == 0:
            w_in = wb['w_in_even'][j]
            z_a = _matmul(hn, w_in[:, :4 * D_A], tn=1024).reshape(bsz, t, 4 * D_A)
            z_b = _matmul(hn, w_in[:, 4 * D_A:], tn=D_RWKV_PROJ // 2).reshape(bsz, t, D_RWKV_PROJ)
            bb, tc, sc = tl['hgrn']
            o_a, n_hgrn = _hgrn(z_a, p['hgrn_lb_raw'], p['hgrn_norm_w'][j], st_hgrn, n_hgrn, j, bb, tc, sc)
            o_b, n_rwkv, sh = _rwkv(z_b, st_rwkv, n_rwkv, st_shift[j], p, j, tl['rwkv'])
            n_shift.append(sh.reshape(bsz, D_RWKV_PROJ))
            w_out = wb['w_out_even'][j]
        else:
            z = _matmul(hn, wb['w_in_odd'][j], tn=1024).reshape(bsz, t, 4 * D_C + 2 * D_D)
            o_a, n_ret = _retention(z, st_ret, n_ret, j, pos0, tl['ret'])
            o_b, scv, shh = _rglru(z, st_conv[j], st_h[j], p, j, pos0, tl['lru'])
            n_conv.append(scv)
            n_h.append(shh.reshape(bsz, D_D))
            w_out = wb['w_out_odd'][j]
        x2, hn = _outproj(o_a.reshape(m, -1), o_b.reshape(m, -1), w_out, x2,
                          p['norm_mix_post'][l], p['norm_ffn_pre'][l], tm=min(512, m))
        next_pre = p['norm_mix_pre'][(l + 1) % DEPTH]
        x2, hn = _ffn(hn, wb['w_ffn_in'][l], wb['w_ffn_out'][l], x2, p['norm_ffn_post'][l], next_pre,
                      tm=min(512, m))
    return x2.reshape(bsz, t, d), (n_hgrn, n_rwkv, jnp.stack(n_shift),
                                   n_ret, jnp.stack(n_conv), jnp.stack(n_h))


def _zero_state(s, batch):
    return jnp.zeros((s.shape[0], batch) + s.shape[2:], s.dtype)


def kernel(x_prompt, x_sample, state_hgrn, state_rwkv, state_rwkv_shift, state_ret, state_rglru_conv, state_rglru_h, norm_mix_pre, norm_mix_post, norm_ffn_pre, norm_ffn_post, w_in_even, w_out_even, hgrn_lb_raw, hgrn_norm_w, rwkv_mu, rwkv_w0, rwkv_w2, rwkv_a0, rwkv_a2, rwkv_g2, rwkv_k_k, rwkv_k_a, rwkv_r_k, rwkv_ln_w, rwkv_ln_b, w_in_odd, w_out_odd, conv_w, conv_b, rglru_wa, rglru_ba, rglru_wx, rglru_bx, rglru_lambda, w_ffn_in, w_ffn_out):
    p = dict(norm_mix_pre=norm_mix_pre, norm_mix_post=norm_mix_post, norm_ffn_pre=norm_ffn_pre,
             norm_ffn_post=norm_ffn_post, hgrn_lb_raw=hgrn_lb_raw, hgrn_norm_w=hgrn_norm_w,
             rwkv_mu=rwkv_mu, rwkv_w0=rwkv_w0, rwkv_w2=rwkv_w2, rwkv_a0=rwkv_a0, rwkv_a2=rwkv_a2,
             rwkv_g2=rwkv_g2, rwkv_k_k=rwkv_k_k, rwkv_k_a=rwkv_k_a, rwkv_r_k=rwkv_r_k,
             rwkv_ln_w=rwkv_ln_w, rwkv_ln_b=rwkv_ln_b, conv_w=conv_w, conv_b=conv_b,
             rglru_wa=rglru_wa, rglru_ba=rglru_ba, rglru_wx=rglru_wx, rglru_bx=rglru_bx,
             rglru_lambda=rglru_lambda)
    wb = dict(w_in_even=w_in_even.astype(BF16), w_out_even=w_out_even.astype(BF16),
              w_in_odd=w_in_odd.astype(BF16), w_out_odd=w_out_odd.astype(BF16),
              w_ffn_in=w_ffn_in.astype(BF16), w_ffn_out=w_ffn_out.astype(BF16))
    bp = x_prompt.shape[0]
    y_prompt, st_p = _run_trunk(x_prompt, 0,
                                _zero_state(state_hgrn, bp), _zero_state(state_rwkv, bp),
                                _zero_state(state_rwkv_shift, bp), _zero_state(state_ret, bp),
                                _zero_state(state_rglru_conv, bp), _zero_state(state_rglru_h, bp), p, wb)
    y_sample, st_s = _run_trunk(x_sample, PAST_LEN, state_hgrn, state_rwkv, state_rwkv_shift,
                                state_ret, state_rglru_conv, state_rglru_h, p, wb)
    hgrn_p, rwkv_p, shift_p, ret_p, conv_p, h_p = st_p
    hgrn_s, rwkv_s, shift_s, ret_s, conv_s, h_s = st_s
    return (y_prompt, y_sample, hgrn_p, hgrn_s, rwkv_p, rwkv_s, shift_p, shift_s,
            ret_p, ret_s, conv_p, conv_s, h_p, h_s)
```

```python
import functools
import math

import jax
import jax.numpy as jnp
from jax import lax
from jax.experimental import pallas as pl
from jax.experimental.pallas import tpu as pltpu

F32 = jnp.float32
BF16 = jnp.bfloat16
HIGHEST = lax.Precision.HIGHEST

D_MODEL = 2048
DEPTH = 4
N_EVEN = (DEPTH + 1) // 2
N_ODD = DEPTH // 2
D_A = D_MODEL // 2
H_A = 8
DK_A = 128
DV_A = 128
D_B = D_MODEL // 2
N_B = 64
H_B = D_B // N_B
R_W = 64
R_A = 64
R_G = 128
D_RWKV_PROJ = 3 * D_B + R_W + R_A + R_G
D_C = D_MODEL // 2
H_C = 4
DK_C = D_C // H_C
DV_C = D_C // H_C
D_D = D_MODEL // 2
H_D = 4
BW_D = D_D // H_D
CONV_W = 4
LRU_C = 8.0
D_FF = ((8 * D_MODEL + 3 * 256 - 1) // (3 * 256)) * 256
RMS_EPS = 1e-6
GN_EPS = 64e-5
TINY = 1e-30
PAST_LEN = 16384

VMEM_LIMIT_BYTES = 56 * 1024 * 1024
LANES = 128


def _params(*sem):
    return pltpu.CompilerParams(dimension_semantics=sem, vmem_limit_bytes=VMEM_LIMIT_BYTES)


def _rms(x, w):
    return x * lax.rsqrt(jnp.mean(x * x, axis=-1, keepdims=True) + RMS_EPS) * w


def _sigmoid(x):
    return jax.nn.sigmoid(x)


def _silu(x):
    return x * jax.nn.sigmoid(x)


def _dot(a, b):
    return jnp.dot(a, b, preferred_element_type=F32)


def _dot_nt(a, b):
    return lax.dot_general(a, b, (((1,), (1,)), ((), ())), preferred_element_type=F32)


def _dot_tn(a, b):
    return lax.dot_general(a, b, (((0,), (0,)), ((), ())), preferred_element_type=F32)


def _dot32(a, b):
    return jnp.dot(a, b, preferred_element_type=F32, precision=HIGHEST)


def _dot32_nt(a, b):
    return lax.dot_general(a, b, (((1,), (1,)), ((), ())), preferred_element_type=F32, precision=HIGHEST)


def _dot32_tn(a, b):
    return lax.dot_general(a, b, (((0,), (0,)), ((), ())), preferred_element_type=F32, precision=HIGHEST)


def _lower_tri(n, strict):
    r = lax.broadcasted_iota(jnp.int32, (n, n), 0)
    c = lax.broadcasted_iota(jnp.int32, (n, n), 1)
    return (r > c) if strict else (r >= c)


def _rmsnorm_kernel(x_ref, w_ref, o_ref):
    o_ref[...] = _rms(x_ref[...], w_ref[...]).astype(o_ref.dtype)


def _rmsnorm_bf16(x, w, tm=512):
    m, d = x.shape
    return pl.pallas_call(
        _rmsnorm_kernel,
        grid=(m // tm,),
        in_specs=[pl.BlockSpec((tm, d), lambda i: (i, 0)), pl.BlockSpec((1, d), lambda i: (0, 0))],
        out_specs=pl.BlockSpec((tm, d), lambda i: (i, 0)),
        out_shape=jax.ShapeDtypeStruct((m, d), BF16),
        compiler_params=_params("parallel"),
        name="rmsnorm",
    )(x, w.reshape(1, d))


def _in_proj_kernel(a_ref, w_ref, o_ref, wb_ref):
    @pl.when(pl.program_id(1) == 0)
    def _():
        wb_ref[...] = w_ref[...].astype(BF16)

    o_ref[...] = _dot(a_ref[...], wb_ref[...])


def _in_proj(a, w_all, layer, n, tn, tm):
    m, k = a.shape
    tm = min(tm, m)
    return pl.pallas_call(
        _in_proj_kernel,
        grid=(n // tn, m // tm),
        in_specs=[pl.BlockSpec((tm, k), lambda j, i: (i, 0)),
                  pl.BlockSpec((None, k, tn), lambda j, i: (layer, 0, j))],
        out_specs=pl.BlockSpec((tm, tn), lambda j, i: (i, j)),
        out_shape=jax.ShapeDtypeStruct((m, n), F32),
        scratch_shapes=[pltpu.VMEM((k, tn), BF16)],
        compiler_params=_params("parallel", "arbitrary"),
        name="in_proj",
    )(a, w_all)


def _outproj_kernel(oa_ref, ob_ref, wa_ref, wb_ref, x_ref, post_ref, pre_ref, xo_ref, hn_ref):
    mix = _dot(oa_ref[...], wa_ref[...]) + _dot(ob_ref[...], wb_ref[...])
    xn = x_ref[...] + _rms(mix, post_ref[...])
    xo_ref[...] = xn
    hn_ref[...] = _rms(xn, pre_ref[...]).astype(hn_ref.dtype)


def _outproj(oa, ob, w_out, x, post_w, pre_w, tm=512):
    m, d = x.shape
    ka, kb = oa.shape[1], ob.shape[1]
    row = lambda i: (i, 0)
    fixed = lambda i: (0, 0)
    return pl.pallas_call(
        _outproj_kernel,
        grid=(m // tm,),
        in_specs=[pl.BlockSpec((tm, ka), row), pl.BlockSpec((tm, kb), row),
                  pl.BlockSpec((ka, d), fixed), pl.BlockSpec((kb, d), lambda i: (1, 0)),
                  pl.BlockSpec((tm, d), row), pl.BlockSpec((1, d), fixed), pl.BlockSpec((1, d), fixed)],
        out_specs=[pl.BlockSpec((tm, d), row), pl.BlockSpec((tm, d), row)],
        out_shape=[jax.ShapeDtypeStruct((m, d), F32), jax.ShapeDtypeStruct((m, d), BF16)],
        compiler_params=_params("parallel"),
        name="out_proj",
    )(oa, ob, w_out, w_out, x, post_w.reshape(1, d), pre_w.reshape(1, d))


def _ffn_up_kernel(h_ref, wg_ref, wu_ref, act_ref, wgb_ref, wub_ref):
    @pl.when(pl.program_id(1) == 0)
    def _():
        wgb_ref[...] = wg_ref[...].astype(BF16)
        wub_ref[...] = wu_ref[...].astype(BF16)

    h = h_ref[...]
    act_ref[...] = (_silu(_dot(h, wgb_ref[...])) * _dot(h, wub_ref[...])).astype(act_ref.dtype)


def _ffn_up(hn, w_in_all, layer, tm=1024, tf=512):
    m, d = hn.shape
    tm = min(tm, m)
    nf = D_FF // tf
    return pl.pallas_call(
        _ffn_up_kernel,
        grid=(nf, m // tm),
        in_specs=[pl.BlockSpec((tm, d), lambda j, i: (i, 0)),
                  pl.BlockSpec((None, d, tf), lambda j, i: (layer, 0, j)),
                  pl.BlockSpec((None, d, tf), lambda j, i: (layer, 0, j + nf))],
        out_specs=pl.BlockSpec((tm, tf), lambda j, i: (i, j)),
        out_shape=jax.ShapeDtypeStruct((m, D_FF), BF16),
        scratch_shapes=[pltpu.VMEM((d, tf), BF16), pltpu.VMEM((d, tf), BF16)],
        compiler_params=_params("parallel", "arbitrary"),
        name="ffn_up",
    )(hn, w_in_all, w_in_all)


def _ffn_down_kernel(act_ref, wo_ref, x_ref, post_ref, pre_ref, xo_ref, hn_ref):
    xn = x_ref[...] + _rms(_dot(act_ref[...], wo_ref[...]), post_ref[...])
    xo_ref[...] = xn
    hn_ref[...] = _rms(xn, pre_ref[...]).astype(hn_ref.dtype)


def _ffn_down(act, w_out, x, post_w, pre_w, tm=256):
    m, d = x.shape
    tm = min(tm, m)
    row = lambda i: (i, 0)
    fixed = lambda i: (0, 0)
    return pl.pallas_call(
        _ffn_down_kernel,
        grid=(m // tm,),
        in_specs=[pl.BlockSpec((tm, D_FF), row),
                  pl.BlockSpec((D_FF, d), fixed, pipeline_mode=pl.Buffered(1)),
                  pl.BlockSpec((tm, d), row), pl.BlockSpec((1, d), fixed), pl.BlockSpec((1, d), fixed)],
        out_specs=[pl.BlockSpec((tm, d), row), pl.BlockSpec((tm, d), row)],
        out_shape=[jax.ShapeDtypeStruct((m, d), F32), jax.ShapeDtypeStruct((m, d), BF16)],
        compiler_params=_params("parallel"),
        name="ffn_down",
    )(act, w_out, x, post_w.reshape(1, d), pre_w.reshape(1, d))


def _hgrn_kernel(q_ref, f_ref, i_ref, g_ref, lbraw_ref, nw_ref, s0_ref, o_ref, sout_ref, st_ref,
                 *, layer, sc, n_sub, bb):
    ci = pl.program_id(1)

    @pl.when(ci == 0)
    def _():
        for b in range(bb):
            for h in range(H_A):
                st_ref[b, h] = s0_ref[b, h].T

    raw = lbraw_ref[...]
    e = jnp.exp(raw - jnp.max(raw, axis=0, keepdims=True))
    pr = e / jnp.sum(e, axis=0, keepdims=True)
    lb_all = jnp.zeros((1, D_A), F32)
    for r in range(1, layer + 1):
        lb_all = lb_all + pr[r:r + 1, :]
    nw_all = nw_ref[...]
    tri = _lower_tri(sc, strict=False).astype(F32)
    row = lax.broadcasted_iota(jnp.int32, (sc, LANES), 0)

    oml = 1.0 - lb_all
    heads = [slice(h * DK_A, (h + 1) * DK_A) for h in range(H_A)]

    def body(idx, carry):
        b = idx // n_sub
        rows = pl.ds(pl.multiple_of((idx % n_sub) * sc, sc), sc)
        zf = f_ref[b, rows, :]
        iv = i_ref[b, rows, :]
        fg = lb_all + oml * _sigmoid(zf)
        logf = jnp.log(jnp.maximum(fg, TINY))
        kk = oml * _sigmoid(-zf)
        qs = _silu(q_ref[b, rows, :])
        bc = _dot32(tri, logf)
        blast = bc[sc - 1:sc, :]
        qe = (qs * jnp.exp(bc)).astype(BF16)
        kh = (kk * jnp.exp(blast - bc)).astype(BF16)
        ivb = iv.astype(BF16)
        keep = jnp.exp(blast)
        sts = [st_ref[b, h] for h in range(H_A)]
        o_inter = [_dot_nt(qe[:, sl], sts[h].astype(BF16)) for h, sl in enumerate(heads)]
        upd = [_dot_tn(ivb[:, sl], kh[:, sl]) for sl in heads]
        for h, sl in enumerate(heads):
            st_ref[b, h] = sts[h] * keep[:, sl] + upd[h]
        outs = []
        for h, sl in enumerate(heads):
            bc_h, kk_h, qs_h, iv_h = bc[:, sl], kk[:, sl], qs[:, sl], iv[:, sl]
            o_rows = []
            for t in range(sc):
                m = row <= t
                dec = jnp.where(m, jnp.exp(jnp.where(m, bc_h[t:t + 1, :] - bc_h, 0.0)), 0.0)
                a_col = jnp.sum(kk_h * dec * qs_h[t:t + 1, :], axis=-1, keepdims=True)
                o_rows.append(jnp.sum(a_col * iv_h, axis=0, keepdims=True))
            o = jnp.concatenate(o_rows, axis=0) + o_inter[h]
            outs.append(o * lax.rsqrt(jnp.mean(o * o, axis=-1, keepdims=True) + RMS_EPS))
        o = jnp.concatenate(outs, axis=-1) * nw_all
        o_ref[b, rows, :] = (o * _silu(g_ref[b, rows, :])).astype(o_ref.dtype)
        return carry

    lax.fori_loop(0, bb * n_sub, body, 0)

    @pl.when(ci == pl.num_programs(1) - 1)
    def _():
        for b in range(bb):
            for h in range(H_A):
                sout_ref[b, h] = st_ref[b, h].T


def _stacked_state_call(kernel_fn, prev, state_out_index, in_specs, args, **kw):
    if prev is None:
        return pl.pallas_call(kernel_fn, in_specs=in_specs, **kw)(*args)

    def with_prev(prev_ref, *refs):
        del prev_ref
        kernel_fn(*refs)

    return pl.pallas_call(with_prev, in_specs=[pl.BlockSpec(memory_space=pl.ANY)] + in_specs,
                          input_output_aliases={0: state_out_index}, **kw)(prev, *args)


def _hgrn(z_a, lb_raw, norm_w, s_all, prev, layer, bb, tc, sc):
    bsz, t, _ = z_a.shape
    col = lambda k: (lambda b, c: (b, c, k))
    fixed2 = lambda b, c: (0, 0)
    state = pl.BlockSpec((None, bb, H_A, DK_A, DV_A), lambda b, c: (layer, b, 0, 0, 0))
    return _stacked_state_call(
        functools.partial(_hgrn_kernel, layer=layer, sc=sc, n_sub=tc // sc, bb=bb), prev, 1,
        [pl.BlockSpec((bb, tc, D_A), col(0)), pl.BlockSpec((bb, tc, D_A), col(1)),
         pl.BlockSpec((bb, tc, D_A), col(2)), pl.BlockSpec((bb, tc, D_A), col(3)),
         pl.BlockSpec((N_EVEN, D_A), fixed2), pl.BlockSpec((1, D_A), fixed2), state],
        (z_a, z_a, z_a, z_a, lb_raw, norm_w.reshape(1, D_A), s_all),
        grid=(bsz // bb, t // tc),
        out_specs=[pl.BlockSpec((bb, tc, D_A), lambda b, c: (b, c, 0)), state],
        out_shape=[jax.ShapeDtypeStruct((bsz, t, D_A), BF16),
                   jax.ShapeDtypeStruct(s_all.shape, F32)],
        scratch_shapes=[pltpu.VMEM((bb, H_A, DV_A, DK_A), F32)],
        compiler_params=_params("parallel", "arbitrary"),
        name="hgrn2",
    )


def _head_sums(x):
    rows = x.shape[0]
    seg = (lax.broadcasted_iota(jnp.int32, (LANES, LANES), 0) // N_B
           == lax.broadcasted_iota(jnp.int32, (LANES, LANES), 1) // N_B).astype(F32)
    n_tiles = D_B // LANES
    xs = jnp.concatenate([x[:, i * LANES:(i + 1) * LANES] for i in range(n_tiles)], axis=0)
    s = _dot32(xs, seg)
    return jnp.concatenate([s[i * rows:(i + 1) * rows] for i in range(n_tiles)], axis=1)


def _rwkv_kernel(z_ref, sh0_ref, s0_ref, mu_ref, w0_ref, w2_ref, a0_ref, a2_ref, g2_ref, kk_ref, ka_ref,
                 rk_ref, lnw_ref, lnb_ref, o_ref, sout_ref, shout_ref, s_ref, prev_ref, *, c, nb):
    ci = pl.program_id(1)
    rows = nb * c
    dz = D_RWKV_PROJ

    @pl.when(ci == 0)
    def _():
        s_ref[...] = s0_ref[...]
        prev_ref[...] = sh0_ref[...]

    blocks = [slice(b * c, (b + 1) * c) for b in range(nb)]
    last = [slice((b + 1) * c - 1, (b + 1) * c) for b in range(nb)]

    def per_block(rows_of):
        n = rows_of[0].shape[-1]
        blk = lax.broadcasted_iota(jnp.int32, (rows, n), 0) // c
        out = jnp.broadcast_to(rows_of[0], (rows, n))
        for b in range(1, nb):
            out = jnp.where(blk == b, rows_of[b], out)
        return out

    zb = z_ref[...].reshape(rows, dz)
    row = lax.broadcasted_iota(jnp.int32, (rows, dz), 0)
    prev = jnp.where(row % c == 0, per_block([prev_ref[b] for b in range(nb)]), pltpu.roll(zb, 1, axis=0))
    for b in range(nb):
        prev_ref[b] = zb[last[b], :]
    zs = zb + (prev - zb) * mu_ref[...]
    r = zs[:, 0:D_B]
    kb = zs[:, D_B:2 * D_B]
    v = zs[:, 2 * D_B:3 * D_B]
    o0 = 3 * D_B
    wl = zs[:, o0:o0 + R_W]
    al = zs[:, o0 + R_W:o0 + R_W + R_A]
    gl = zs[:, o0 + R_W + R_A:o0 + R_W + R_A + R_G]
    wlin = w0_ref[...] + _dot(jnp.tanh(wl).astype(BF16), w2_ref[...])
    w_log = -math.exp(-0.5) * _sigmoid(wlin)
    a = _sigmoid(a0_ref[...] + _dot(al.astype(BF16), a2_ref[...]))
    gate = _dot(_sigmoid(gl).astype(BF16), g2_ref[...])
    kkr = kb * kk_ref[...]
    kmod = kb * (1.0 + (a - 1.0) * ka_ref[...])

    ri = lax.broadcasted_iota(jnp.int32, (rows, rows), 0)
    cj = lax.broadcasted_iota(jnp.int32, (rows, rows), 1)
    tri = jnp.where(ri // c == cj // c, (ri >= cj).astype(F32), 0.0)
    bc = _dot32(tri, w_log)
    b_last = [bc[l, :] for l in last]
    blast = per_block(b_last)
    gam = jnp.exp(bc)
    inv_gam = jnp.exp(-bc)
    gam_prev = jnp.exp(bc - w_log)
    to_end = jnp.exp(blast - bc)
    gam_last = [jnp.exp(x) for x in b_last]
    kkn = kkr / jnp.maximum(jnp.sqrt(_head_sums(kkr * kkr)), 1e-12)
    beta = kkn * a
    xa = -kkn * gam_prev
    xr = r * gam
    yb = beta * inv_gam
    yk = kmod * inv_gam
    eb = beta * to_end
    ek = kmod * to_end
    lhs = [jnp.concatenate([xa[bl], xr[bl]], axis=0).astype(BF16) for bl in blocks]
    rhs = [jnp.concatenate([yb[bl], yk[bl]], axis=0).astype(BF16) for bl in blocks]
    end = [jnp.concatenate([eb[bl], ek[bl]], axis=0).astype(BF16) for bl in blocks]
    vb = [v[bl].astype(BF16) for bl in blocks]

    ri2 = lax.broadcasted_iota(jnp.int32, (2 * c, 2 * c), 0)
    cj2 = lax.broadcasted_iota(jnp.int32, (2 * c, 2 * c), 1)
    keep = (ri2 % c + ri2 // c) > (cj2 % c)
    n_steps = max(1, int(math.log2(c)))
    heads = [slice(h * N_B, (h + 1) * N_B) for h in range(H_B)]
    pairs = [(b, h) for b in range(nb) for h in range(H_B)]
    idx = range(len(pairs))
    s0 = [s_ref[b, h] for b, h in pairs]
    res = [_dot_nt(lhs[b][:, heads[h]], jnp.concatenate([rhs[b][:, heads[h]], s0[i].astype(BF16)], axis=0))
           for i, (b, h) in enumerate(pairs)]
    att = [jnp.where(keep, res[i][:, :2 * c], 0.0) for i in idx]
    from_s0 = [res[i][:, 2 * c:] for i in idx]
    u = [from_s0[i][:c] + _dot(att[i][:c, c:].astype(BF16), vb[b][:, heads[h]]) for i, (b, h) in enumerate(pairs)]
    pb = [att[i][:c, :c].astype(BF16) for i in idx]
    for it in range(n_steps):
        u = [u[i] + _dot(pb[i], u[i].astype(BF16)) for i in idx]
        if it + 1 < n_steps:
            pb = [_dot(pb[i], pb[i]).astype(BF16) for i in idx]
    uv = [jnp.concatenate([u[i].astype(BF16), vb[b][:, heads[h]]], axis=0) for i, (b, h) in enumerate(pairs)]
    ys = [from_s0[i][c:] + _dot(att[i][c:].astype(BF16), uv[i]) for i in idx]
    s_new = [_dot_tn(uv[i], end[b][:, heads[h]]) for i, (b, h) in enumerate(pairs)]
    for i, (b, h) in enumerate(pairs):
        s_ref[b, h] = s0[i] * gam_last[b][:, heads[h]] + s_new[i]

    y = jnp.concatenate([jnp.concatenate(ys[b * H_B:(b + 1) * H_B], axis=-1) for b in range(nb)], axis=0)
    yc = y - _head_sums(y) * (1.0 / N_B)
    var = _head_sums(yc * yc) * (1.0 / N_B)
    yn = yc * lax.rsqrt(var + GN_EPS) * lnw_ref[...] + lnb_ref[...]
    out = (yn + _head_sums(r * kmod * rk_ref[...]) * v) * gate
    o_ref[...] = out.reshape(nb, c, D_B).astype(o_ref.dtype)

    @pl.when(ci == pl.num_programs(1) - 1)
    def _():
        sout_ref[...] = s_ref[...]
        for b in range(nb):
            shout_ref[b] = zb[last[b], :]


def _rwkv(z_b, s_all, prev, shift0, p, j, c, nb):
    bsz, t, dz = z_b.shape
    vec = lambda x: x.reshape(1, -1)
    fixed2 = lambda b, ci: (0, 0)
    vspec = lambda n: pl.BlockSpec((1, n), fixed2)
    state = pl.BlockSpec((None, nb, H_B, N_B, N_B), lambda b, ci: (j, b, 0, 0, 0))
    shift = pl.BlockSpec((nb, 1, dz), lambda b, ci: (b, 0, 0))
    return _stacked_state_call(
        functools.partial(_rwkv_kernel, c=c, nb=nb), prev, 1,
        [pl.BlockSpec((nb, c, dz), lambda b, ci: (b, ci, 0)), shift, state,
         vspec(dz), vspec(D_B),
         pl.BlockSpec((R_W, D_B), fixed2), vspec(D_B),
         pl.BlockSpec((R_A, D_B), fixed2), pl.BlockSpec((R_G, D_B), fixed2),
         vspec(D_B), vspec(D_B), vspec(D_B), vspec(D_B), vspec(D_B)],
        (z_b, shift0.reshape(bsz, 1, dz), s_all, vec(p['rwkv_mu'][j]), vec(p['rwkv_w0'][j]),
         p['rwkv_w2'][j].astype(BF16), vec(p['rwkv_a0'][j]), p['rwkv_a2'][j].astype(BF16),
         p['rwkv_g2'][j].astype(BF16), vec(p['rwkv_k_k'][j]), vec(p['rwkv_k_a'][j]),
         vec(p['rwkv_r_k'][j]), vec(p['rwkv_ln_w'][j]), vec(p['rwkv_ln_b'][j])),
        grid=(bsz // nb, t // c),
        out_specs=[pl.BlockSpec((nb, c, D_B), lambda b, ci: (b, ci, 0)), state, shift],
        out_shape=[jax.ShapeDtypeStruct((bsz, t, D_B), BF16),
                   jax.ShapeDtypeStruct(s_all.shape, F32),
                   jax.ShapeDtypeStruct((bsz, 1, dz), F32)],
        scratch_shapes=[pltpu.VMEM((nb, H_B, N_B, N_B), F32), pltpu.VMEM((nb, 1, dz), F32)],
        compiler_params=_params("parallel", "arbitrary"),
        name="rwkv7",
    )


def _rope_kernel(ang_ref, sin_ref, cos_ref, *, tt, pos0):
    row = lax.broadcasted_iota(jnp.int32, (tt, DK_C), 0)
    ang = (pos0 + pl.program_id(0) * tt + row).astype(F32) * ang_ref[...]
    sin_ref[...] = jnp.sin(ang)
    cos_ref[...] = jnp.cos(ang)


def _rope_table(pos0, t):
    angle = 1.0 / (10000.0 ** jnp.linspace(0.0, 1.0, DK_C // 2, dtype=F32))
    angle = jnp.repeat(angle, 2).reshape(1, DK_C)
    tt = min(t, 256)
    out = pl.BlockSpec((tt, DK_C), lambda i: (i, 0))
    return pl.pallas_call(
        functools.partial(_rope_kernel, tt=tt, pos0=pos0),
        grid=(t // tt,),
        in_specs=[pl.BlockSpec((1, DK_C), lambda i: (0, 0))],
        out_specs=[out, out],
        out_shape=[jax.ShapeDtypeStruct((t, DK_C), F32)] * 2,
        compiler_params=_params("parallel"),
        name="rope_table",
    )(angle)


def _ret_kernel(q_ref, k_ref, v_ref, g_ref, sin_ref, cos_ref, s0_ref, o_ref, sout_ref, s_ref, *, c, nb):
    ci = pl.program_id(1)

    @pl.when(ci == 0)
    def _():
        s_ref[...] = s0_ref[...]

    sin = jnp.concatenate([sin_ref[...]] * H_C, axis=1)
    cos = jnp.concatenate([cos_ref[...]] * H_C, axis=1)
    even = (lax.broadcasted_iota(jnp.int32, (c, D_C), 1) % 2) == 0

    def rope(x):
        rot = jnp.where(even, -pltpu.roll(x, D_C - 1, axis=1), pltpu.roll(x, 1, axis=1))
        return x * cos + rot * sin

    ti = lax.broadcasted_iota(jnp.int32, (c, c), 0)
    si = lax.broadcasted_iota(jnp.int32, (c, c), 1)
    causal = ti >= si
    dist = jnp.where(causal, (ti - si).astype(F32), 0.0)
    tcol = lax.broadcasted_iota(jnp.int32, (c, 1), 0).astype(F32)
    heads = [slice(h * DK_C, (h + 1) * DK_C) for h in range(H_C)]
    log_gamma = [math.log1p(-2.0 ** (-5.0 - h)) for h in range(H_C)]
    dec = [jnp.where(causal, jnp.exp(dist * lg), 0.0) for lg in log_gamma]
    from_start = [jnp.exp((tcol + 1.0) * lg) for lg in log_gamma]
    to_end = [jnp.exp((float(c) - 1.0 - tcol) * lg) for lg in log_gamma]
    keep = [math.exp(float(c) * lg) for lg in log_gamma]

    pairs = [(b, h) for b in range(nb) for h in range(H_C)]
    qs = [rope(q_ref[b]) for b in range(nb)]
    ks = [rope(k_ref[b]) * (DK_C ** -0.5) for b in range(nb)]
    vb = [v_ref[b].astype(BF16) for b in range(nb)]
    qb = [q.astype(BF16) for q in qs]
    kb = [k.astype(BF16) for k in ks]
    s_old = [s_ref[b, h] for b, h in pairs]
    att = [_dot_nt(qb[b][:, heads[h]], kb[b][:, heads[h]]) for b, h in pairs]
    o_st = [_dot((qs[b][:, heads[h]] * from_start[h]).astype(BF16), s_old[i].astype(BF16))
            for i, (b, h) in enumerate(pairs)]
    o_in = [_dot((att[i] * dec[h]).astype(BF16), vb[b][:, heads[h]]) for i, (b, h) in enumerate(pairs)]
    upd = [_dot_tn((ks[b][:, heads[h]] * to_end[h]).astype(BF16), vb[b][:, heads[h]]) for b, h in pairs]
    s_new = [keep[h] * s_old[i] + upd[i] for i, (b, h) in enumerate(pairs)]
    for i, (b, h) in enumerate(pairs):
        s_ref[b, h] = s_new[i]
    for b in range(nb):
        outs = []
        for h in range(H_C):
            o = o_in[b * H_C + h] + o_st[b * H_C + h]
            outs.append(o * lax.rsqrt(jnp.mean(o * o, axis=-1, keepdims=True) + RMS_EPS))
        o_ref[b] = (jnp.concatenate(outs, axis=-1) * _silu(g_ref[b])).astype(o_ref.dtype)

    @pl.when(ci == pl.num_programs(1) - 1)
    def _():
        for i, (b, h) in enumerate(pairs):
            sout_ref[b, h] = s_new[i]


def _retention(z, s_all, prev, j, sin, cos, c, nb):
    bsz, t, _ = z.shape
    col = lambda k: (lambda b, ci: (b, ci, k))
    state = pl.BlockSpec((None, nb, H_C, DK_C, DV_C), lambda b, ci: (j, b, 0, 0, 0))
    table = pl.BlockSpec((c, DK_C), lambda b, ci: (ci, 0))
    return _stacked_state_call(
        functools.partial(_ret_kernel, c=c, nb=nb), prev, 1,
        [pl.BlockSpec((nb, c, D_C), col(0)), pl.BlockSpec((nb, c, D_C), col(1)),
         pl.BlockSpec((nb, c, D_C), col(2)), pl.BlockSpec((nb, c, D_C), col(3)),
         table, table, state],
        (z, z, z, z, sin, cos, s_all),
        grid=(bsz // nb, t // c),
        out_specs=[pl.BlockSpec((nb, c, D_C), lambda b, ci: (b, ci, 0)), state],
        out_shape=[jax.ShapeDtypeStruct((bsz, t, D_C), BF16), jax.ShapeDtypeStruct(s_all.shape, F32)],
        scratch_shapes=[pltpu.VMEM((nb, H_C, DK_C, DV_C), F32)],
        compiler_params=_params("parallel", "arbitrary"),
        name="retention",
    )


def _rglru_kernel(y_ref, x_ref, conv0_ref, h0_ref, cw_ref, cb_ref, wa_ref, wx_ref, ba_ref, bx_ref, lam_ref,
                  o_ref, convout_ref, hout_ref, tail_ref, h_ref, *, tc, pos0):
    ci = pl.program_id(1)

    @pl.when(ci == 0)
    def _():
        tail_ref[...] = conv0_ref[0]
        h_ref[...] = h0_ref[0]

    x = x_ref[0]
    tail = tail_ref[...]
    row = lax.broadcasted_iota(jnp.int32, x.shape, 0)

    def shifted(d):
        r = pltpu.roll(x, d, axis=0)
        for m in range(d):
            r = jnp.where(row == m, tail[CONV_W - 1 - d + m:CONV_W - d + m, :], r)
        return r

    cw = cw_ref[...]
    xc = cb_ref[...] + cw[CONV_W - 1:CONV_W, :] * x
    for d in range(1, CONV_W):
        xc = xc + cw[CONV_W - 1 - d:CONV_W - d, :] * shifted(d)
    new_tail = jnp.concatenate([tail, x], axis=0)[tc:tc + CONV_W - 1, :]
    tail_ref[...] = new_tail

    ra, rx = [], []
    for hd in range(H_D):
        xh = xc[:, hd * BW_D:(hd + 1) * BW_D].astype(BF16)
        ra.append(_dot(xh, wa_ref[hd]))
        rx.append(_dot(xh, wx_ref[hd]))
    rg = _sigmoid(jnp.concatenate(ra, axis=-1) + ba_ref[...])
    ig = _sigmoid(jnp.concatenate(rx, axis=-1) + bx_ref[...])
    nlam = -lam_ref[...]
    softplus = jnp.maximum(nlam, 0.0) + jnp.log1p(jnp.exp(-jnp.abs(nlam)))
    log_a = -LRU_C * rg * softplus
    a = jnp.exp(log_a)
    mult = jnp.sqrt(-jnp.tanh(log_a) * (a * a + 1.0))
    pos = pos0 + ci * tc + row
    mult = jnp.where(pos == 0, 1.0, mult)
    bt = mult * ig * xc

    d = 1
    while d < tc:
        m = row >= d
        bt = jnp.where(m, a * pltpu.roll(bt, d, axis=0) + bt, bt)
        a = jnp.where(m, a * pltpu.roll(a, d, axis=0), a)
        d *= 2
    hseq = bt + a * h_ref[...]
    h_last = hseq[tc - 1:tc, :]
    h_ref[...] = h_last
    y = y_ref[0]
    gelu = 0.5 * y * (1.0 + jnp.tanh(math.sqrt(2.0 / math.pi) * (y + 0.044715 * (y * y * y))))
    o_ref[0] = (hseq * gelu).astype(o_ref.dtype)

    @pl.when(ci == pl.num_programs(1) - 1)
    def _():
        convout_ref[0] = new_tail
        hout_ref[0] = h_last


def _rglru(z, conv0, h0, p, j, pos0, tc):
    bsz, t, _ = z.shape
    vec = lambda x: x.reshape(1, -1)
    fixed2 = lambda b, ci: (0, 0)
    fixed3 = lambda b, ci: (0, 0, 0)
    vspec = pl.BlockSpec((1, D_D), fixed2)
    ycol = 4 * D_C // D_D
    return pl.pallas_call(
        functools.partial(_rglru_kernel, tc=tc, pos0=pos0),
        grid=(bsz, t // tc),
        in_specs=[pl.BlockSpec((1, tc, D_D), lambda b, ci: (b, ci, ycol)),
                  pl.BlockSpec((1, tc, D_D), lambda b, ci: (b, ci, ycol + 1)),
                  pl.BlockSpec((1, CONV_W - 1, D_D), lambda b, ci: (b, 0, 0)),
                  pl.BlockSpec((1, 1, D_D), lambda b, ci: (b, 0, 0)),
                  pl.BlockSpec((CONV_W, D_D), fixed2), vspec,
                  pl.BlockSpec((H_D, BW_D, BW_D), fixed3), pl.BlockSpec((H_D, BW_D, BW_D), fixed3),
                  vspec, vspec, vspec],
        out_specs=[pl.BlockSpec((1, tc, D_D), lambda b, ci: (b, ci, 0)),
                   pl.BlockSpec((1, CONV_W - 1, D_D), lambda b, ci: (b, 0, 0)),
                   pl.BlockSpec((1, 1, D_D), lambda b, ci: (b, 0, 0))],
        out_shape=[jax.ShapeDtypeStruct((bsz, t, D_D), BF16),
                   jax.ShapeDtypeStruct((bsz, CONV_W - 1, D_D), F32),
                   jax.ShapeDtypeStruct((bsz, 1, D_D), F32)],
        scratch_shapes=[pltpu.VMEM((CONV_W - 1, D_D), F32), pltpu.VMEM((1, D_D), F32)],
        compiler_params=_params("parallel", "arbitrary"),
        name="rglru",
    )(z, z, conv0, h0.reshape(bsz, 1, D_D), p['conv_w'][j], vec(p['conv_b'][j]),
      p['rglru_wa'][j].astype(BF16), p['rglru_wx'][j].astype(BF16),
      vec(p['rglru_ba'][j]), vec(p['rglru_bx'][j]), vec(p['rglru_lambda'][j]))


def _tiles(bsz, t):
    if t >= 256:
        return dict(hgrn=(1, 256, 16), rwkv=(64, 1), ret=(256, 1), lru=256)
    nb = math.gcd(bsz, 4)
    return dict(hgrn=(nb, t, min(t, 16)), rwkv=(t, nb), ret=(t, nb), lru=t)


def _run_trunk(x, pos0, st_hgrn, st_rwkv, st_shift, st_ret, st_conv, st_h, p, wb):
    bsz, t, d = x.shape
    m = bsz * t
    tl = _tiles(bsz, t)
    x2 = x.reshape(m, d)
    hn = _rmsnorm_bf16(x2, p['norm_mix_pre'][0], tm=min(512, m))
    n_hgrn, n_rwkv, n_ret = None, None, None
    n_shift, n_conv, n_h = [], [], []
    sin, cos = _rope_table(pos0, t)
    for l in range(DEPTH):
        j = l // 2
        if l % 2 == 0:
            z_a = _in_proj(hn, p['w_in_even'], j, 4 * D_A, tn=1024, tm=1024).reshape(bsz, t, 4 * D_A)
            z_b = _in_proj(hn, wb['w_in_rwkv'], j, D_RWKV_PROJ, tn=D_RWKV_PROJ // 2, tm=512)
            z_b = z_b.reshape(bsz, t, D_RWKV_PROJ)
            bb, tc, sc = tl['hgrn']
            o_a, n_hgrn = _hgrn(z_a, p['hgrn_lb_raw'], p['hgrn_norm_w'][j], st_hgrn, n_hgrn, j, bb, tc, sc)
            o_b, n_rwkv, sh = _rwkv(z_b, st_rwkv, n_rwkv, st_shift[j], p, j, *tl['rwkv'])
            n_shift.append(sh.reshape(bsz, D_RWKV_PROJ))
            w_out = wb['w_out_even'][j]
        else:
            z = _in_proj(hn, p['w_in_odd'], j, 4 * D_C + 2 * D_D, tn=1024, tm=1024)
            z = z.reshape(bsz, t, 4 * D_C + 2 * D_D)
            o_a, n_ret = _retention(z, st_ret, n_ret, j, sin, cos, *tl['ret'])
            o_b, scv, shh = _rglru(z, st_conv[j], st_h[j], p, j, pos0, tl['lru'])
            n_conv.append(scv)
            n_h.append(shh.reshape(bsz, D_D))
            w_out = wb['w_out_odd'][j]
        x2, hn = _outproj(o_a.reshape(m, -1), o_b.reshape(m, -1), w_out, x2,
                          p['norm_mix_post'][l], p['norm_ffn_pre'][l], tm=min(512, m))
        next_pre = p['norm_mix_pre'][(l + 1) % DEPTH]
        act = _ffn_up(hn, p['w_ffn_in'], l)
        x2, hn = _ffn_down(act, wb['w_ffn_out'][l], x2, p['norm_ffn_post'][l], next_pre)
    return x2.reshape(bsz, t, d), (n_hgrn, n_rwkv, jnp.stack(n_shift),
                                   n_ret, jnp.stack(n_conv), jnp.stack(n_h))


def _zero_state(s, batch):
    return jnp.zeros((s.shape[0], batch) + s.shape[2:], s.dtype)


def kernel(x_prompt, x_sample, state_hgrn, state_rwkv, state_rwkv_shift, state_ret, state_rglru_conv, state_rglru_h, norm_mix_pre, norm_mix_post, norm_ffn_pre, norm_ffn_post, w_in_even, w_out_even, hgrn_lb_raw, hgrn_norm_w, rwkv_mu, rwkv_w0, rwkv_w2, rwkv_a0, rwkv_a2, rwkv_g2, rwkv_k_k, rwkv_k_a, rwkv_r_k, rwkv_ln_w, rwkv_ln_b, w_in_odd, w_out_odd, conv_w, conv_b, rglru_wa, rglru_ba, rglru_wx, rglru_bx, rglru_lambda, w_ffn_in, w_ffn_out):
    p = dict(norm_mix_pre=norm_mix_pre, norm_mix_post=norm_mix_post, norm_ffn_pre=norm_ffn_pre,
             norm_ffn_post=norm_ffn_post, hgrn_lb_raw=hgrn_lb_raw, hgrn_norm_w=hgrn_norm_w,
             rwkv_mu=rwkv_mu, rwkv_w0=rwkv_w0, rwkv_w2=rwkv_w2, rwkv_a0=rwkv_a0, rwkv_a2=rwkv_a2,
             rwkv_g2=rwkv_g2, rwkv_k_k=rwkv_k_k, rwkv_k_a=rwkv_k_a, rwkv_r_k=rwkv_r_k,
             rwkv_ln_w=rwkv_ln_w, rwkv_ln_b=rwkv_ln_b, conv_w=conv_w, conv_b=conv_b,
             rglru_wa=rglru_wa, rglru_ba=rglru_ba, rglru_wx=rglru_wx, rglru_bx=rglru_bx,
             rglru_lambda=rglru_lambda, w_in_even=w_in_even, w_in_odd=w_in_odd, w_ffn_in=w_ffn_in)
    wb = dict(w_in_rwkv=w_in_even[:, :, 4 * D_A:], w_out_even=w_out_even.astype(BF16),
              w_out_odd=w_out_odd.astype(BF16), w_ffn_out=w_ffn_out.astype(BF16))
    bp = x_prompt.shape[0]
    y_prompt, st_p = _run_trunk(x_prompt, 0,
                                _zero_state(state_hgrn, bp), _zero_state(state_rwkv, bp),
                                _zero_state(state_rwkv_shift, bp), _zero_state(state_ret, bp),
                                _zero_state(state_rglru_conv, bp), _zero_state(state_rglru_h, bp), p, wb)
    y_sample, st_s = _run_trunk(x_sample, PAST_LEN, state_hgrn, state_rwkv, state_rwkv_shift,
                                state_ret, state_rglru_conv, state_rglru_h, p, wb)
    hgrn_p, rwkv_p, shift_p, ret_p, conv_p, h_p = st_p
    hgrn_s, rwkv_s, shift_s, ret_s, conv_s, h_s = st_s
    return (y_prompt, y_sample, hgrn_p, hgrn_s, rwkv_p, rwkv_s, shift_p, shift_s,
            ret_p, ret_s, conv_p, conv_s, h_p, h_s)
```

```python
import functools
import math

import jax
import jax.numpy as jnp
from jax import lax
from jax.experimental import pallas as pl
from jax.experimental.pallas import tpu as pltpu

F32 = jnp.float32
BF16 = jnp.bfloat16

D_MODEL = 2048
DEPTH = 4
N_EVEN = (DEPTH + 1) // 2
N_ODD = DEPTH // 2
D_A = D_MODEL // 2
H_A = 8
DK_A = 128
DV_A = 128
D_B = D_MODEL // 2
N_B = 64
H_B = D_B // N_B
R_W = 64
R_A = 64
R_G = 128
D_RWKV_PROJ = 3 * D_B + R_W + R_A + R_G
D_C = D_MODEL // 2
H_C = 4
DK_C = D_C // H_C
DV_C = D_C // H_C
D_D = D_MODEL // 2
H_D = 4
BW_D = D_D // H_D
CONV_W = 4
LRU_C = 8.0
D_FF = ((8 * D_MODEL + 3 * 256 - 1) // (3 * 256)) * 256
RMS_EPS = 1e-6
GN_EPS = 64e-5
TINY = 1e-30
PAST_LEN = 16384

VMEM_LIMIT_BYTES = 56 * 1024 * 1024
LANES = 128


def _params(*sem):
    return pltpu.CompilerParams(dimension_semantics=sem, vmem_limit_bytes=VMEM_LIMIT_BYTES)


def _rms(x, w):
    return x * lax.rsqrt(jnp.mean(x * x, axis=-1, keepdims=True) + RMS_EPS) * w


def _sigmoid(x):
    return jax.nn.sigmoid(x)


def _silu(x):
    return x * jax.nn.sigmoid(x)


def _dot(a, b):
    return jnp.dot(a, b, preferred_element_type=F32)


def _dot_nt(a, b):
    return lax.dot_general(a, b, (((1,), (1,)), ((), ())), preferred_element_type=F32)


def _dot_tn(a, b):
    return lax.dot_general(a, b, (((0,), (0,)), ((), ())), preferred_element_type=F32)


def _split3(x):
    hi = x.astype(BF16)
    r1 = x - hi.astype(F32)
    mid = r1.astype(BF16)
    lo = (r1 - mid.astype(F32)).astype(BF16)
    return hi, mid, lo


def _dot_mask_lhs(mask, x):
    m = mask.astype(BF16)
    hi, mid, lo = _split3(x)
    return _dot(m, hi) + _dot(m, mid) + _dot(m, lo)


def _dot_mask_rhs(x, mask):
    m = mask.astype(BF16)
    hi, mid, lo = _split3(x)
    return _dot(hi, m) + _dot(mid, m) + _dot(lo, m)


def _lower_tri(n, strict):
    r = lax.broadcasted_iota(jnp.int32, (n, n), 0)
    c = lax.broadcasted_iota(jnp.int32, (n, n), 1)
    return (r > c) if strict else (r >= c)


def _rmsnorm_kernel(x_ref, w_ref, o_ref):
    o_ref[...] = _rms(x_ref[...], w_ref[...]).astype(o_ref.dtype)


def _rmsnorm_bf16(x, w, tm=512):
    m, d = x.shape
    return pl.pallas_call(
        _rmsnorm_kernel,
        grid=(m // tm,),
        in_specs=[pl.BlockSpec((tm, d), lambda i: (i, 0)), pl.BlockSpec((1, d), lambda i: (0, 0))],
        out_specs=pl.BlockSpec((tm, d), lambda i: (i, 0)),
        out_shape=jax.ShapeDtypeStruct((m, d), BF16),
        compiler_params=_params("parallel"),
        name="rmsnorm",
    )(x, w.reshape(1, d))


def _in_proj_kernel(a_ref, w_ref, o_ref, wb_ref):
    @pl.when(pl.program_id(1) == 0)
    def _():
        wb_ref[...] = w_ref[...].astype(BF16)

    o_ref[...] = _dot(a_ref[...], wb_ref[...])


def _in_proj(a, w_all, layer, n, tn, tm):
    m, k = a.shape
    tm = min(tm, m)
    return pl.pallas_call(
        _in_proj_kernel,
        grid=(n // tn, m // tm),
        in_specs=[pl.BlockSpec((tm, k), lambda j, i: (i, 0)),
                  pl.BlockSpec((None, k, tn), lambda j, i: (layer, 0, j))],
        out_specs=pl.BlockSpec((tm, tn), lambda j, i: (i, j)),
        out_shape=jax.ShapeDtypeStruct((m, n), F32),
        scratch_shapes=[pltpu.VMEM((k, tn), BF16)],
        compiler_params=_params("parallel", "arbitrary"),
        name="in_proj",
    )(a, w_all)


def _outproj_kernel(oa_ref, ob_ref, wa_ref, wb_ref, x_ref, post_ref, pre_ref, xo_ref, hn_ref):
    mix = _dot(oa_ref[...], wa_ref[...]) + _dot(ob_ref[...], wb_ref[...])
    xn = x_ref[...] + _rms(mix, post_ref[...])
    xo_ref[...] = xn
    hn_ref[...] = _rms(xn, pre_ref[...]).astype(hn_ref.dtype)


def _outproj(oa, ob, w_out, x, post_w, pre_w, tm=512):
    m, d = x.shape
    ka, kb = oa.shape[1], ob.shape[1]
    row = lambda i: (i, 0)
    fixed = lambda i: (0, 0)
    return pl.pallas_call(
        _outproj_kernel,
        grid=(m // tm,),
        in_specs=[pl.BlockSpec((tm, ka), row), pl.BlockSpec((tm, kb), row),
                  pl.BlockSpec((ka, d), fixed), pl.BlockSpec((kb, d), lambda i: (1, 0)),
                  pl.BlockSpec((tm, d), row), pl.BlockSpec((1, d), fixed), pl.BlockSpec((1, d), fixed)],
        out_specs=[pl.BlockSpec((tm, d), row), pl.BlockSpec((tm, d), row)],
        out_shape=[jax.ShapeDtypeStruct((m, d), F32), jax.ShapeDtypeStruct((m, d), BF16)],
        compiler_params=_params("parallel"),
        name="out_proj",
    )(oa, ob, w_out, w_out, x, post_w.reshape(1, d), pre_w.reshape(1, d))


def _ffn_up_kernel(h_ref, wg_ref, wu_ref, act_ref, wgb_ref, wub_ref):
    @pl.when(pl.program_id(1) == 0)
    def _():
        wgb_ref[...] = wg_ref[...].astype(BF16)
        wub_ref[...] = wu_ref[...].astype(BF16)

    h = h_ref[...]
    act_ref[...] = (_silu(_dot(h, wgb_ref[...])) * _dot(h, wub_ref[...])).astype(act_ref.dtype)


def _ffn_up(hn, w_in_all, layer, tm=1024, tf=512):
    m, d = hn.shape
    tm = min(tm, m)
    nf = D_FF // tf
    return pl.pallas_call(
        _ffn_up_kernel,
        grid=(nf, m // tm),
        in_specs=[pl.BlockSpec((tm, d), lambda j, i: (i, 0)),
                  pl.BlockSpec((None, d, tf), lambda j, i: (layer, 0, j)),
                  pl.BlockSpec((None, d, tf), lambda j, i: (layer, 0, j + nf))],
        out_specs=pl.BlockSpec((tm, tf), lambda j, i: (i, j)),
        out_shape=jax.ShapeDtypeStruct((m, D_FF), BF16),
        scratch_shapes=[pltpu.VMEM((d, tf), BF16), pltpu.VMEM((d, tf), BF16)],
        compiler_params=_params("parallel", "arbitrary"),
        name="ffn_up",
    )(hn, w_in_all, w_in_all)


def _ffn_down_kernel(act_ref, wo_ref, x_ref, post_ref, pre_ref, xo_ref, hn_ref):
    xn = x_ref[...] + _rms(_dot(act_ref[...], wo_ref[...]), post_ref[...])
    xo_ref[...] = xn
    hn_ref[...] = _rms(xn, pre_ref[...]).astype(hn_ref.dtype)


def _ffn_down(act, w_out, x, post_w, pre_w, tm=256):
    m, d = x.shape
    tm = min(tm, m)
    row = lambda i: (i, 0)
    fixed = lambda i: (0, 0)
    return pl.pallas_call(
        _ffn_down_kernel,
        grid=(m // tm,),
        in_specs=[pl.BlockSpec((tm, D_FF), row),
                  pl.BlockSpec((D_FF, d), fixed, pipeline_mode=pl.Buffered(1)),
                  pl.BlockSpec((tm, d), row), pl.BlockSpec((1, d), fixed), pl.BlockSpec((1, d), fixed)],
        out_specs=[pl.BlockSpec((tm, d), row), pl.BlockSpec((tm, d), row)],
        out_shape=[jax.ShapeDtypeStruct((m, d), F32), jax.ShapeDtypeStruct((m, d), BF16)],
        compiler_params=_params("parallel"),
        name="ffn_down",
    )(act, w_out, x, post_w.reshape(1, d), pre_w.reshape(1, d))


def _hgrn_kernel(q_ref, f_ref, i_ref, g_ref, lbraw_ref, nw_ref, s0_ref, o_ref, sout_ref, st_ref,
                 *, layer, sc, n_sub, bb):
    ci = pl.program_id(1)

    @pl.when(ci == 0)
    def _():
        for b in range(bb):
            for h in range(H_A):
                st_ref[b, h] = s0_ref[b, h].T

    raw = lbraw_ref[...]
    e = jnp.exp(raw - jnp.max(raw, axis=0, keepdims=True))
    pr = e / jnp.sum(e, axis=0, keepdims=True)
    lb_all = jnp.zeros((1, D_A), F32)
    for r in range(1, layer + 1):
        lb_all = lb_all + pr[r:r + 1, :]
    nw_all = nw_ref[...]
    tri = _lower_tri(sc, strict=False).astype(F32)
    row = lax.broadcasted_iota(jnp.int32, (sc, LANES), 0)
    sub = 8
    lane8 = lax.broadcasted_iota(jnp.int32, (sub, LANES), 1)

    oml = 1.0 - lb_all
    heads = [slice(h * DK_A, (h + 1) * DK_A) for h in range(H_A)]

    def body(idx, carry):
        b = idx // n_sub
        rows = pl.ds(pl.multiple_of((idx % n_sub) * sc, sc), sc)
        zf = f_ref[b, rows, :]
        iv = i_ref[b, rows, :]
        fg = lb_all + oml * _sigmoid(zf)
        logf = jnp.log(jnp.maximum(fg, TINY))
        kk = oml * _sigmoid(-zf)
        qs = _silu(q_ref[b, rows, :])
        bc = _dot_mask_lhs(tri, logf)
        blast = bc[sc - 1:sc, :]
        qe = (qs * jnp.exp(bc)).astype(BF16)
        kh = (kk * jnp.exp(blast - bc)).astype(BF16)
        ivb = iv.astype(BF16)
        keep = jnp.exp(blast)
        sts = [st_ref[b, h] for h in range(H_A)]
        o_inter = [_dot_nt(qe[:, sl], sts[h].astype(BF16)) for h, sl in enumerate(heads)]
        upd = [_dot_tn(ivb[:, sl], kh[:, sl]) for sl in heads]
        for h, sl in enumerate(heads):
            st_ref[b, h] = sts[h] * keep[:, sl] + upd[h]
        atts = []
        for h, sl in enumerate(heads):
            bc_h, kk_h, qs_h = bc[:, sl], kk[:, sl], qs[:, sl]
            att_t = [jnp.zeros((sub, LANES), F32)] * (sc // sub)
            for t in range(sc):
                n = (t // sub + 1) * sub
                dec = jnp.exp(bc_h[t:t + 1, :] - bc_h[:n])
                pair = jnp.where(row[:n] <= t, kk_h[:n] * dec * qs_h[t:t + 1, :], 0.0)
                a_col = jnp.sum(pair, axis=-1, keepdims=True)
                for k in range(n // sub):
                    att_t[k] = jnp.where(lane8 == t, a_col[k * sub:(k + 1) * sub], att_t[k])
            atts.append(jnp.concatenate(att_t, axis=0)[:, :sc].astype(BF16))
        o_intra = [_dot_tn(atts[h], ivb[:, sl]) for h, sl in enumerate(heads)]
        outs = []
        for h in range(H_A):
            o = o_intra[h] + o_inter[h]
            outs.append(o * lax.rsqrt(jnp.mean(o * o, axis=-1, keepdims=True) + RMS_EPS))
        o = jnp.concatenate(outs, axis=-1) * nw_all
        o_ref[b, rows, :] = (o * _silu(g_ref[b, rows, :])).astype(o_ref.dtype)
        return carry

    lax.fori_loop(0, bb * n_sub, body, 0, unroll=math.gcd(bb * n_sub, 4))

    @pl.when(ci == pl.num_programs(1) - 1)
    def _():
        for b in range(bb):
            for h in range(H_A):
                sout_ref[b, h] = st_ref[b, h].T


def _stacked_state_call(kernel_fn, prev, state_out_index, in_specs, args, **kw):
    if prev is None:
        return pl.pallas_call(kernel_fn, in_specs=in_specs, **kw)(*args)

    def with_prev(prev_ref, *refs):
        del prev_ref
        kernel_fn(*refs)

    return pl.pallas_call(with_prev, in_specs=[pl.BlockSpec(memory_space=pl.ANY)] + in_specs,
                          input_output_aliases={0: state_out_index}, **kw)(prev, *args)


def _hgrn(z_a, lb_raw, norm_w, s_all, prev, layer, bb, tc, sc):
    bsz, t, _ = z_a.shape
    col = lambda k: (lambda b, c: (b, c, k))
    fixed2 = lambda b, c: (0, 0)
    state = pl.BlockSpec((None, bb, H_A, DK_A, DV_A), lambda b, c: (layer, b, 0, 0, 0))
    return _stacked_state_call(
        functools.partial(_hgrn_kernel, layer=layer, sc=sc, n_sub=tc // sc, bb=bb), prev, 1,
        [pl.BlockSpec((bb, tc, D_A), col(0)), pl.BlockSpec((bb, tc, D_A), col(1)),
         pl.BlockSpec((bb, tc, D_A), col(2)), pl.BlockSpec((bb, tc, D_A), col(3)),
         pl.BlockSpec((N_EVEN, D_A), fixed2), pl.BlockSpec((1, D_A), fixed2), state],
        (z_a, z_a, z_a, z_a, lb_raw, norm_w.reshape(1, D_A), s_all),
        grid=(bsz // bb, t // tc),
        out_specs=[pl.BlockSpec((bb, tc, D_A), lambda b, c: (b, c, 0)), state],
        out_shape=[jax.ShapeDtypeStruct((bsz, t, D_A), BF16),
                   jax.ShapeDtypeStruct(s_all.shape, F32)],
        scratch_shapes=[pltpu.VMEM((bb, H_A, DV_A, DK_A), F32)],
        compiler_params=_params("parallel", "arbitrary"),
        name="hgrn2",
    )


def _head_sums(x):
    rows = x.shape[0]
    seg = (lax.broadcasted_iota(jnp.int32, (LANES, LANES), 0) // N_B
           == lax.broadcasted_iota(jnp.int32, (LANES, LANES), 1) // N_B).astype(F32)
    n_tiles = D_B // LANES
    xs = jnp.concatenate([x[:, i * LANES:(i + 1) * LANES] for i in range(n_tiles)], axis=0)
    s = _dot_mask_rhs(xs, seg)
    return jnp.concatenate([s[i * rows:(i + 1) * rows] for i in range(n_tiles)], axis=1)


def _rwkv_kernel(z_ref, sh0_ref, s0_ref, mu_ref, w0_ref, w2_ref, a0_ref, a2_ref, g2_ref, kk_ref, ka_ref,
                 rk_ref, lnw_ref, lnb_ref, o_ref, sout_ref, shout_ref, s_ref, prev_ref, *, c, nb):
    ci = pl.program_id(1)
    rows = nb * c
    dz = D_RWKV_PROJ

    @pl.when(ci == 0)
    def _():
        s_ref[...] = s0_ref[...]
        prev_ref[...] = sh0_ref[...]

    blocks = [slice(b * c, (b + 1) * c) for b in range(nb)]
    last = [slice((b + 1) * c - 1, (b + 1) * c) for b in range(nb)]

    def per_block(rows_of):
        n = rows_of[0].shape[-1]
        blk = lax.broadcasted_iota(jnp.int32, (rows, n), 0) // c
        out = jnp.broadcast_to(rows_of[0], (rows, n))
        for b in range(1, nb):
            out = jnp.where(blk == b, rows_of[b], out)
        return out

    zb = z_ref[...].reshape(rows, dz)
    row = lax.broadcasted_iota(jnp.int32, (rows, dz), 0)
    prev = jnp.where(row % c == 0, per_block([prev_ref[b] for b in range(nb)]), pltpu.roll(zb, 1, axis=0))
    for b in range(nb):
        prev_ref[b] = zb[last[b], :]
    zs = zb + (prev - zb) * mu_ref[...]
    r = zs[:, 0:D_B]
    kb = zs[:, D_B:2 * D_B]
    v = zs[:, 2 * D_B:3 * D_B]
    o0 = 3 * D_B
    wl = zs[:, o0:o0 + R_W]
    al = zs[:, o0 + R_W:o0 + R_W + R_A]
    gl = zs[:, o0 + R_W + R_A:o0 + R_W + R_A + R_G]
    wlin = w0_ref[...] + _dot(jnp.tanh(wl).astype(BF16), w2_ref[...])
    w_log = -math.exp(-0.5) * _sigmoid(wlin)
    a = _sigmoid(a0_ref[...] + _dot(al.astype(BF16), a2_ref[...]))
    gate = _dot(_sigmoid(gl).astype(BF16), g2_ref[...])
    kkr = kb * kk_ref[...]
    kmod = kb * (1.0 + (a - 1.0) * ka_ref[...])

    ri = lax.broadcasted_iota(jnp.int32, (rows, rows), 0)
    cj = lax.broadcasted_iota(jnp.int32, (rows, rows), 1)
    tri = jnp.where(ri // c == cj // c, (ri >= cj).astype(F32), 0.0)
    bc = _dot_mask_lhs(tri, w_log)
    b_last = [bc[l, :] for l in last]
    blast = per_block(b_last)
    gam = jnp.exp(bc)
    inv_gam = jnp.exp(-bc)
    gam_prev = jnp.exp(bc - w_log)
    to_end = jnp.exp(blast - bc)
    gam_last = [jnp.exp(x) for x in b_last]
    kkn = kkr / jnp.maximum(jnp.sqrt(_head_sums(kkr * kkr)), 1e-12)
    beta = kkn * a
    xa = -kkn * gam_prev
    xr = r * gam
    yb = beta * inv_gam
    yk = kmod * inv_gam
    eb = beta * to_end
    ek = kmod * to_end
    lhs = [jnp.concatenate([xa[bl], xr[bl]], axis=0).astype(BF16) for bl in blocks]
    rhs = [jnp.concatenate([yb[bl], yk[bl]], axis=0).astype(BF16) for bl in blocks]
    end = [jnp.concatenate([eb[bl], ek[bl]], axis=0).astype(BF16) for bl in blocks]
    vb = [v[bl].astype(BF16) for bl in blocks]

    ri2 = lax.broadcasted_iota(jnp.int32, (2 * c, 2 * c), 0)
    cj2 = lax.broadcasted_iota(jnp.int32, (2 * c, 2 * c), 1)
    keep = (ri2 % c + ri2 // c) > (cj2 % c)
    n_steps = max(1, int(math.log2(c)))
    heads = [slice(h * N_B, (h + 1) * N_B) for h in range(H_B)]
    pairs = [(b, h) for b in range(nb) for h in range(H_B)]
    idx = range(len(pairs))
    s0 = [s_ref[b, h] for b, h in pairs]
    res = [_dot_nt(lhs[b][:, heads[h]], jnp.concatenate([rhs[b][:, heads[h]], s0[i].astype(BF16)], axis=0))
           for i, (b, h) in enumerate(pairs)]
    att = [jnp.where(keep, res[i][:, :2 * c], 0.0) for i in idx]
    from_s0 = [res[i][:, 2 * c:] for i in idx]
    u = [from_s0[i][:c] + _dot(att[i][:c, c:].astype(BF16), vb[b][:, heads[h]]) for i, (b, h) in enumerate(pairs)]
    pb = [att[i][:c, :c].astype(BF16) for i in idx]
    for it in range(n_steps):
        u = [u[i] + _dot(pb[i], u[i].astype(BF16)) for i in idx]
        if it + 1 < n_steps:
            pb = [_dot(pb[i], pb[i]).astype(BF16) for i in idx]
    uv = [jnp.concatenate([u[i].astype(BF16), vb[b][:, heads[h]]], axis=0) for i, (b, h) in enumerate(pairs)]
    ys = [from_s0[i][c:] + _dot(att[i][c:].astype(BF16), uv[i]) for i in idx]
    s_new = [_dot_tn(uv[i], end[b][:, heads[h]]) for i, (b, h) in enumerate(pairs)]
    for i, (b, h) in enumerate(pairs):
        s_ref[b, h] = s0[i] * gam_last[b][:, heads[h]] + s_new[i]

    y = jnp.concatenate([jnp.concatenate(ys[b * H_B:(b + 1) * H_B], axis=-1) for b in range(nb)], axis=0)
    yc = y - _head_sums(y) * (1.0 / N_B)
    var = _head_sums(yc * yc) * (1.0 / N_B)
    yn = yc * lax.rsqrt(var + GN_EPS) * lnw_ref[...] + lnb_ref[...]
    out = (yn + _head_sums(r * kmod * rk_ref[...]) * v) * gate
    o_ref[...] = out.reshape(nb, c, D_B).astype(o_ref.dtype)

    @pl.when(ci == pl.num_programs(1) - 1)
    def _():
        sout_ref[...] = s_ref[...]
        for b in range(nb):
            shout_ref[b] = zb[last[b], :]


def _rwkv(z_b, s_all, prev, shift0, p, j, c, nb):
    bsz, t, dz = z_b.shape
    vec = lambda x: x.reshape(1, -1)
    fixed2 = lambda b, ci: (0, 0)
    vspec = lambda n: pl.BlockSpec((1, n), fixed2)
    state = pl.BlockSpec((None, nb, H_B, N_B, N_B), lambda b, ci: (j, b, 0, 0, 0))
    shift = pl.BlockSpec((nb, 1, dz), lambda b, ci: (b, 0, 0))
    return _stacked_state_call(
        functools.partial(_rwkv_kernel, c=c, nb=nb), prev, 1,
        [pl.BlockSpec((nb, c, dz), lambda b, ci: (b, ci, 0)), shift, state,
         vspec(dz), vspec(D_B),
         pl.BlockSpec((R_W, D_B), fixed2), vspec(D_B),
         pl.BlockSpec((R_A, D_B), fixed2), pl.BlockSpec((R_G, D_B), fixed2),
         vspec(D_B), vspec(D_B), vspec(D_B), vspec(D_B), vspec(D_B)],
        (z_b, shift0.reshape(bsz, 1, dz), s_all, vec(p['rwkv_mu'][j]), vec(p['rwkv_w0'][j]),
         p['rwkv_w2'][j].astype(BF16), vec(p['rwkv_a0'][j]), p['rwkv_a2'][j].astype(BF16),
         p['rwkv_g2'][j].astype(BF16), vec(p['rwkv_k_k'][j]), vec(p['rwkv_k_a'][j]),
         vec(p['rwkv_r_k'][j]), vec(p['rwkv_ln_w'][j]), vec(p['rwkv_ln_b'][j])),
        grid=(bsz // nb, t // c),
        out_specs=[pl.BlockSpec((nb, c, D_B), lambda b, ci: (b, ci, 0)), state, shift],
        out_shape=[jax.ShapeDtypeStruct((bsz, t, D_B), BF16),
                   jax.ShapeDtypeStruct(s_all.shape, F32),
                   jax.ShapeDtypeStruct((bsz, 1, dz), F32)],
        scratch_shapes=[pltpu.VMEM((nb, H_B, N_B, N_B), F32), pltpu.VMEM((nb, 1, dz), F32)],
        compiler_params=_params("parallel", "arbitrary"),
        name="rwkv7",
    )


def _rope_kernel(ang_ref, sin_ref, cos_ref, *, tt, pos0):
    row = lax.broadcasted_iota(jnp.int32, (tt, DK_C), 0)
    ang = (pos0 + pl.program_id(0) * tt + row).astype(F32) * ang_ref[...]
    sin_ref[...] = jnp.sin(ang)
    cos_ref[...] = jnp.cos(ang)


def _rope_table(pos0, t):
    angle = 1.0 / (10000.0 ** jnp.linspace(0.0, 1.0, DK_C // 2, dtype=F32))
    angle = jnp.repeat(angle, 2).reshape(1, DK_C)
    tt = min(t, 256)
    out = pl.BlockSpec((tt, DK_C), lambda i: (i, 0))
    return pl.pallas_call(
        functools.partial(_rope_kernel, tt=tt, pos0=pos0),
        grid=(t // tt,),
        in_specs=[pl.BlockSpec((1, DK_C), lambda i: (0, 0))],
        out_specs=[out, out],
        out_shape=[jax.ShapeDtypeStruct((t, DK_C), F32)] * 2,
        compiler_params=_params("parallel"),
        name="rope_table",
    )(angle)


def _ret_kernel(q_ref, k_ref, v_ref, g_ref, sin_ref, cos_ref, s0_ref, o_ref, sout_ref, s_ref, *, c, nb):
    ci = pl.program_id(1)

    @pl.when(ci == 0)
    def _():
        s_ref[...] = s0_ref[...]

    sin = jnp.concatenate([sin_ref[...]] * H_C, axis=1)
    cos = jnp.concatenate([cos_ref[...]] * H_C, axis=1)
    even = (lax.broadcasted_iota(jnp.int32, (c, D_C), 1) % 2) == 0

    def rope(x):
        rot = jnp.where(even, -pltpu.roll(x, D_C - 1, axis=1), pltpu.roll(x, 1, axis=1))
        return x * cos + rot * sin

    ti = lax.broadcasted_iota(jnp.int32, (c, c), 0)
    si = lax.broadcasted_iota(jnp.int32, (c, c), 1)
    causal = ti >= si
    dist = jnp.where(causal, (ti - si).astype(F32), 0.0)
    tcol = lax.broadcasted_iota(jnp.int32, (c, 1), 0).astype(F32)
    heads = [slice(h * DK_C, (h + 1) * DK_C) for h in range(H_C)]
    log_gamma = [math.log1p(-2.0 ** (-5.0 - h)) for h in range(H_C)]
    dec = [jnp.where(causal, jnp.exp(dist * lg), 0.0) for lg in log_gamma]
    from_start = [jnp.exp((tcol + 1.0) * lg) for lg in log_gamma]
    to_end = [jnp.exp((float(c) - 1.0 - tcol) * lg) for lg in log_gamma]
    keep = [math.exp(float(c) * lg) for lg in log_gamma]

    pairs = [(b, h) for b in range(nb) for h in range(H_C)]
    qs = [rope(q_ref[b]) for b in range(nb)]
    ks = [rope(k_ref[b]) * (DK_C ** -0.5) for b in range(nb)]
    vb = [v_ref[b].astype(BF16) for b in range(nb)]
    qb = [q.astype(BF16) for q in qs]
    kb = [k.astype(BF16) for k in ks]
    s_old = [s_ref[b, h] for b, h in pairs]
    att = [_dot_nt(qb[b][:, heads[h]], kb[b][:, heads[h]]) for b, h in pairs]
    o_st = [_dot((qs[b][:, heads[h]] * from_start[h]).astype(BF16), s_old[i].astype(BF16))
            for i, (b, h) in enumerate(pairs)]
    o_in = [_dot((att[i] * dec[h]).astype(BF16), vb[b][:, heads[h]]) for i, (b, h) in enumerate(pairs)]
    upd = [_dot_tn((ks[b][:, heads[h]] * to_end[h]).astype(BF16), vb[b][:, heads[h]]) for b, h in pairs]
    s_new = [keep[h] * s_old[i] + upd[i] for i, (b, h) in enumerate(pairs)]
    for i, (b, h) in enumerate(pairs):
        s_ref[b, h] = s_new[i]
    for b in range(nb):
        outs = []
        for h in range(H_C):
            o = o_in[b * H_C + h] + o_st[b * H_C + h]
            outs.append(o * lax.rsqrt(jnp.mean(o * o, axis=-1, keepdims=True) + RMS_EPS))
        o_ref[b] = (jnp.concatenate(outs, axis=-1) * _silu(g_ref[b])).astype(o_ref.dtype)

    @pl.when(ci == pl.num_programs(1) - 1)
    def _():
        for i, (b, h) in enumerate(pairs):
            sout_ref[b, h] = s_new[i]


def _retention(z, s_all, prev, j, sin, cos, c, nb):
    bsz, t, _ = z.shape
    col = lambda k: (lambda b, ci: (b, ci, k))
    state = pl.BlockSpec((None, nb, H_C, DK_C, DV_C), lambda b, ci: (j, b, 0, 0, 0))
    table = pl.BlockSpec((c, DK_C), lambda b, ci: (ci, 0))
    return _stacked_state_call(
        functools.partial(_ret_kernel, c=c, nb=nb), prev, 1,
        [pl.BlockSpec((nb, c, D_C), col(0)), pl.BlockSpec((nb, c, D_C), col(1)),
         pl.BlockSpec((nb, c, D_C), col(2)), pl.BlockSpec((nb, c, D_C), col(3)),
         table, table, state],
        (z, z, z, z, sin, cos, s_all),
        grid=(bsz // nb, t // c),
        out_specs=[pl.BlockSpec((nb, c, D_C), lambda b, ci: (b, ci, 0)), state],
        out_shape=[jax.ShapeDtypeStruct((bsz, t, D_C), BF16), jax.ShapeDtypeStruct(s_all.shape, F32)],
        scratch_shapes=[pltpu.VMEM((nb, H_C, DK_C, DV_C), F32)],
        compiler_params=_params("parallel", "arbitrary"),
        name="retention",
    )


def _rglru_kernel(y_ref, x_ref, conv0_ref, h0_ref, cw_ref, cb_ref, wa_ref, wx_ref, ba_ref, bx_ref, lam_ref,
                  o_ref, convout_ref, hout_ref, tail_ref, h_ref, *, tc, nb, pos0):
    ci = pl.program_id(1)
    rows = nb * tc
    sub = 8

    @pl.when(ci == 0)
    def _():
        tail_ref[...] = conv0_ref[...]
        h_ref[...] = h0_ref[...]

    x = x_ref[...].reshape(rows, D_D)
    row = lax.broadcasted_iota(jnp.int32, (rows, D_D), 0)
    t_in = row % tc
    tails = [tail_ref[b] for b in range(nb)]

    def shifted(d):
        r = pltpu.roll(x, d, axis=0)
        for m in range(d):
            k = CONV_W - 1 - d + m
            for b in range(nb):
                r = jnp.where(row == b * tc + m, tails[b][k:k + 1, :], r)
        return r

    cw = cw_ref[...]
    xc = cb_ref[...] + cw[CONV_W - 1:CONV_W, :] * x
    for d in range(1, CONV_W):
        xc = xc + cw[CONV_W - 1 - d:CONV_W - d, :] * shifted(d)
    new_tails = [x[(b + 1) * tc - (CONV_W - 1):(b + 1) * tc, :] for b in range(nb)]
    for b in range(nb):
        tail_ref[b] = new_tails[b]

    ra, rx = [], []
    for hd in range(H_D):
        xh = xc[:, hd * BW_D:(hd + 1) * BW_D].astype(BF16)
        ra.append(_dot(xh, wa_ref[hd]))
        rx.append(_dot(xh, wx_ref[hd]))
    rg = _sigmoid(jnp.concatenate(ra, axis=-1) + ba_ref[...])
    ig = _sigmoid(jnp.concatenate(rx, axis=-1) + bx_ref[...])
    nlam = -lam_ref[...]
    softplus = jnp.maximum(nlam, 0.0) + jnp.log1p(jnp.exp(-jnp.abs(nlam)))
    log_a = -LRU_C * rg * softplus
    a = jnp.exp(log_a)
    mult = jnp.sqrt(-jnp.tanh(log_a) * (a * a + 1.0))
    mult = jnp.where(pos0 + ci * tc + t_in == 0, 1.0, mult)
    bt = mult * ig * xc

    in_sub = row % sub
    d = 1
    while d < sub:
        m = in_sub >= d
        bt = jnp.where(m, a * pltpu.roll(bt, d, axis=0) + bt, bt)
        a = jnp.where(m, a * pltpu.roll(a, d, axis=0), a)
        d *= 2
    hs = []
    for b in range(nb):
        carry = h_ref[b]
        for k in range(tc // sub):
            sl = slice(b * tc + k * sub, b * tc + (k + 1) * sub)
            h = bt[sl] + a[sl] * carry
            hs.append(h)
            carry = h[sub - 1:sub, :]
        h_ref[b] = carry
    hseq = jnp.concatenate(hs, axis=0)
    y = y_ref[...].reshape(rows, D_D)
    gelu = 0.5 * y * (1.0 + jnp.tanh(math.sqrt(2.0 / math.pi) * (y + 0.044715 * (y * y * y))))
    o_ref[...] = (hseq * gelu).reshape(nb, tc, D_D).astype(o_ref.dtype)

    @pl.when(ci == pl.num_programs(1) - 1)
    def _():
        for b in range(nb):
            convout_ref[b] = new_tails[b]
        hout_ref[...] = h_ref[...]


def _rglru(z, conv0, h0, p, j, pos0, tc, nb):
    bsz, t, _ = z.shape
    vec = lambda x: x.reshape(1, -1)
    fixed2 = lambda b, ci: (0, 0)
    fixed3 = lambda b, ci: (0, 0, 0)
    vspec = pl.BlockSpec((1, D_D), fixed2)
    ycol = 4 * D_C // D_D
    conv = pl.BlockSpec((nb, CONV_W - 1, D_D), lambda b, ci: (b, 0, 0))
    hid = pl.BlockSpec((nb, 1, D_D), lambda b, ci: (b, 0, 0))
    return pl.pallas_call(
        functools.partial(_rglru_kernel, tc=tc, nb=nb, pos0=pos0),
        grid=(bsz // nb, t // tc),
        in_specs=[pl.BlockSpec((nb, tc, D_D), lambda b, ci: (b, ci, ycol)),
                  pl.BlockSpec((nb, tc, D_D), lambda b, ci: (b, ci, ycol + 1)),
                  conv, hid,
                  pl.BlockSpec((CONV_W, D_D), fixed2), vspec,
                  pl.BlockSpec((H_D, BW_D, BW_D), fixed3), pl.BlockSpec((H_D, BW_D, BW_D), fixed3),
                  vspec, vspec, vspec],
        out_specs=[pl.BlockSpec((nb, tc, D_D), lambda b, ci: (b, ci, 0)), conv, hid],
        out_shape=[jax.ShapeDtypeStruct((bsz, t, D_D), BF16),
                   jax.ShapeDtypeStruct((bsz, CONV_W - 1, D_D), F32),
                   jax.ShapeDtypeStruct((bsz, 1, D_D), F32)],
        scratch_shapes=[pltpu.VMEM((nb, CONV_W - 1, D_D), F32), pltpu.VMEM((nb, 1, D_D), F32)],
        compiler_params=_params("parallel", "arbitrary"),
        name="rglru",
    )(z, z, conv0, h0.reshape(bsz, 1, D_D), p['conv_w'][j], vec(p['conv_b'][j]),
      p['rglru_wa'][j].astype(BF16), p['rglru_wx'][j].astype(BF16),
      vec(p['rglru_ba'][j]), vec(p['rglru_bx'][j]), vec(p['rglru_lambda'][j]))


def _tiles(bsz, t):
    if t >= 256:
        return dict(hgrn=(1, 256, 16), rwkv=(64, 1), ret=(256, 1), lru=(256, 1))
    nb = math.gcd(bsz, 4)
    return dict(hgrn=(nb, t, min(t, 8)), rwkv=(t, nb), ret=(t, nb), lru=(t, nb))


def _run_trunk(x, pos0, st_hgrn, st_rwkv, st_shift, st_ret, st_conv, st_h, p, wb):
    bsz, t, d = x.shape
    m = bsz * t
    tl = _tiles(bsz, t)
    x2 = x.reshape(m, d)
    hn = _rmsnorm_bf16(x2, p['norm_mix_pre'][0], tm=min(512, m))
    n_hgrn, n_rwkv, n_ret = None, None, None
    n_shift, n_conv, n_h = [], [], []
    sin, cos = _rope_table(pos0, t)
    for l in range(DEPTH):
        j = l // 2
        if l % 2 == 0:
            z_a = _in_proj(hn, p['w_in_even'], j, 4 * D_A, tn=1024, tm=1024).reshape(bsz, t, 4 * D_A)
            z_b = _in_proj(hn, wb['w_in_rwkv'], j, D_RWKV_PROJ, tn=D_RWKV_PROJ // 2, tm=512)
            z_b = z_b.reshape(bsz, t, D_RWKV_PROJ)
            bb, tc, sc = tl['hgrn']
            o_a, n_hgrn = _hgrn(z_a, p['hgrn_lb_raw'], p['hgrn_norm_w'][j], st_hgrn, n_hgrn, j, bb, tc, sc)
            o_b, n_rwkv, sh = _rwkv(z_b, st_rwkv, n_rwkv, st_shift[j], p, j, *tl['rwkv'])
            n_shift.append(sh.reshape(bsz, D_RWKV_PROJ))
            w_out = wb['w_out_even'][j]
        else:
            z = _in_proj(hn, p['w_in_odd'], j, 4 * D_C + 2 * D_D, tn=1024, tm=1024)
            z = z.reshape(bsz, t, 4 * D_C + 2 * D_D)
            o_a, n_ret = _retention(z, st_ret, n_ret, j, sin, cos, *tl['ret'])
            o_b, scv, shh = _rglru(z, st_conv[j], st_h[j], p, j, pos0, *tl['lru'])
            n_conv.append(scv)
            n_h.append(shh.reshape(bsz, D_D))
            w_out = wb['w_out_odd'][j]
        x2, hn = _outproj(o_a.reshape(m, -1), o_b.reshape(m, -1), w_out, x2,
                          p['norm_mix_post'][l], p['norm_ffn_pre'][l], tm=min(512, m))
        next_pre = p['norm_mix_pre'][(l + 1) % DEPTH]
        act = _ffn_up(hn, p['w_ffn_in'], l)
        x2, hn = _ffn_down(act, wb['w_ffn_out'][l], x2, p['norm_ffn_post'][l], next_pre)
    return x2.reshape(bsz, t, d), (n_hgrn, n_rwkv, jnp.stack(n_shift),
                                   n_ret, jnp.stack(n_conv), jnp.stack(n_h))


def _zero_state(s, batch):
    return jnp.zeros((s.shape[0], batch) + s.shape[2:], s.dtype)


def kernel(x_prompt, x_sample, state_hgrn, state_rwkv, state_rwkv_shift, state_ret, state_rglru_conv, state_rglru_h, norm_mix_pre, norm_mix_post, norm_ffn_pre, norm_ffn_post, w_in_even, w_out_even, hgrn_lb_raw, hgrn_norm_w, rwkv_mu, rwkv_w0, rwkv_w2, rwkv_a0, rwkv_a2, rwkv_g2, rwkv_k_k, rwkv_k_a, rwkv_r_k, rwkv_ln_w, rwkv_ln_b, w_in_odd, w_out_odd, conv_w, conv_b, rglru_wa, rglru_ba, rglru_wx, rglru_bx, rglru_lambda, w_ffn_in, w_ffn_out):
    p = dict(norm_mix_pre=norm_mix_pre, norm_mix_post=norm_mix_post, norm_ffn_pre=norm_ffn_pre,
             norm_ffn_post=norm_ffn_post, hgrn_lb_raw=hgrn_lb_raw, hgrn_norm_w=hgrn_norm_w,
             rwkv_mu=rwkv_mu, rwkv_w0=rwkv_w0, rwkv_w2=rwkv_w2, rwkv_a0=rwkv_a0, rwkv_a2=rwkv_a2,
             rwkv_g2=rwkv_g2, rwkv_k_k=rwkv_k_k, rwkv_k_a=rwkv_k_a, rwkv_r_k=rwkv_r_k,
             rwkv_ln_w=rwkv_ln_w, rwkv_ln_b=rwkv_ln_b, conv_w=conv_w, conv_b=conv_b,
             rglru_wa=rglru_wa, rglru_ba=rglru_ba, rglru_wx=rglru_wx, rglru_bx=rglru_bx,
             rglru_lambda=rglru_lambda, w_in_even=w_in_even, w_in_odd=w_in_odd, w_ffn_in=w_ffn_in)
    wb = dict(w_in_rwkv=w_in_even[:, :, 4 * D_A:], w_out_even=w_out_even.astype(BF16),
              w_out_odd=w_out_odd.astype(BF16), w_ffn_out=w_ffn_out.astype(BF16))
    bp = x_prompt.shape[0]
    y_prompt, st_p = _run_trunk(x_prompt, 0,
                                _zero_state(state_hgrn, bp), _zero_state(state_rwkv, bp),
                                _zero_state(state_rwkv_shift, bp), _zero_state(state_ret, bp),
                                _zero_state(state_rglru_conv, bp), _zero_state(state_rglru_h, bp), p, wb)
    y_sample, st_s = _run_trunk(x_sample, PAST_LEN, state_hgrn, state_rwkv, state_rwkv_shift,
                                state_ret, state_rglru_conv, state_rglru_h, p, wb)
    hgrn_p, rwkv_p, shift_p, ret_p, conv_p, h_p = st_p
    hgrn_s, rwkv_s, shift_s, ret_s, conv_s, h_s = st_s
    return (y_prompt, y_sample, hgrn_p, hgrn_s, rwkv_p, rwkv_s, shift_p, shift_s,
            ret_p, ret_s, conv_p, conv_s, h_p, h_s)
```

```python
import functools
import math

import jax
import jax.numpy as jnp
from jax import lax
from jax.experimental import pallas as pl
from jax.experimental.pallas import tpu as pltpu

F32 = jnp.float32
BF16 = jnp.bfloat16

D_MODEL = 2048
DEPTH = 4
N_EVEN = (DEPTH + 1) // 2
N_ODD = DEPTH // 2
D_A = D_MODEL // 2
H_A = 8
DK_A = 128
DV_A = 128
D_B = D_MODEL // 2
N_B = 64
H_B = D_B // N_B
R_W = 64
R_A = 64
R_G = 128
D_RWKV_PROJ = 3 * D_B + R_W + R_A + R_G
D_C = D_MODEL // 2
H_C = 4
DK_C = D_C // H_C
DV_C = D_C // H_C
D_D = D_MODEL // 2
H_D = 4
BW_D = D_D // H_D
CONV_W = 4
LRU_C = 8.0
D_FF = ((8 * D_MODEL + 3 * 256 - 1) // (3 * 256)) * 256
RMS_EPS = 1e-6
GN_EPS = 64e-5
TINY = 1e-30
PAST_LEN = 16384

VMEM_LIMIT_BYTES = 56 * 1024 * 1024
LANES = 128


def _params(*sem):
    return pltpu.CompilerParams(dimension_semantics=sem, vmem_limit_bytes=VMEM_LIMIT_BYTES)


def _rms(x, w):
    return x * lax.rsqrt(jnp.mean(x * x, axis=-1, keepdims=True) + RMS_EPS) * w


def _sigmoid(x):
    return jax.nn.sigmoid(x)


def _silu(x):
    return x * jax.nn.sigmoid(x)


def _dot(a, b):
    return jnp.dot(a, b, preferred_element_type=F32)


def _dot_nt(a, b):
    return lax.dot_general(a, b, (((1,), (1,)), ((), ())), preferred_element_type=F32)


def _dot_tn(a, b):
    return lax.dot_general(a, b, (((0,), (0,)), ((), ())), preferred_element_type=F32)


def _split(x, pieces):
    out = []
    for _ in range(pieces - 1):
        part = x.astype(BF16)
        out.append(part)
        x = x - part.astype(F32)
    out.append(x.astype(BF16))
    return out


def _dot_mask_lhs(mask, x):
    m = mask.astype(BF16)
    return sum(_dot(m, part) for part in _split(x, 3))


def _dot_mask_rhs(x, mask, pieces):
    m = mask.astype(BF16)
    return sum(_dot(part, m) for part in _split(x, pieces))


def _row_parts(n, size):
    size = min(size, n)
    return [slice(i, i + size) for i in range(0, n, size)]


def _lower_tri(n, strict):
    r = lax.broadcasted_iota(jnp.int32, (n, n), 0)
    c = lax.broadcasted_iota(jnp.int32, (n, n), 1)
    return (r > c) if strict else (r >= c)


def _rmsnorm_kernel(x_ref, w_ref, o_ref):
    o_ref[...] = _rms(x_ref[...], w_ref[...]).astype(o_ref.dtype)


def _rmsnorm_bf16(x, w, tm=512):
    m, d = x.shape
    return pl.pallas_call(
        _rmsnorm_kernel,
        grid=(m // tm,),
        in_specs=[pl.BlockSpec((tm, d), lambda i: (i, 0)), pl.BlockSpec((1, d), lambda i: (0, 0))],
        out_specs=pl.BlockSpec((tm, d), lambda i: (i, 0)),
        out_shape=jax.ShapeDtypeStruct((m, d), BF16),
        compiler_params=_params("parallel"),
        name="rmsnorm",
    )(x, w.reshape(1, d))


def _in_proj_kernel(a_ref, w_ref, o_ref, wb_ref):
    @pl.when(pl.program_id(1) == 0)
    def _():
        wb_ref[...] = w_ref[...].astype(BF16)

    o_ref[...] = _dot(a_ref[...], wb_ref[...])


def _in_proj(a, w_all, layer, n, tn, tm):
    m, k = a.shape
    tm = min(tm, m)
    return pl.pallas_call(
        _in_proj_kernel,
        grid=(n // tn, m // tm),
        in_specs=[pl.BlockSpec((tm, k), lambda j, i: (i, 0)),
                  pl.BlockSpec((None, k, tn), lambda j, i: (layer, 0, j))],
        out_specs=pl.BlockSpec((tm, tn), lambda j, i: (i, j)),
        out_shape=jax.ShapeDtypeStruct((m, n), F32),
        scratch_shapes=[pltpu.VMEM((k, tn), BF16)],
        compiler_params=_params("parallel", "arbitrary"),
        name="in_proj",
    )(a, w_all)


def _outproj_kernel(oa_ref, ob_ref, wa_ref, wb_ref, x_ref, post_ref, pre_ref, xo_ref, hn_ref):
    for rows in _row_parts(x_ref.shape[0], 256):
        mix = _dot(oa_ref[rows, :], wa_ref[...]) + _dot(ob_ref[rows, :], wb_ref[...])
        xn = x_ref[rows, :] + _rms(mix, post_ref[...])
        xo_ref[rows, :] = xn
        hn_ref[rows, :] = _rms(xn, pre_ref[...]).astype(hn_ref.dtype)


def _outproj(oa, ob, w_out, x, post_w, pre_w, tm=512):
    m, d = x.shape
    ka, kb = oa.shape[1], ob.shape[1]
    row = lambda i: (i, 0)
    fixed = lambda i: (0, 0)
    return pl.pallas_call(
        _outproj_kernel,
        grid=(m // tm,),
        in_specs=[pl.BlockSpec((tm, ka), row), pl.BlockSpec((tm, kb), row),
                  pl.BlockSpec((ka, d), fixed), pl.BlockSpec((kb, d), lambda i: (1, 0)),
                  pl.BlockSpec((tm, d), row), pl.BlockSpec((1, d), fixed), pl.BlockSpec((1, d), fixed)],
        out_specs=[pl.BlockSpec((tm, d), row), pl.BlockSpec((tm, d), row)],
        out_shape=[jax.ShapeDtypeStruct((m, d), F32), jax.ShapeDtypeStruct((m, d), BF16)],
        compiler_params=_params("parallel"),
        name="out_proj",
    )(oa, ob, w_out, w_out, x, post_w.reshape(1, d), pre_w.reshape(1, d))


def _ffn_up_kernel(h_ref, wg_ref, wu_ref, act_ref, wgb_ref, wub_ref):
    @pl.when(pl.program_id(1) == 0)
    def _():
        wgb_ref[...] = wg_ref[...].astype(BF16)
        wub_ref[...] = wu_ref[...].astype(BF16)

    h = h_ref[...]
    act_ref[...] = (_silu(_dot(h, wgb_ref[...])) * _dot(h, wub_ref[...])).astype(act_ref.dtype)


def _ffn_up(hn, w_in_all, layer, tm=1024, tf=512):
    m, d = hn.shape
    tm = min(tm, m)
    nf = D_FF // tf
    return pl.pallas_call(
        _ffn_up_kernel,
        grid=(nf, m // tm),
        in_specs=[pl.BlockSpec((tm, d), lambda j, i: (i, 0)),
                  pl.BlockSpec((None, d, tf), lambda j, i: (layer, 0, j)),
                  pl.BlockSpec((None, d, tf), lambda j, i: (layer, 0, j + nf))],
        out_specs=pl.BlockSpec((tm, tf), lambda j, i: (i, j)),
        out_shape=jax.ShapeDtypeStruct((m, D_FF), BF16),
        scratch_shapes=[pltpu.VMEM((d, tf), BF16), pltpu.VMEM((d, tf), BF16)],
        compiler_params=_params("parallel", "arbitrary"),
        name="ffn_up",
    )(hn, w_in_all, w_in_all)


def _ffn_down_kernel(act_ref, wo_ref, x_ref, post_ref, pre_ref, xo_ref, hn_ref):
    xn = x_ref[...] + _rms(_dot(act_ref[...], wo_ref[...]), post_ref[...])
    xo_ref[...] = xn
    hn_ref[...] = _rms(xn, pre_ref[...]).astype(hn_ref.dtype)


def _ffn_down(act, w_out, x, post_w, pre_w, tm=256):
    m, d = x.shape
    tm = min(tm, m)
    row = lambda i: (i, 0)
    fixed = lambda i: (0, 0)
    return pl.pallas_call(
        _ffn_down_kernel,
        grid=(m // tm,),
        in_specs=[pl.BlockSpec((tm, D_FF), row),
                  pl.BlockSpec((D_FF, d), fixed, pipeline_mode=pl.Buffered(1)),
                  pl.BlockSpec((tm, d), row), pl.BlockSpec((1, d), fixed), pl.BlockSpec((1, d), fixed)],
        out_specs=[pl.BlockSpec((tm, d), row), pl.BlockSpec((tm, d), row)],
        out_shape=[jax.ShapeDtypeStruct((m, d), F32), jax.ShapeDtypeStruct((m, d), BF16)],
        compiler_params=_params("parallel"),
        name="ffn_down",
    )(act, w_out, x, post_w.reshape(1, d), pre_w.reshape(1, d))


def _hgrn_kernel(q_ref, f_ref, i_ref, g_ref, lbraw_ref, nw_ref, s0_ref, o_ref, sout_ref, st_ref,
                 *, layer, sc, n_sub, bb):
    ci = pl.program_id(1)

    @pl.when(ci == 0)
    def _():
        for b in range(bb):
            for h in range(H_A):
                st_ref[b, h] = s0_ref[b, h].T

    raw = lbraw_ref[...]
    e = jnp.exp(raw - jnp.max(raw, axis=0, keepdims=True))
    pr = e / jnp.sum(e, axis=0, keepdims=True)
    lb_all = jnp.zeros((1, D_A), F32)
    for r in range(1, layer + 1):
        lb_all = lb_all + pr[r:r + 1, :]
    nw_all = nw_ref[...]
    tri = _lower_tri(sc, strict=False).astype(F32)
    row = lax.broadcasted_iota(jnp.int32, (sc, LANES), 0)
    sub = 8
    lane8 = lax.broadcasted_iota(jnp.int32, (sub, LANES), 1)

    oml = 1.0 - lb_all
    heads = [slice(h * DK_A, (h + 1) * DK_A) for h in range(H_A)]

    def body(idx, carry):
        b = idx // n_sub
        rows = pl.ds(pl.multiple_of((idx % n_sub) * sc, sc), sc)
        zf = f_ref[b, rows, :]
        iv = i_ref[b, rows, :]
        fg = lb_all + oml * _sigmoid(zf)
        logf = jnp.log(jnp.maximum(fg, TINY))
        kk = oml * _sigmoid(-zf)
        qs = _silu(q_ref[b, rows, :])
        bc = _dot_mask_lhs(tri, logf)
        blast = bc[sc - 1:sc, :]
        qe = (qs * jnp.exp(bc)).astype(BF16)
        kh = (kk * jnp.exp(blast - bc)).astype(BF16)
        ivb = iv.astype(BF16)
        keep = jnp.exp(blast)
        sts = [st_ref[b, h] for h in range(H_A)]
        o_inter = [_dot_nt(qe[:, sl], sts[h].astype(BF16)) for h, sl in enumerate(heads)]
        upd = [_dot_tn(ivb[:, sl], kh[:, sl]) for sl in heads]
        for h, sl in enumerate(heads):
            st_ref[b, h] = sts[h] * keep[:, sl] + upd[h]
        atts = []
        for h, sl in enumerate(heads):
            bc_h, kk_h, qs_h = bc[:, sl], kk[:, sl], qs[:, sl]
            att_t = [jnp.zeros((sub, LANES), F32)] * (sc // sub)
            for t in range(sc):
                n = (t // sub + 1) * sub
                dec = jnp.exp(bc_h[t:t + 1, :] - bc_h[:n])
                pair = jnp.where(row[:n] <= t, kk_h[:n] * dec * qs_h[t:t + 1, :], 0.0)
                a_col = jnp.sum(pair, axis=-1, keepdims=True)
                for k in range(n // sub):
                    att_t[k] = jnp.where(lane8 == t, a_col[k * sub:(k + 1) * sub], att_t[k])
            atts.append(jnp.concatenate(att_t, axis=0)[:, :sc].astype(BF16))
        o_intra = [_dot_tn(atts[h], ivb[:, sl]) for h, sl in enumerate(heads)]
        outs = []
        for h in range(H_A):
            o = o_intra[h] + o_inter[h]
            outs.append(o * lax.rsqrt(jnp.mean(o * o, axis=-1, keepdims=True) + RMS_EPS))
        o = jnp.concatenate(outs, axis=-1) * nw_all
        o_ref[b, rows, :] = (o * _silu(g_ref[b, rows, :])).astype(o_ref.dtype)
        return carry

    lax.fori_loop(0, bb * n_sub, body, 0, unroll=math.gcd(bb * n_sub, 4))

    @pl.when(ci == pl.num_programs(1) - 1)
    def _():
        for b in range(bb):
            for h in range(H_A):
                sout_ref[b, h] = st_ref[b, h].T


def _stacked_state_call(kernel_fn, prev, state_out_index, in_specs, args, **kw):
    if prev is None:
        return pl.pallas_call(kernel_fn, in_specs=in_specs, **kw)(*args)

    def with_prev(prev_ref, *refs):
        del prev_ref
        kernel_fn(*refs)

    return pl.pallas_call(with_prev, in_specs=[pl.BlockSpec(memory_space=pl.ANY)] + in_specs,
                          input_output_aliases={0: state_out_index}, **kw)(prev, *args)


def _hgrn(z_a, lb_raw, norm_w, s_all, prev, layer, bb, tc, sc):
    bsz, t, _ = z_a.shape
    col = lambda k: (lambda b, c: (b, c, k))
    fixed2 = lambda b, c: (0, 0)
    state = pl.BlockSpec((None, bb, H_A, DK_A, DV_A), lambda b, c: (layer, b, 0, 0, 0))
    return _stacked_state_call(
        functools.partial(_hgrn_kernel, layer=layer, sc=sc, n_sub=tc // sc, bb=bb), prev, 1,
        [pl.BlockSpec((bb, tc, D_A), col(0)), pl.BlockSpec((bb, tc, D_A), col(1)),
         pl.BlockSpec((bb, tc, D_A), col(2)), pl.BlockSpec((bb, tc, D_A), col(3)),
         pl.BlockSpec((N_EVEN, D_A), fixed2), pl.BlockSpec((1, D_A), fixed2), state],
        (z_a, z_a, z_a, z_a, lb_raw, norm_w.reshape(1, D_A), s_all),
        grid=(bsz // bb, t // tc),
        out_specs=[pl.BlockSpec((bb, tc, D_A), lambda b, c: (b, c, 0)), state],
        out_shape=[jax.ShapeDtypeStruct((bsz, t, D_A), BF16),
                   jax.ShapeDtypeStruct(s_all.shape, F32)],
        scratch_shapes=[pltpu.VMEM((bb, H_A, DV_A, DK_A), F32)],
        compiler_params=_params("parallel", "arbitrary"),
        name="hgrn2",
    )


def _head_sums(x):
    rows = x.shape[0]
    seg = (lax.broadcasted_iota(jnp.int32, (LANES, LANES), 0) // N_B
           == lax.broadcasted_iota(jnp.int32, (LANES, LANES), 1) // N_B).astype(F32)
    n_tiles = D_B // LANES
    xs = jnp.concatenate([x[:, i * LANES:(i + 1) * LANES] for i in range(n_tiles)], axis=0)
    s = _dot_mask_rhs(xs, seg, pieces=2)
    return jnp.concatenate([s[i * rows:(i + 1) * rows] for i in range(n_tiles)], axis=1)


def _rwkv_kernel(z_ref, sh0_ref, s0_ref, mu_ref, w0_ref, w2_ref, a0_ref, a2_ref, g2_ref, kk_ref, ka_ref,
                 rk_ref, lnw_ref, lnb_ref, o_ref, sout_ref, shout_ref, s_ref, prev_ref, *, c, nb):
    ci = pl.program_id(1)
    rows = nb * c
    dz = D_RWKV_PROJ

    @pl.when(ci == 0)
    def _():
        s_ref[...] = s0_ref[...]
        prev_ref[...] = sh0_ref[...]

    blocks = [slice(b * c, (b + 1) * c) for b in range(nb)]
    last = [slice((b + 1) * c - 1, (b + 1) * c) for b in range(nb)]

    def per_block(rows_of):
        n = rows_of[0].shape[-1]
        blk = lax.broadcasted_iota(jnp.int32, (rows, n), 0) // c
        out = jnp.broadcast_to(rows_of[0], (rows, n))
        for b in range(1, nb):
            out = jnp.where(blk == b, rows_of[b], out)
        return out

    zb = z_ref[...].reshape(rows, dz)
    row = lax.broadcasted_iota(jnp.int32, (rows, dz), 0)
    prev = jnp.where(row % c == 0, per_block([prev_ref[b] for b in range(nb)]), pltpu.roll(zb, 1, axis=0))
    for b in range(nb):
        prev_ref[b] = zb[last[b], :]
    zs = zb + (prev - zb) * mu_ref[...]
    r = zs[:, 0:D_B]
    kb = zs[:, D_B:2 * D_B]
    v = zs[:, 2 * D_B:3 * D_B]
    o0 = 3 * D_B
    wl = zs[:, o0:o0 + R_W]
    al = zs[:, o0 + R_W:o0 + R_W + R_A]
    gl = zs[:, o0 + R_W + R_A:o0 + R_W + R_A + R_G]
    wlin = w0_ref[...] + _dot(jnp.tanh(wl).astype(BF16), w2_ref[...])
    w_log = -math.exp(-0.5) * _sigmoid(wlin)
    a = _sigmoid(a0_ref[...] + _dot(al.astype(BF16), a2_ref[...]))
    gate = _dot(_sigmoid(gl).astype(BF16), g2_ref[...])
    kkr = kb * kk_ref[...]
    kmod = kb * (1.0 + (a - 1.0) * ka_ref[...])

    ri = lax.broadcasted_iota(jnp.int32, (rows, rows), 0)
    cj = lax.broadcasted_iota(jnp.int32, (rows, rows), 1)
    tri = jnp.where(ri // c == cj // c, (ri >= cj).astype(F32), 0.0)
    bc = _dot_mask_lhs(tri, w_log)
    b_last = [bc[l, :] for l in last]
    blast = per_block(b_last)
    gam = jnp.exp(bc)
    inv_gam = jnp.exp(-bc)
    gam_prev = jnp.exp(bc - w_log)
    to_end = jnp.exp(blast - bc)
    gam_last = [jnp.exp(x) for x in b_last]
    kkn = kkr / jnp.maximum(jnp.sqrt(_head_sums(kkr * kkr)), 1e-12)
    beta = kkn * a
    xa = -kkn * gam_prev
    xr = r * gam
    yb = beta * inv_gam
    yk = kmod * inv_gam
    eb = beta * to_end
    ek = kmod * to_end
    lhs = [jnp.concatenate([xa[bl], xr[bl]], axis=0).astype(BF16) for bl in blocks]
    rhs = [jnp.concatenate([yb[bl], yk[bl]], axis=0).astype(BF16) for bl in blocks]
    end = [jnp.concatenate([eb[bl], ek[bl]], axis=0).astype(BF16) for bl in blocks]
    vb = [v[bl].astype(BF16) for bl in blocks]

    ri2 = lax.broadcasted_iota(jnp.int32, (2 * c, 2 * c), 0)
    cj2 = lax.broadcasted_iota(jnp.int32, (2 * c, 2 * c), 1)
    keep = (ri2 % c + ri2 // c) > (cj2 % c)
    n_steps = max(1, int(math.log2(c)))
    heads = [slice(h * N_B, (h + 1) * N_B) for h in range(H_B)]
    pairs = [(b, h) for b in range(nb) for h in range(H_B)]
    idx = range(len(pairs))
    s0 = [s_ref[b, h] for b, h in pairs]
    res = [_dot_nt(lhs[b][:, heads[h]], jnp.concatenate([rhs[b][:, heads[h]], s0[i].astype(BF16)], axis=0))
           for i, (b, h) in enumerate(pairs)]
    att = [jnp.where(keep, res[i][:, :2 * c], 0.0) for i in idx]
    from_s0 = [res[i][:, 2 * c:] for i in idx]
    u = [from_s0[i][:c] + _dot(att[i][:c, c:].astype(BF16), vb[b][:, heads[h]]) for i, (b, h) in enumerate(pairs)]
    pb = [att[i][:c, :c].astype(BF16) for i in idx]
    for it in range(n_steps - 1):
        both = [_dot(pb[i], jnp.concatenate([u[i].astype(BF16), pb[i]], axis=1)) for i in idx]
        u = [u[i] + both[i][:, :N_B] for i in idx]
        pb = [both[i][:, N_B:].astype(BF16) for i in idx]
    u = [u[i] + _dot(pb[i], u[i].astype(BF16)) for i in idx]
    uv = [jnp.concatenate([u[i].astype(BF16), vb[b][:, heads[h]]], axis=0) for i, (b, h) in enumerate(pairs)]
    ys = [from_s0[i][c:] + _dot(att[i][c:].astype(BF16), uv[i]) for i in idx]
    s_new = [_dot_tn(uv[i], end[b][:, heads[h]]) for i, (b, h) in enumerate(pairs)]
    for i, (b, h) in enumerate(pairs):
        s_ref[b, h] = s0[i] * gam_last[b][:, heads[h]] + s_new[i]

    y = jnp.concatenate([jnp.concatenate(ys[b * H_B:(b + 1) * H_B], axis=-1) for b in range(nb)], axis=0)
    yc = y - _head_sums(y) * (1.0 / N_B)
    var = _head_sums(yc * yc) * (1.0 / N_B)
    yn = yc * lax.rsqrt(var + GN_EPS) * lnw_ref[...] + lnb_ref[...]
    out = (yn + _head_sums(r * kmod * rk_ref[...]) * v) * gate
    o_ref[...] = out.reshape(nb, c, D_B).astype(o_ref.dtype)

    @pl.when(ci == pl.num_programs(1) - 1)
    def _():
        sout_ref[...] = s_ref[...]
        for b in range(nb):
            shout_ref[b] = zb[last[b], :]


def _rwkv(z_b, s_all, prev, shift0, p, j, c, nb):
    bsz, t, dz = z_b.shape
    vec = lambda x: x.reshape(1, -1)
    fixed2 = lambda b, ci: (0, 0)
    vspec = lambda n: pl.BlockSpec((1, n), fixed2)
    state = pl.BlockSpec((None, nb, H_B, N_B, N_B), lambda b, ci: (j, b, 0, 0, 0))
    shift = pl.BlockSpec((nb, 1, dz), lambda b, ci: (b, 0, 0))
    return _stacked_state_call(
        functools.partial(_rwkv_kernel, c=c, nb=nb), prev, 1,
        [pl.BlockSpec((nb, c, dz), lambda b, ci: (b, ci, 0)), shift, state,
         vspec(dz), vspec(D_B),
         pl.BlockSpec((R_W, D_B), fixed2), vspec(D_B),
         pl.BlockSpec((R_A, D_B), fixed2), pl.BlockSpec((R_G, D_B), fixed2),
         vspec(D_B), vspec(D_B), vspec(D_B), vspec(D_B), vspec(D_B)],
        (z_b, shift0.reshape(bsz, 1, dz), s_all, vec(p['rwkv_mu'][j]), vec(p['rwkv_w0'][j]),
         p['rwkv_w2'][j].astype(BF16), vec(p['rwkv_a0'][j]), p['rwkv_a2'][j].astype(BF16),
         p['rwkv_g2'][j].astype(BF16), vec(p['rwkv_k_k'][j]), vec(p['rwkv_k_a'][j]),
         vec(p['rwkv_r_k'][j]), vec(p['rwkv_ln_w'][j]), vec(p['rwkv_ln_b'][j])),
        grid=(bsz // nb, t // c),
        out_specs=[pl.BlockSpec((nb, c, D_B), lambda b, ci: (b, ci, 0)), state, shift],
        out_shape=[jax.ShapeDtypeStruct((bsz, t, D_B), BF16),
                   jax.ShapeDtypeStruct(s_all.shape, F32),
                   jax.ShapeDtypeStruct((bsz, 1, dz), F32)],
        scratch_shapes=[pltpu.VMEM((nb, H_B, N_B, N_B), F32), pltpu.VMEM((nb, 1, dz), F32)],
        compiler_params=_params("parallel", "arbitrary"),
        name="rwkv7",
    )


def _rope_kernel(ang_ref, sin_ref, cos_ref, *, tt, pos0):
    row = lax.broadcasted_iota(jnp.int32, (tt, DK_C), 0)
    ang = (pos0 + pl.program_id(0) * tt + row).astype(F32) * ang_ref[...]
    sin_ref[...] = jnp.sin(ang)
    cos_ref[...] = jnp.cos(ang)


def _rope_table(pos0, t):
    angle = 1.0 / (10000.0 ** jnp.linspace(0.0, 1.0, DK_C // 2, dtype=F32))
    angle = jnp.repeat(angle, 2).reshape(1, DK_C)
    tt = min(t, 256)
    out = pl.BlockSpec((tt, DK_C), lambda i: (i, 0))
    return pl.pallas_call(
        functools.partial(_rope_kernel, tt=tt, pos0=pos0),
        grid=(t // tt,),
        in_specs=[pl.BlockSpec((1, DK_C), lambda i: (0, 0))],
        out_specs=[out, out],
        out_shape=[jax.ShapeDtypeStruct((t, DK_C), F32)] * 2,
        compiler_params=_params("parallel"),
        name="rope_table",
    )(angle)


def _ret_kernel(q_ref, k_ref, v_ref, g_ref, sin_ref, cos_ref, s0_ref, o_ref, sout_ref, s_ref, *, c, nb):
    ci = pl.program_id(1)

    @pl.when(ci == 0)
    def _():
        s_ref[...] = s0_ref[...]

    sin = jnp.concatenate([sin_ref[...]] * H_C, axis=1)
    cos = jnp.concatenate([cos_ref[...]] * H_C, axis=1)
    even = (lax.broadcasted_iota(jnp.int32, (c, D_C), 1) % 2) == 0

    def rope(x):
        rot = jnp.where(even, -pltpu.roll(x, D_C - 1, axis=1), pltpu.roll(x, 1, axis=1))
        return x * cos + rot * sin

    ti = lax.broadcasted_iota(jnp.int32, (c, c), 0)
    si = lax.broadcasted_iota(jnp.int32, (c, c), 1)
    causal = ti >= si
    dist = jnp.where(causal, (ti - si).astype(F32), 0.0)
    tcol = lax.broadcasted_iota(jnp.int32, (c, 1), 0).astype(F32)
    heads = [slice(h * DK_C, (h + 1) * DK_C) for h in range(H_C)]
    log_gamma = [math.log1p(-2.0 ** (-5.0 - h)) for h in range(H_C)]
    dec = [jnp.where(causal, jnp.exp(dist * lg), 0.0) for lg in log_gamma]
    from_start = [jnp.exp((tcol + 1.0) * lg) for lg in log_gamma]
    to_end = [jnp.exp((float(c) - 1.0 - tcol) * lg) for lg in log_gamma]
    keep = [math.exp(float(c) * lg) for lg in log_gamma]

    pairs = [(b, h) for b in range(nb) for h in range(H_C)]
    qs = [rope(q_ref[b]) for b in range(nb)]
    ks = [rope(k_ref[b]) * (DK_C ** -0.5) for b in range(nb)]
    vb = [v_ref[b].astype(BF16) for b in range(nb)]
    qb = [q.astype(BF16) for q in qs]
    kb = [k.astype(BF16) for k in ks]
    s_old = [s_ref[b, h] for b, h in pairs]
    att = [_dot_nt(qb[b][:, heads[h]], kb[b][:, heads[h]]) for b, h in pairs]
    o_st = [_dot((qs[b][:, heads[h]] * from_start[h]).astype(BF16), s_old[i].astype(BF16))
            for i, (b, h) in enumerate(pairs)]
    o_in = [_dot((att[i] * dec[h]).astype(BF16), vb[b][:, heads[h]]) for i, (b, h) in enumerate(pairs)]
    upd = [_dot_tn((ks[b][:, heads[h]] * to_end[h]).astype(BF16), vb[b][:, heads[h]]) for b, h in pairs]
    s_new = [keep[h] * s_old[i] + upd[i] for i, (b, h) in enumerate(pairs)]
    for i, (b, h) in enumerate(pairs):
        s_ref[b, h] = s_new[i]
    for b in range(nb):
        outs = []
        for h in range(H_C):
            o = o_in[b * H_C + h] + o_st[b * H_C + h]
            outs.append(o * lax.rsqrt(jnp.mean(o * o, axis=-1, keepdims=True) + RMS_EPS))
        o_ref[b] = (jnp.concatenate(outs, axis=-1) * _silu(g_ref[b])).astype(o_ref.dtype)

    @pl.when(ci == pl.num_programs(1) - 1)
    def _():
        for i, (b, h) in enumerate(pairs):
            sout_ref[b, h] = s_new[i]


def _retention(z, s_all, prev, j, sin, cos, c, nb):
    bsz, t, _ = z.shape
    col = lambda k: (lambda b, ci: (b, ci, k))
    state = pl.BlockSpec((None, nb, H_C, DK_C, DV_C), lambda b, ci: (j, b, 0, 0, 0))
    table = pl.BlockSpec((c, DK_C), lambda b, ci: (ci, 0))
    return _stacked_state_call(
        functools.partial(_ret_kernel, c=c, nb=nb), prev, 1,
        [pl.BlockSpec((nb, c, D_C), col(0)), pl.BlockSpec((nb, c, D_C), col(1)),
         pl.BlockSpec((nb, c, D_C), col(2)), pl.BlockSpec((nb, c, D_C), col(3)),
         table, table, state],
        (z, z, z, z, sin, cos, s_all),
        grid=(bsz // nb, t // c),
        out_specs=[pl.BlockSpec((nb, c, D_C), lambda b, ci: (b, ci, 0)), state],
        out_shape=[jax.ShapeDtypeStruct((bsz, t, D_C), BF16), jax.ShapeDtypeStruct(s_all.shape, F32)],
        scratch_shapes=[pltpu.VMEM((nb, H_C, DK_C, DV_C), F32)],
        compiler_params=_params("parallel", "arbitrary"),
        name="retention",
    )


def _rglru_kernel(y_ref, x_ref, conv0_ref, h0_ref, cw_ref, cb_ref, wa_ref, wx_ref, ba_ref, bx_ref, lam_ref,
                  o_ref, convout_ref, hout_ref, tail_ref, h_ref, *, tc, nb, pos0):
    ci = pl.program_id(1)
    rows = nb * tc
    sub = 8

    @pl.when(ci == 0)
    def _():
        tail_ref[...] = conv0_ref[...]
        h_ref[...] = h0_ref[...]

    x = x_ref[...].reshape(rows, D_D)
    row = lax.broadcasted_iota(jnp.int32, (rows, D_D), 0)
    t_in = row % tc
    tails = [tail_ref[b] for b in range(nb)]

    def shifted(d):
        r = pltpu.roll(x, d, axis=0)
        for m in range(d):
            k = CONV_W - 1 - d + m
            for b in range(nb):
                r = jnp.where(row == b * tc + m, tails[b][k:k + 1, :], r)
        return r

    cw = cw_ref[...]
    xc = cb_ref[...] + cw[CONV_W - 1:CONV_W, :] * x
    for d in range(1, CONV_W):
        xc = xc + cw[CONV_W - 1 - d:CONV_W - d, :] * shifted(d)
    new_tails = [x[(b + 1) * tc - (CONV_W - 1):(b + 1) * tc, :] for b in range(nb)]
    for b in range(nb):
        tail_ref[b] = new_tails[b]

    ra, rx = [], []
    for hd in range(H_D):
        xh = xc[:, hd * BW_D:(hd + 1) * BW_D].astype(BF16)
        ra.append(_dot(xh, wa_ref[hd]))
        rx.append(_dot(xh, wx_ref[hd]))
    rg = _sigmoid(jnp.concatenate(ra, axis=-1) + ba_ref[...])
    ig = _sigmoid(jnp.concatenate(rx, axis=-1) + bx_ref[...])
    nlam = -lam_ref[...]
    softplus = jnp.maximum(nlam, 0.0) + jnp.log1p(jnp.exp(-jnp.abs(nlam)))
    log_a = -LRU_C * rg * softplus
    a = jnp.exp(log_a)
    mult = jnp.sqrt(-jnp.tanh(log_a) * (a * a + 1.0))
    mult = jnp.where(pos0 + ci * tc + t_in == 0, 1.0, mult)
    bt = mult * ig * xc

    in_sub = row % sub
    d = 1
    while d < sub:
        m = in_sub >= d
        bt = jnp.where(m, a * pltpu.roll(bt, d, axis=0) + bt, bt)
        a = jnp.where(m, a * pltpu.roll(a, d, axis=0), a)
        d *= 2
    hs = []
    for b in range(nb):
        carry = h_ref[b]
        for k in range(tc // sub):
            sl = slice(b * tc + k * sub, b * tc + (k + 1) * sub)
            h = bt[sl] + a[sl] * carry
            hs.append(h)
            carry = h[sub - 1:sub, :]
        h_ref[b] = carry
    hseq = jnp.concatenate(hs, axis=0)
    y = y_ref[...].reshape(rows, D_D)
    gelu = 0.5 * y * (1.0 + jnp.tanh(math.sqrt(2.0 / math.pi) * (y + 0.044715 * (y * y * y))))
    o_ref[...] = (hseq * gelu).reshape(nb, tc, D_D).astype(o_ref.dtype)

    @pl.when(ci == pl.num_programs(1) - 1)
    def _():
        for b in range(nb):
            convout_ref[b] = new_tails[b]
        hout_ref[...] = h_ref[...]


def _rglru(z, conv0, h0, p, j, pos0, tc, nb):
    bsz, t, _ = z.shape
    vec = lambda x: x.reshape(1, -1)
    fixed2 = lambda b, ci: (0, 0)
    fixed3 = lambda b, ci: (0, 0, 0)
    vspec = pl.BlockSpec((1, D_D), fixed2)
    ycol = 4 * D_C // D_D
    conv = pl.BlockSpec((nb, CONV_W - 1, D_D), lambda b, ci: (b, 0, 0))
    hid = pl.BlockSpec((nb, 1, D_D), lambda b, ci: (b, 0, 0))
    return pl.pallas_call(
        functools.partial(_rglru_kernel, tc=tc, nb=nb, pos0=pos0),
        grid=(bsz // nb, t // tc),
        in_specs=[pl.BlockSpec((nb, tc, D_D), lambda b, ci: (b, ci, ycol)),
                  pl.BlockSpec((nb, tc, D_D), lambda b, ci: (b, ci, ycol + 1)),
                  conv, hid,
                  pl.BlockSpec((CONV_W, D_D), fixed2), vspec,
                  pl.BlockSpec((H_D, BW_D, BW_D), fixed3), pl.BlockSpec((H_D, BW_D, BW_D), fixed3),
                  vspec, vspec, vspec],
        out_specs=[pl.BlockSpec((nb, tc, D_D), lambda b, ci: (b, ci, 0)), conv, hid],
        out_shape=[jax.ShapeDtypeStruct((bsz, t, D_D), BF16),
                   jax.ShapeDtypeStruct((bsz, CONV_W - 1, D_D), F32),
                   jax.ShapeDtypeStruct((bsz, 1, D_D), F32)],
        scratch_shapes=[pltpu.VMEM((nb, CONV_W - 1, D_D), F32), pltpu.VMEM((nb, 1, D_D), F32)],
        compiler_params=_params("parallel", "arbitrary"),
        name="rglru",
    )(z, z, conv0, h0.reshape(bsz, 1, D_D), p['conv_w'][j], vec(p['conv_b'][j]),
      p['rglru_wa'][j].astype(BF16), p['rglru_wx'][j].astype(BF16),
      vec(p['rglru_ba'][j]), vec(p['rglru_bx'][j]), vec(p['rglru_lambda'][j]))


def _tiles(bsz, t):
    if t >= 256:
        return dict(hgrn=(1, 256, 16), rwkv=(64, math.gcd(bsz, 2)), ret=(256, 1), lru=(256, 1))
    nb = math.gcd(bsz, 4)
    return dict(hgrn=(nb, t, min(t, 8)), rwkv=(t, nb), ret=(t, nb), lru=(t, nb))


def _run_trunk(x, pos0, st_hgrn, st_rwkv, st_shift, st_ret, st_conv, st_h, p, wb):
    bsz, t, d = x.shape
    m = bsz * t
    tl = _tiles(bsz, t)
    x2 = x.reshape(m, d)
    hn = _rmsnorm_bf16(x2, p['norm_mix_pre'][0], tm=min(512, m))
    n_hgrn, n_rwkv, n_ret = None, None, None
    n_shift, n_conv, n_h = [], [], []
    sin, cos = _rope_table(pos0, t)
    for l in range(DEPTH):
        j = l // 2
        if l % 2 == 0:
            z_a = _in_proj(hn, p['w_in_even'], j, 4 * D_A, tn=1024, tm=1024).reshape(bsz, t, 4 * D_A)
            z_b = _in_proj(hn, wb['w_in_rwkv'], j, D_RWKV_PROJ, tn=D_RWKV_PROJ // 2, tm=512)
            z_b = z_b.reshape(bsz, t, D_RWKV_PROJ)
            bb, tc, sc = tl['hgrn']
            o_a, n_hgrn = _hgrn(z_a, p['hgrn_lb_raw'], p['hgrn_norm_w'][j], st_hgrn, n_hgrn, j, bb, tc, sc)
            o_b, n_rwkv, sh = _rwkv(z_b, st_rwkv, n_rwkv, st_shift[j], p, j, *tl['rwkv'])
            n_shift.append(sh.reshape(bsz, D_RWKV_PROJ))
            w_out = wb['w_out_even'][j]
        else:
            z = _in_proj(hn, p['w_in_odd'], j, 4 * D_C + 2 * D_D, tn=1024, tm=1024)
            z = z.reshape(bsz, t, 4 * D_C + 2 * D_D)
            o_a, n_ret = _retention(z, st_ret, n_ret, j, sin, cos, *tl['ret'])
            o_b, scv, shh = _rglru(z, st_conv[j], st_h[j], p, j, pos0, *tl['lru'])
            n_conv.append(scv)
            n_h.append(shh.reshape(bsz, D_D))
            w_out = wb['w_out_odd'][j]
        x2, hn = _outproj(o_a.reshape(m, -1), o_b.reshape(m, -1), w_out, x2,
                          p['norm_mix_post'][l], p['norm_ffn_pre'][l], tm=min(512, m))
        next_pre = p['norm_mix_pre'][(l + 1) % DEPTH]
        act = _ffn_up(hn, p['w_ffn_in'], l)
        x2, hn = _ffn_down(act, wb['w_ffn_out'][l], x2, p['norm_ffn_post'][l], next_pre)
    return x2.reshape(bsz, t, d), (n_hgrn, n_rwkv, jnp.stack(n_shift),
                                   n_ret, jnp.stack(n_conv), jnp.stack(n_h))


def _zero_state(s, batch):
    return jnp.zeros((s.shape[0], batch) + s.shape[2:], s.dtype)


def kernel(x_prompt, x_sample, state_hgrn, state_rwkv, state_rwkv_shift, state_ret, state_rglru_conv, state_rglru_h, norm_mix_pre, norm_mix_post, norm_ffn_pre, norm_ffn_post, w_in_even, w_out_even, hgrn_lb_raw, hgrn_norm_w, rwkv_mu, rwkv_w0, rwkv_w2, rwkv_a0, rwkv_a2, rwkv_g2, rwkv_k_k, rwkv_k_a, rwkv_r_k, rwkv_ln_w, rwkv_ln_b, w_in_odd, w_out_odd, conv_w, conv_b, rglru_wa, rglru_ba, rglru_wx, rglru_bx, rglru_lambda, w_ffn_in, w_ffn_out):
    p = dict(norm_mix_pre=norm_mix_pre, norm_mix_post=norm_mix_post, norm_ffn_pre=norm_ffn_pre,
             norm_ffn_post=norm_ffn_post, hgrn_lb_raw=hgrn_lb_raw, hgrn_norm_w=hgrn_norm_w,
             rwkv_mu=rwkv_mu, rwkv_w0=rwkv_w0, rwkv_w2=rwkv_w2, rwkv_a0=rwkv_a0, rwkv_a2=rwkv_a2,
             rwkv_g2=rwkv_g2, rwkv_k_k=rwkv_k_k, rwkv_k_a=rwkv_k_a, rwkv_r_k=rwkv_r_k,
             rwkv_ln_w=rwkv_ln_w, rwkv_ln_b=rwkv_ln_b, conv_w=conv_w, conv_b=conv_b,
             rglru_wa=rglru_wa, rglru_ba=rglru_ba, rglru_wx=rglru_wx, rglru_bx=rglru_bx,
             rglru_lambda=rglru_lambda, w_in_even=w_in_even, w_in_odd=w_in_odd, w_ffn_in=w_ffn_in)
    wb = dict(w_in_rwkv=w_in_even[:, :, 4 * D_A:], w_out_even=w_out_even.astype(BF16),
              w_out_odd=w_out_odd.astype(BF16), w_ffn_out=w_ffn_out.astype(BF16))
    bp = x_prompt.shape[0]
    y_prompt, st_p = _run_trunk(x_prompt, 0,
                                _zero_state(state_hgrn, bp), _zero_state(state_rwkv, bp),
                                _zero_state(state_rwkv_shift, bp), _zero_state(state_ret, bp),
                                _zero_state(state_rglru_conv, bp), _zero_state(state_rglru_h, bp), p, wb)
    y_sample, st_s = _run_trunk(x_sample, PAST_LEN, state_hgrn, state_rwkv, state_rwkv_shift,
                                state_ret, state_rglru_conv, state_rglru_h, p, wb)
    hgrn_p, rwkv_p, shift_p, ret_p, conv_p, h_p = st_p
    hgrn_s, rwkv_s, shift_s, ret_s, conv_s, h_s = st_s
    return (y_prompt, y_sample, hgrn_p, hgrn_s, rwkv_p, rwkv_s, shift_p, shift_s,
            ret_p, ret_s, conv_p, conv_s, h_p, h_s)
```

```python
import functools
import math

import jax
import jax.numpy as jnp
from jax import lax
from jax.experimental import pallas as pl
from jax.experimental.pallas import tpu as pltpu

F32 = jnp.float32
BF16 = jnp.bfloat16

D_MODEL = 2048
DEPTH = 4
N_EVEN = (DEPTH + 1) // 2
N_ODD = DEPTH // 2
D_A = D_MODEL // 2
H_A = 8
DK_A = 128
DV_A = 128
D_B = D_MODEL // 2
N_B = 64
H_B = D_B // N_B
R_W = 64
R_A = 64
R_G = 128
D_RWKV_PROJ = 3 * D_B + R_W + R_A + R_G
D_C = D_MODEL // 2
H_C = 4
DK_C = D_C // H_C
DV_C = D_C // H_C
D_D = D_MODEL // 2
H_D = 4
BW_D = D_D // H_D
CONV_W = 4
LRU_C = 8.0
D_FF = ((8 * D_MODEL + 3 * 256 - 1) // (3 * 256)) * 256
RMS_EPS = 1e-6
GN_EPS = 64e-5
TINY = 1e-30
PAST_LEN = 16384
LOG2_E = 1.4426950408889634

VMEM_LIMIT_BYTES = 56 * 1024 * 1024
LANES = 128


def _params(*sem):
    return pltpu.CompilerParams(dimension_semantics=sem, vmem_limit_bytes=VMEM_LIMIT_BYTES)


def _rms(x, w):
    return x * lax.rsqrt(jnp.mean(x * x, axis=-1, keepdims=True) + RMS_EPS) * w


def _sigmoid(x):
    return jax.nn.sigmoid(x)


def _silu(x):
    return x * jax.nn.sigmoid(x)


def _dot(a, b):
    return jnp.dot(a, b, preferred_element_type=F32)


def _dot_nt(a, b):
    return lax.dot_general(a, b, (((1,), (1,)), ((), ())), preferred_element_type=F32)


def _dot_tn(a, b):
    return lax.dot_general(a, b, (((0,), (0,)), ((), ())), preferred_element_type=F32)


def _split(x, pieces):
    out = []
    for _ in range(pieces - 1):
        part = x.astype(BF16)
        out.append(part)
        x = x - part.astype(F32)
    out.append(x.astype(BF16))
    return out


def _dot_mask_lhs(mask, x):
    m = mask.astype(BF16)
    return sum(_dot(m, part) for part in _split(x, 3))


def _dot_mask_rhs(x, mask, pieces):
    m = mask.astype(BF16)
    return sum(_dot(part, m) for part in _split(x, pieces))


def _row_parts(n, size):
    size = min(size, n)
    return [slice(i, i + size) for i in range(0, n, size)]


def _lower_tri(n, strict):
    r = lax.broadcasted_iota(jnp.int32, (n, n), 0)
    c = lax.broadcasted_iota(jnp.int32, (n, n), 1)
    return (r > c) if strict else (r >= c)


def _rmsnorm_kernel(x_ref, w_ref, o_ref):
    o_ref[...] = _rms(x_ref[...], w_ref[...]).astype(o_ref.dtype)


def _rmsnorm_bf16(x, w, tm=512):
    m, d = x.shape
    return pl.pallas_call(
        _rmsnorm_kernel,
        grid=(m // tm,),
        in_specs=[pl.BlockSpec((tm, d), lambda i: (i, 0)), pl.BlockSpec((1, d), lambda i: (0, 0))],
        out_specs=pl.BlockSpec((tm, d), lambda i: (i, 0)),
        out_shape=jax.ShapeDtypeStruct((m, d), BF16),
        compiler_params=_params("parallel"),
        name="rmsnorm",
    )(x, w.reshape(1, d))


def _in_proj_kernel(a_ref, w_ref, o_ref, wb_ref):
    @pl.when(pl.program_id(1) == 0)
    def _():
        wb_ref[...] = w_ref[...].astype(BF16)

    o_ref[...] = _dot(a_ref[...], wb_ref[...])


def _in_proj(a, w_all, layer, n, tn, tm):
    m, k = a.shape
    tm = min(tm, m)
    return pl.pallas_call(
        _in_proj_kernel,
        grid=(n // tn, m // tm),
        in_specs=[pl.BlockSpec((tm, k), lambda j, i: (i, 0)),
                  pl.BlockSpec((None, k, tn), lambda j, i: (layer, 0, j))],
        out_specs=pl.BlockSpec((tm, tn), lambda j, i: (i, j)),
        out_shape=jax.ShapeDtypeStruct((m, n), F32),
        scratch_shapes=[pltpu.VMEM((k, tn), BF16)],
        compiler_params=_params("parallel", "arbitrary"),
        name="in_proj",
    )(a, w_all)


def _outproj_kernel(oa_ref, ob_ref, wa_ref, wb_ref, x_ref, post_ref, pre_ref, xo_ref, hn_ref):
    for rows in _row_parts(x_ref.shape[0], 256):
        mix = _dot(oa_ref[rows, :], wa_ref[...]) + _dot(ob_ref[rows, :], wb_ref[...])
        xn = x_ref[rows, :] + _rms(mix, post_ref[...])
        xo_ref[rows, :] = xn
        hn_ref[rows, :] = _rms(xn, pre_ref[...]).astype(hn_ref.dtype)


def _outproj(oa, ob, w_out, x, post_w, pre_w, tm=512):
    m, d = x.shape
    ka, kb = oa.shape[1], ob.shape[1]
    row = lambda i: (i, 0)
    fixed = lambda i: (0, 0)
    return pl.pallas_call(
        _outproj_kernel,
        grid=(m // tm,),
        in_specs=[pl.BlockSpec((tm, ka), row), pl.BlockSpec((tm, kb), row),
                  pl.BlockSpec((ka, d), fixed), pl.BlockSpec((kb, d), lambda i: (1, 0)),
                  pl.BlockSpec((tm, d), row), pl.BlockSpec((1, d), fixed), pl.BlockSpec((1, d), fixed)],
        out_specs=[pl.BlockSpec((tm, d), row), pl.BlockSpec((tm, d), row)],
        out_shape=[jax.ShapeDtypeStruct((m, d), F32), jax.ShapeDtypeStruct((m, d), BF16)],
        compiler_params=_params("parallel"),
        name="out_proj",
    )(oa, ob, w_out, w_out, x, post_w.reshape(1, d), pre_w.reshape(1, d))


def _ffn_up_kernel(h_ref, wg_ref, wu_ref, act_ref, wgb_ref, wub_ref):
    @pl.when(pl.program_id(1) == 0)
    def _():
        wgb_ref[...] = wg_ref[...].astype(BF16)
        wub_ref[...] = wu_ref[...].astype(BF16)

    h = h_ref[...]
    act_ref[...] = (_silu(_dot(h, wgb_ref[...])) * _dot(h, wub_ref[...])).astype(act_ref.dtype)


def _ffn_up(hn, w_in_all, layer, tm=2048, tf=512):
    m, d = hn.shape
    tm = min(tm, m)
    nf = D_FF // tf
    return pl.pallas_call(
        _ffn_up_kernel,
        grid=(nf, m // tm),
        in_specs=[pl.BlockSpec((tm, d), lambda j, i: (i, 0)),
                  pl.BlockSpec((None, d, tf), lambda j, i: (layer, 0, j)),
                  pl.BlockSpec((None, d, tf), lambda j, i: (layer, 0, j + nf))],
        out_specs=pl.BlockSpec((tm, tf), lambda j, i: (i, j)),
        out_shape=jax.ShapeDtypeStruct((m, D_FF), BF16),
        scratch_shapes=[pltpu.VMEM((d, tf), BF16), pltpu.VMEM((d, tf), BF16)],
        compiler_params=_params("parallel", "arbitrary"),
        name="ffn_up",
    )(hn, w_in_all, w_in_all)


def _ffn_down_kernel(act_ref, wo_ref, x_ref, post_ref, pre_ref, xo_ref, hn_ref):
    for rows in _row_parts(x_ref.shape[0], 256):
        xn = x_ref[rows, :] + _rms(_dot(act_ref[rows, :], wo_ref[...]), post_ref[...])
        xo_ref[rows, :] = xn
        hn_ref[rows, :] = _rms(xn, pre_ref[...]).astype(hn_ref.dtype)


def _ffn_down(act, w_out, x, post_w, pre_w, tm=512):
    m, d = x.shape
    tm = min(tm, m)
    row = lambda i: (i, 0)
    fixed = lambda i: (0, 0)
    return pl.pallas_call(
        _ffn_down_kernel,
        grid=(m // tm,),
        in_specs=[pl.BlockSpec((tm, D_FF), row),
                  pl.BlockSpec((D_FF, d), fixed, pipeline_mode=pl.Buffered(1)),
                  pl.BlockSpec((tm, d), row, pipeline_mode=pl.Buffered(1)),
                  pl.BlockSpec((1, d), fixed), pl.BlockSpec((1, d), fixed)],
        out_specs=[pl.BlockSpec((tm, d), row), pl.BlockSpec((tm, d), row)],
        out_shape=[jax.ShapeDtypeStruct((m, d), F32), jax.ShapeDtypeStruct((m, d), BF16)],
        compiler_params=_params("parallel"),
        name="ffn_down",
    )(act, w_out, x, post_w.reshape(1, d), pre_w.reshape(1, d))


def _hgrn_kernel(q_ref, f_ref, i_ref, g_ref, lbraw_ref, nw_ref, s0_ref, o_ref, sout_ref, st_ref,
                 *, layer, sc, n_sub, bb):
    ci = pl.program_id(1)

    @pl.when(ci == 0)
    def _():
        for b in range(bb):
            for h in range(H_A):
                st_ref[b, h] = s0_ref[b, h].T

    raw = lbraw_ref[...]
    e = jnp.exp(raw - jnp.max(raw, axis=0, keepdims=True))
    pr = e / jnp.sum(e, axis=0, keepdims=True)
    lb_all = jnp.zeros((1, D_A), F32)
    for r in range(1, layer + 1):
        lb_all = lb_all + pr[r:r + 1, :]
    nw_all = nw_ref[...]
    tri = _lower_tri(sc, strict=False).astype(F32)
    row = lax.broadcasted_iota(jnp.int32, (sc, LANES), 0)
    sub = 8
    lane8 = lax.broadcasted_iota(jnp.int32, (sub, LANES), 1)

    oml = 1.0 - lb_all
    heads = [slice(h * DK_A, (h + 1) * DK_A) for h in range(H_A)]

    def body(idx, carry):
        b = idx % bb
        rows = pl.ds(pl.multiple_of((idx // bb) * sc, sc), sc)
        zf = f_ref[b, rows, :]
        iv = i_ref[b, rows, :]
        fg = lb_all + oml * _sigmoid(zf)
        logf = jnp.log(jnp.maximum(fg, TINY))
        kk = oml * _sigmoid(-zf)
        qs = _silu(q_ref[b, rows, :])
        bc = _dot_mask_lhs(tri, logf)
        blast = bc[sc - 1:sc, :]
        qe = (qs * jnp.exp(bc)).astype(BF16)
        kh = (kk * jnp.exp(blast - bc)).astype(BF16)
        ivb = iv.astype(BF16)
        keep = jnp.exp(blast)
        sts = [st_ref[b, h] for h in range(H_A)]
        o_inter = [_dot_nt(qe[:, sl], sts[h].astype(BF16)) for h, sl in enumerate(heads)]
        upd = [_dot_tn(ivb[:, sl], kh[:, sl]) for sl in heads]
        for h, sl in enumerate(heads):
            st_ref[b, h] = sts[h] * keep[:, sl] + upd[h]
        bc2 = bc * LOG2_E
        lk2 = jnp.log(kk) * LOG2_E - bc2
        atts = []
        for h, sl in enumerate(heads):
            bc_h, lk_h, qs_h = bc2[:, sl], lk2[:, sl], qs[:, sl]
            att_t = [jnp.zeros((sub, LANES), F32)] * (sc // sub)
            for t in range(sc):
                n = (t // sub + 1) * sub
                dec_k = jnp.exp2(bc_h[t:t + 1, :] + lk_h[:n])
                pair = jnp.where(row[:n] <= t, dec_k * qs_h[t:t + 1, :], 0.0)
                a_col = jnp.sum(pair, axis=-1, keepdims=True)
                for k in range(n // sub):
                    att_t[k] = jnp.where(lane8 == t, a_col[k * sub:(k + 1) * sub], att_t[k])
            atts.append(jnp.concatenate(att_t, axis=0)[:, :sc].astype(BF16))
        o_intra = [_dot_tn(atts[h], ivb[:, sl]) for h, sl in enumerate(heads)]
        outs = []
        for h in range(H_A):
            o = o_intra[h] + o_inter[h]
            outs.append(o * lax.rsqrt(jnp.mean(o * o, axis=-1, keepdims=True) + RMS_EPS))
        o = jnp.concatenate(outs, axis=-1) * nw_all
        o_ref[b, rows, :] = (o * _silu(g_ref[b, rows, :])).astype(o_ref.dtype)
        return carry

    lax.fori_loop(0, bb * n_sub, body, 0, unroll=bb)

    @pl.when(ci == pl.num_programs(1) - 1)
    def _():
        for b in range(bb):
            for h in range(H_A):
                sout_ref[b, h] = st_ref[b, h].T


def _stacked_state_call(kernel_fn, prev, state_out_index, in_specs, args, **kw):
    if prev is None:
        return pl.pallas_call(kernel_fn, in_specs=in_specs, **kw)(*args)

    def with_prev(prev_ref, *refs):
        del prev_ref
        kernel_fn(*refs)

    return pl.pallas_call(with_prev, in_specs=[pl.BlockSpec(memory_space=pl.ANY)] + in_specs,
                          input_output_aliases={0: state_out_index}, **kw)(prev, *args)


def _hgrn(z_a, lb_raw, norm_w, s_all, prev, layer, bb, tc, sc):
    bsz, t, _ = z_a.shape
    col = lambda k: (lambda b, c: (b, c, k))
    fixed2 = lambda b, c: (0, 0)
    state = pl.BlockSpec((None, bb, H_A, DK_A, DV_A), lambda b, c: (layer, b, 0, 0, 0))
    return _stacked_state_call(
        functools.partial(_hgrn_kernel, layer=layer, sc=sc, n_sub=tc // sc, bb=bb), prev, 1,
        [pl.BlockSpec((bb, tc, D_A), col(0)), pl.BlockSpec((bb, tc, D_A), col(1)),
         pl.BlockSpec((bb, tc, D_A), col(2)), pl.BlockSpec((bb, tc, D_A), col(3)),
         pl.BlockSpec((N_EVEN, D_A), fixed2), pl.BlockSpec((1, D_A), fixed2), state],
        (z_a, z_a, z_a, z_a, lb_raw, norm_w.reshape(1, D_A), s_all),
        grid=(bsz // bb, t // tc),
        out_specs=[pl.BlockSpec((bb, tc, D_A), lambda b, c: (b, c, 0)), state],
        out_shape=[jax.ShapeDtypeStruct((bsz, t, D_A), BF16),
                   jax.ShapeDtypeStruct(s_all.shape, F32)],
        scratch_shapes=[pltpu.VMEM((bb, H_A, DV_A, DK_A), F32)],
        compiler_params=_params("parallel", "arbitrary"),
        name="hgrn2",
    )


def _head_sums(x):
    rows = x.shape[0]
    seg = (lax.broadcasted_iota(jnp.int32, (LANES, LANES), 0) // N_B
           == lax.broadcasted_iota(jnp.int32, (LANES, LANES), 1) // N_B).astype(F32)
    n_tiles = D_B // LANES
    xs = jnp.concatenate([x[:, i * LANES:(i + 1) * LANES] for i in range(n_tiles)], axis=0)
    s = _dot_mask_rhs(xs, seg, pieces=2)
    return jnp.concatenate([s[i * rows:(i + 1) * rows] for i in range(n_tiles)], axis=1)


def _rwkv_kernel(z_ref, sh0_ref, s0_ref, mu_ref, w0_ref, w2_ref, a0_ref, a2_ref, g2_ref, kk_ref, ka_ref,
                 rk_ref, lnw_ref, lnb_ref, o_ref, sout_ref, shout_ref, s_ref, prev_ref, *, c, nb):
    ci = pl.program_id(1)
    rows = nb * c
    dz = D_RWKV_PROJ

    @pl.when(ci == 0)
    def _():
        s_ref[...] = s0_ref[...]
        prev_ref[...] = sh0_ref[...]

    blocks = [slice(b * c, (b + 1) * c) for b in range(nb)]
    last = [slice((b + 1) * c - 1, (b + 1) * c) for b in range(nb)]

    def per_block(rows_of):
        n = rows_of[0].shape[-1]
        blk = lax.broadcasted_iota(jnp.int32, (rows, n), 0) // c
        out = jnp.broadcast_to(rows_of[0], (rows, n))
        for b in range(1, nb):
            out = jnp.where(blk == b, rows_of[b], out)
        return out

    zb = z_ref[...].reshape(rows, dz)
    row = lax.broadcasted_iota(jnp.int32, (rows, dz), 0)
    prev = jnp.where(row % c == 0, per_block([prev_ref[b] for b in range(nb)]), pltpu.roll(zb, 1, axis=0))
    for b in range(nb):
        prev_ref[b] = zb[last[b], :]
    zs = zb + (prev - zb) * mu_ref[...]
    r = zs[:, 0:D_B]
    kb = zs[:, D_B:2 * D_B]
    v = zs[:, 2 * D_B:3 * D_B]
    o0 = 3 * D_B
    wl = zs[:, o0:o0 + R_W]
    al = zs[:, o0 + R_W:o0 + R_W + R_A]
    gl = zs[:, o0 + R_W + R_A:o0 + R_W + R_A + R_G]
    wlin = w0_ref[...] + _dot(jnp.tanh(wl).astype(BF16), w2_ref[...])
    w_log = -math.exp(-0.5) * _sigmoid(wlin)
    a = _sigmoid(a0_ref[...] + _dot(al.astype(BF16), a2_ref[...]))
    gate = _dot(_sigmoid(gl).astype(BF16), g2_ref[...])
    kkr = kb * kk_ref[...]
    kmod = kb * (1.0 + (a - 1.0) * ka_ref[...])

    ri = lax.broadcasted_iota(jnp.int32, (rows, rows), 0)
    cj = lax.broadcasted_iota(jnp.int32, (rows, rows), 1)
    tri = jnp.where(ri // c == cj // c, (ri >= cj).astype(F32), 0.0)
    bc = _dot_mask_lhs(tri, w_log)
    b_last = [bc[l, :] for l in last]
    blast = per_block(b_last)
    gam = jnp.exp(bc)
    inv_gam = jnp.exp(-bc)
    gam_prev = jnp.exp(bc - w_log)
    to_end = jnp.exp(blast - bc)
    gam_last = [jnp.exp(x) for x in b_last]
    kkn = kkr / jnp.maximum(jnp.sqrt(_head_sums(kkr * kkr)), 1e-12)
    beta = kkn * a
    xa = -kkn * gam_prev
    xr = r * gam
    yb = beta * inv_gam
    yk = kmod * inv_gam
    eb = beta * to_end
    ek = kmod * to_end
    lhs = [jnp.concatenate([xa[bl], xr[bl]], axis=0).astype(BF16) for bl in blocks]
    rhs = [jnp.concatenate([yb[bl], yk[bl]], axis=0).astype(BF16) for bl in blocks]
    end = [jnp.concatenate([eb[bl], ek[bl]], axis=0).astype(BF16) for bl in blocks]
    vb = [v[bl].astype(BF16) for bl in blocks]

    ri2 = lax.broadcasted_iota(jnp.int32, (2 * c, 2 * c), 0)
    cj2 = lax.broadcasted_iota(jnp.int32, (2 * c, 2 * c), 1)
    keep = (ri2 % c + ri2 // c) > (cj2 % c)
    n_steps = max(1, int(math.log2(c)))
    heads = [slice(h * N_B, (h + 1) * N_B) for h in range(H_B)]
    pairs = [(b, h) for b in range(nb) for h in range(H_B)]
    idx = range(len(pairs))
    s0 = [s_ref[b, h] for b, h in pairs]
    res = [_dot_nt(lhs[b][:, heads[h]], jnp.concatenate([rhs[b][:, heads[h]], s0[i].astype(BF16)], axis=0))
           for i, (b, h) in enumerate(pairs)]
    att = [jnp.where(keep, res[i][:, :2 * c], 0.0) for i in idx]
    from_s0 = [res[i][:, 2 * c:] for i in idx]
    u = [from_s0[i][:c] + _dot(att[i][:c, c:].astype(BF16), vb[b][:, heads[h]]) for i, (b, h) in enumerate(pairs)]
    pb = [att[i][:c, :c].astype(BF16) for i in idx]
    for it in range(n_steps - 1):
        both = [_dot(pb[i], jnp.concatenate([u[i].astype(BF16), pb[i]], axis=1)) for i in idx]
        u = [u[i] + both[i][:, :N_B] for i in idx]
        pb = [both[i][:, N_B:].astype(BF16) for i in idx]
    u = [u[i] + _dot(pb[i], u[i].astype(BF16)) for i in idx]
    uv = [jnp.concatenate([u[i].astype(BF16), vb[b][:, heads[h]]], axis=0) for i, (b, h) in enumerate(pairs)]
    ys = [from_s0[i][c:] + _dot(att[i][c:].astype(BF16), uv[i]) for i in idx]
    s_new = [_dot_tn(uv[i], end[b][:, heads[h]]) for i, (b, h) in enumerate(pairs)]
    for i, (b, h) in enumerate(pairs):
        s_ref[b, h] = s0[i] * gam_last[b][:, heads[h]] + s_new[i]

    y = jnp.concatenate([jnp.concatenate(ys[b * H_B:(b + 1) * H_B], axis=-1) for b in range(nb)], axis=0)
    yc = y - _head_sums(y) * (1.0 / N_B)
    var = _head_sums(yc * yc) * (1.0 / N_B)
    yn = yc * lax.rsqrt(var + GN_EPS) * lnw_ref[...] + lnb_ref[...]
    out = (yn + _head_sums(r * kmod * rk_ref[...]) * v) * gate
    o_ref[...] = out.reshape(nb, c, D_B).astype(o_ref.dtype)

    @pl.when(ci == pl.num_programs(1) - 1)
    def _():
        sout_ref[...] = s_ref[...]
        for b in range(nb):
            shout_ref[b] = zb[last[b], :]


def _rwkv(z_b, s_all, prev, shift0, p, j, c, nb):
    bsz, t, dz = z_b.shape
    vec = lambda x: x.reshape(1, -1)
    fixed2 = lambda b, ci: (0, 0)
    vspec = lambda n: pl.BlockSpec((1, n), fixed2)
    state = pl.BlockSpec((None, nb, H_B, N_B, N_B), lambda b, ci: (j, b, 0, 0, 0))
    shift = pl.BlockSpec((nb, 1, dz), lambda b, ci: (b, 0, 0))
    return _stacked_state_call(
        functools.partial(_rwkv_kernel, c=c, nb=nb), prev, 1,
        [pl.BlockSpec((nb, c, dz), lambda b, ci: (b, ci, 0)), shift, state,
         vspec(dz), vspec(D_B),
         pl.BlockSpec((R_W, D_B), fixed2), vspec(D_B),
         pl.BlockSpec((R_A, D_B), fixed2), pl.BlockSpec((R_G, D_B), fixed2),
         vspec(D_B), vspec(D_B), vspec(D_B), vspec(D_B), vspec(D_B)],
        (z_b, shift0.reshape(bsz, 1, dz), s_all, vec(p['rwkv_mu'][j]), vec(p['rwkv_w0'][j]),
         p['rwkv_w2'][j].astype(BF16), vec(p['rwkv_a0'][j]), p['rwkv_a2'][j].astype(BF16),
         p['rwkv_g2'][j].astype(BF16), vec(p['rwkv_k_k'][j]), vec(p['rwkv_k_a'][j]),
         vec(p['rwkv_r_k'][j]), vec(p['rwkv_ln_w'][j]), vec(p['rwkv_ln_b'][j])),
        grid=(bsz // nb, t // c),
        out_specs=[pl.BlockSpec((nb, c, D_B), lambda b, ci: (b, ci, 0)), state, shift],
        out_shape=[jax.ShapeDtypeStruct((bsz, t, D_B), BF16),
                   jax.ShapeDtypeStruct(s_all.shape, F32),
                   jax.ShapeDtypeStruct((bsz, 1, dz), F32)],
        scratch_shapes=[pltpu.VMEM((nb, H_B, N_B, N_B), F32), pltpu.VMEM((nb, 1, dz), F32)],
        compiler_params=_params("parallel", "arbitrary"),
        name="rwkv7",
    )


def _rope_kernel(ang_ref, sin_ref, cos_ref, *, tt, pos0):
    row = lax.broadcasted_iota(jnp.int32, (tt, DK_C), 0)
    ang = (pos0 + pl.program_id(0) * tt + row).astype(F32) * ang_ref[...]
    sin_ref[...] = jnp.sin(ang)
    cos_ref[...] = jnp.cos(ang)


def _rope_table(pos0, t):
    angle = 1.0 / (10000.0 ** jnp.linspace(0.0, 1.0, DK_C // 2, dtype=F32))
    angle = jnp.repeat(angle, 2).reshape(1, DK_C)
    tt = min(t, 256)
    out = pl.BlockSpec((tt, DK_C), lambda i: (i, 0))
    return pl.pallas_call(
        functools.partial(_rope_kernel, tt=tt, pos0=pos0),
        grid=(t // tt,),
        in_specs=[pl.BlockSpec((1, DK_C), lambda i: (0, 0))],
        out_specs=[out, out],
        out_shape=[jax.ShapeDtypeStruct((t, DK_C), F32)] * 2,
        compiler_params=_params("parallel"),
        name="rope_table",
    )(angle)


def _ret_kernel(q_ref, k_ref, v_ref, g_ref, sin_ref, cos_ref, s0_ref, o_ref, sout_ref, s_ref, *, c, nb):
    ci = pl.program_id(1)

    @pl.when(ci == 0)
    def _():
        s_ref[...] = s0_ref[...]

    sin = jnp.concatenate([sin_ref[...]] * H_C, axis=1)
    cos = jnp.concatenate([cos_ref[...]] * H_C, axis=1)
    even = (lax.broadcasted_iota(jnp.int32, (c, D_C), 1) % 2) == 0

    def rope(x):
        rot = jnp.where(even, -pltpu.roll(x, D_C - 1, axis=1), pltpu.roll(x, 1, axis=1))
        return x * cos + rot * sin

    ti = lax.broadcasted_iota(jnp.int32, (c, c), 0)
    si = lax.broadcasted_iota(jnp.int32, (c, c), 1)
    causal = ti >= si
    dist = jnp.where(causal, (ti - si).astype(F32), 0.0)
    tcol = lax.broadcasted_iota(jnp.int32, (c, 1), 0).astype(F32)
    heads = [slice(h * DK_C, (h + 1) * DK_C) for h in range(H_C)]
    log_gamma = [math.log1p(-2.0 ** (-5.0 - h)) for h in range(H_C)]
    dec = [jnp.where(causal, jnp.exp(dist * lg), 0.0) for lg in log_gamma]
    from_start = [jnp.exp((tcol + 1.0) * lg) for lg in log_gamma]
    to_end = [jnp.exp((float(c) - 1.0 - tcol) * lg) for lg in log_gamma]
    keep = [math.exp(float(c) * lg) for lg in log_gamma]

    pairs = [(b, h) for b in range(nb) for h in range(H_C)]
    qs = [rope(q_ref[b]) for b in range(nb)]
    ks = [rope(k_ref[b]) * (DK_C ** -0.5) for b in range(nb)]
    vb = [v_ref[b].astype(BF16) for b in range(nb)]
    qb = [q.astype(BF16) for q in qs]
    kb = [k.astype(BF16) for k in ks]
    s_old = [s_ref[b, h] for b, h in pairs]
    att = [_dot_nt(qb[b][:, heads[h]], kb[b][:, heads[h]]) for b, h in pairs]
    o_st = [_dot((qs[b][:, heads[h]] * from_start[h]).astype(BF16), s_old[i].astype(BF16))
            for i, (b, h) in enumerate(pairs)]
    o_in = [_dot((att[i] * dec[h]).astype(BF16), vb[b][:, heads[h]]) for i, (b, h) in enumerate(pairs)]
    upd = [_dot_tn((ks[b][:, heads[h]] * to_end[h]).astype(BF16), vb[b][:, heads[h]]) for b, h in pairs]
    s_new = [keep[h] * s_old[i] + upd[i] for i, (b, h) in enumerate(pairs)]
    for i, (b, h) in enumerate(pairs):
        s_ref[b, h] = s_new[i]
    for b in range(nb):
        outs = []
        for h in range(H_C):
            o = o_in[b * H_C + h] + o_st[b * H_C + h]
            outs.append(o * lax.rsqrt(jnp.mean(o * o, axis=-1, keepdims=True) + RMS_EPS))
        o_ref[b] = (jnp.concatenate(outs, axis=-1) * _silu(g_ref[b])).astype(o_ref.dtype)

    @pl.when(ci == pl.num_programs(1) - 1)
    def _():
        for i, (b, h) in enumerate(pairs):
            sout_ref[b, h] = s_new[i]


def _retention(z, s_all, prev, j, sin, cos, c, nb):
    bsz, t, _ = z.shape
    col = lambda k: (lambda b, ci: (b, ci, k))
    state = pl.BlockSpec((None, nb, H_C, DK_C, DV_C), lambda b, ci: (j, b, 0, 0, 0))
    table = pl.BlockSpec((c, DK_C), lambda b, ci: (ci, 0))
    return _stacked_state_call(
        functools.partial(_ret_kernel, c=c, nb=nb), prev, 1,
        [pl.BlockSpec((nb, c, D_C), col(0)), pl.BlockSpec((nb, c, D_C), col(1)),
         pl.BlockSpec((nb, c, D_C), col(2)), pl.BlockSpec((nb, c, D_C), col(3)),
         table, table, state],
        (z, z, z, z, sin, cos, s_all),
        grid=(bsz // nb, t // c),
        out_specs=[pl.BlockSpec((nb, c, D_C), lambda b, ci: (b, ci, 0)), state],
        out_shape=[jax.ShapeDtypeStruct((bsz, t, D_C), BF16), jax.ShapeDtypeStruct(s_all.shape, F32)],
        scratch_shapes=[pltpu.VMEM((nb, H_C, DK_C, DV_C), F32)],
        compiler_params=_params("parallel", "arbitrary"),
        name="retention",
    )


def _rglru_kernel(y_ref, x_ref, conv0_ref, h0_ref, cw_ref, cb_ref, wa_ref, wx_ref, ba_ref, bx_ref, lam_ref,
                  o_ref, convout_ref, hout_ref, tail_ref, h_ref, *, tc, nb, pos0):
    ci = pl.program_id(1)
    rows = nb * tc
    sub = 8

    @pl.when(ci == 0)
    def _():
        tail_ref[...] = conv0_ref[...]
        h_ref[...] = h0_ref[...]

    x = x_ref[...].reshape(rows, D_D)
    row = lax.broadcasted_iota(jnp.int32, (rows, D_D), 0)
    t_in = row % tc
    tails = [tail_ref[b] for b in range(nb)]

    def shifted(d):
        r = pltpu.roll(x, d, axis=0)
        for m in range(d):
            k = CONV_W - 1 - d + m
            for b in range(nb):
                r = jnp.where(row == b * tc + m, tails[b][k:k + 1, :], r)
        return r

    cw = cw_ref[...]
    xc = cb_ref[...] + cw[CONV_W - 1:CONV_W, :] * x
    for d in range(1, CONV_W):
        xc = xc + cw[CONV_W - 1 - d:CONV_W - d, :] * shifted(d)
    new_tails = [x[(b + 1) * tc - (CONV_W - 1):(b + 1) * tc, :] for b in range(nb)]
    for b in range(nb):
        tail_ref[b] = new_tails[b]

    ra, rx = [], []
    for hd in range(H_D):
        xh = xc[:, hd * BW_D:(hd + 1) * BW_D].astype(BF16)
        ra.append(_dot(xh, wa_ref[hd]))
        rx.append(_dot(xh, wx_ref[hd]))
    rg = _sigmoid(jnp.concatenate(ra, axis=-1) + ba_ref[...])
    ig = _sigmoid(jnp.concatenate(rx, axis=-1) + bx_ref[...])
    nlam = -lam_ref[...]
    softplus = jnp.maximum(nlam, 0.0) + jnp.log1p(jnp.exp(-jnp.abs(nlam)))
    log_a = -LRU_C * rg * softplus
    a = jnp.exp(log_a)
    mult = jnp.sqrt(-jnp.tanh(log_a) * (a * a + 1.0))
    mult = jnp.where(pos0 + ci * tc + t_in == 0, 1.0, mult)
    bt = mult * ig * xc

    in_sub = row % sub
    d = 1
    while d < sub:
        m = in_sub >= d
        bt = jnp.where(m, a * pltpu.roll(bt, d, axis=0) + bt, bt)
        a = jnp.where(m, a * pltpu.roll(a, d, axis=0), a)
        d *= 2
    hs = []
    for b in range(nb):
        carry = h_ref[b]
        for k in range(tc // sub):
            sl = slice(b * tc + k * sub, b * tc + (k + 1) * sub)
            h = bt[sl] + a[sl] * carry
            hs.append(h)
            carry = h[sub - 1:sub, :]
        h_ref[b] = carry
    hseq = jnp.concatenate(hs, axis=0)
    y = y_ref[...].reshape(rows, D_D)
    gelu = 0.5 * y * (1.0 + jnp.tanh(math.sqrt(2.0 / math.pi) * (y + 0.044715 * (y * y * y))))
    o_ref[...] = (hseq * gelu).reshape(nb, tc, D_D).astype(o_ref.dtype)

    @pl.when(ci == pl.num_programs(1) - 1)
    def _():
        for b in range(nb):
            convout_ref[b] = new_tails[b]
        hout_ref[...] = h_ref[...]


def _rglru(z, conv0, h0, p, j, pos0, tc, nb):
    bsz, t, _ = z.shape
    vec = lambda x: x.reshape(1, -1)
    fixed2 = lambda b, ci: (0, 0)
    fixed3 = lambda b, ci: (0, 0, 0)
    vspec = pl.BlockSpec((1, D_D), fixed2)
    ycol = 4 * D_C // D_D
    conv = pl.BlockSpec((nb, CONV_W - 1, D_D), lambda b, ci: (b, 0, 0))
    hid = pl.BlockSpec((nb, 1, D_D), lambda b, ci: (b, 0, 0))
    return pl.pallas_call(
        functools.partial(_rglru_kernel, tc=tc, nb=nb, pos0=pos0),
        grid=(bsz // nb, t // tc),
        in_specs=[pl.BlockSpec((nb, tc, D_D), lambda b, ci: (b, ci, ycol)),
                  pl.BlockSpec((nb, tc, D_D), lambda b, ci: (b, ci, ycol + 1)),
                  conv, hid,
                  pl.BlockSpec((CONV_W, D_D), fixed2), vspec,
                  pl.BlockSpec((H_D, BW_D, BW_D), fixed3), pl.BlockSpec((H_D, BW_D, BW_D), fixed3),
                  vspec, vspec, vspec],
        out_specs=[pl.BlockSpec((nb, tc, D_D), lambda b, ci: (b, ci, 0)), conv, hid],
        out_shape=[jax.ShapeDtypeStruct((bsz, t, D_D), BF16),
                   jax.ShapeDtypeStruct((bsz, CONV_W - 1, D_D), F32),
                   jax.ShapeDtypeStruct((bsz, 1, D_D), F32)],
        scratch_shapes=[pltpu.VMEM((nb, CONV_W - 1, D_D), F32), pltpu.VMEM((nb, 1, D_D), F32)],
        compiler_params=_params("parallel", "arbitrary"),
        name="rglru",
    )(z, z, conv0, h0.reshape(bsz, 1, D_D), p['conv_w'][j], vec(p['conv_b'][j]),
      p['rglru_wa'][j].astype(BF16), p['rglru_wx'][j].astype(BF16),
      vec(p['rglru_ba'][j]), vec(p['rglru_bx'][j]), vec(p['rglru_lambda'][j]))


def _tiles(bsz, t):
    if t >= 256:
        return dict(hgrn=(math.gcd(bsz, 4), 128, 16), rwkv=(64, math.gcd(bsz, 2)), ret=(256, 1), lru=(256, 1))
    nb = math.gcd(bsz, 4)
    return dict(hgrn=(nb, t, min(t, 8)), rwkv=(t, nb), ret=(t, nb), lru=(t, nb))


def _run_trunk(x, pos0, st_hgrn, st_rwkv, st_shift, st_ret, st_conv, st_h, p, wb):
    bsz, t, d = x.shape
    m = bsz * t
    tl = _tiles(bsz, t)
    x2 = x.reshape(m, d)
    hn = _rmsnorm_bf16(x2, p['norm_mix_pre'][0], tm=min(512, m))
    n_hgrn, n_rwkv, n_ret = None, None, None
    n_shift, n_conv, n_h = [], [], []
    sin, cos = _rope_table(pos0, t)
    for l in range(DEPTH):
        j = l // 2
        if l % 2 == 0:
            z_a = _in_proj(hn, p['w_in_even'], j, 4 * D_A, tn=1024, tm=1024).reshape(bsz, t, 4 * D_A)
            z_b = _in_proj(hn, wb['w_in_rwkv'], j, D_RWKV_PROJ, tn=D_RWKV_PROJ // 2, tm=512)
            z_b = z_b.reshape(bsz, t, D_RWKV_PROJ)
            bb, tc, sc = tl['hgrn']
            o_a, n_hgrn = _hgrn(z_a, p['hgrn_lb_raw'], p['hgrn_norm_w'][j], st_hgrn, n_hgrn, j, bb, tc, sc)
            o_b, n_rwkv, sh = _rwkv(z_b, st_rwkv, n_rwkv, st_shift[j], p, j, *tl['rwkv'])
            n_shift.append(sh.reshape(bsz, D_RWKV_PROJ))
            w_out = wb['w_out_even'][j]
        else:
            z = _in_proj(hn, p['w_in_odd'], j, 4 * D_C + 2 * D_D, tn=1024, tm=1024)
            z = z.reshape(bsz, t, 4 * D_C + 2 * D_D)
            o_a, n_ret = _retention(z, st_ret, n_ret, j, sin, cos, *tl['ret'])
            o_b, scv, shh = _rglru(z, st_conv[j], st_h[j], p, j, pos0, *tl['lru'])
            n_conv.append(scv)
            n_h.append(shh.reshape(bsz, D_D))
            w_out = wb['w_out_odd'][j]
        x2, hn = _outproj(o_a.reshape(m, -1), o_b.reshape(m, -1), w_out, x2,
                          p['norm_mix_post'][l], p['norm_ffn_pre'][l], tm=min(512, m))
        next_pre = p['norm_mix_pre'][(l + 1) % DEPTH]
        act = _ffn_up(hn, p['w_ffn_in'], l)
        x2, hn = _ffn_down(act, wb['w_ffn_out'][l], x2, p['norm_ffn_post'][l], next_pre)
    return x2.reshape(bsz, t, d), (n_hgrn, n_rwkv, jnp.stack(n_shift),
                                   n_ret, jnp.stack(n_conv), jnp.stack(n_h))


def _zero_state(s, batch):
    return jnp.zeros((s.shape[0], batch) + s.shape[2:], s.dtype)


def kernel(x_prompt, x_sample, state_hgrn, state_rwkv, state_rwkv_shift, state_ret, state_rglru_conv, state_rglru_h, norm_mix_pre, norm_mix_post, norm_ffn_pre, norm_ffn_post, w_in_even, w_out_even, hgrn_lb_raw, hgrn_norm_w, rwkv_mu, rwkv_w0, rwkv_w2, rwkv_a0, rwkv_a2, rwkv_g2, rwkv_k_k, rwkv_k_a, rwkv_r_k, rwkv_ln_w, rwkv_ln_b, w_in_odd, w_out_odd, conv_w, conv_b, rglru_wa, rglru_ba, rglru_wx, rglru_bx, rglru_lambda, w_ffn_in, w_ffn_out):
    p = dict(norm_mix_pre=norm_mix_pre, norm_mix_post=norm_mix_post, norm_ffn_pre=norm_ffn_pre,
             norm_ffn_post=norm_ffn_post, hgrn_lb_raw=hgrn_lb_raw, hgrn_norm_w=hgrn_norm_w,
             rwkv_mu=rwkv_mu, rwkv_w0=rwkv_w0, rwkv_w2=rwkv_w2, rwkv_a0=rwkv_a0, rwkv_a2=rwkv_a2,
             rwkv_g2=rwkv_g2, rwkv_k_k=rwkv_k_k, rwkv_k_a=rwkv_k_a, rwkv_r_k=rwkv_r_k,
             rwkv_ln_w=rwkv_ln_w, rwkv_ln_b=rwkv_ln_b, conv_w=conv_w, conv_b=conv_b,
             rglru_wa=rglru_wa, rglru_ba=rglru_ba, rglru_wx=rglru_wx, rglru_bx=rglru_bx,
             rglru_lambda=rglru_lambda, w_in_even=w_in_even, w_in_odd=w_in_odd, w_ffn_in=w_ffn_in)
    wb = dict(w_in_rwkv=w_in_even[:, :, 4 * D_A:], w_out_even=w_out_even.astype(BF16),
              w_out_odd=w_out_odd.astype(BF16), w_ffn_out=w_ffn_out.astype(BF16))
    bp = x_prompt.shape[0]
    y_prompt, st_p = _run_trunk(x_prompt, 0,
                                _zero_state(state_hgrn, bp), _zero_state(state_rwkv, bp),
                                _zero_state(state_rwkv_shift, bp), _zero_state(state_ret, bp),
                                _zero_state(state_rglru_conv, bp), _zero_state(state_rglru_h, bp), p, wb)
    y_sample, st_s = _run_trunk(x_sample, PAST_LEN, state_hgrn, state_rwkv, state_rwkv_shift,
                                state_ret, state_rglru_conv, state_rglru_h, p, wb)
    hgrn_p, rwkv_p, shift_p, ret_p, conv_p, h_p = st_p
    hgrn_s, rwkv_s, shift_s, ret_s, conv_s, h_s = st_s
    return (y_prompt, y_sample, hgrn_p, hgrn_s, rwkv_p, rwkv_s, shift_p, shift_s,
            ret_p, ret_s, conv_p, conv_s, h_p, h_s)
```

```python
import functools
import math

import jax
import jax.numpy as jnp
from jax import lax
from jax.experimental import pallas as pl
from jax.experimental.pallas import tpu as pltpu

F32 = jnp.float32
BF16 = jnp.bfloat16

D_MODEL = 2048
DEPTH = 4
N_EVEN = (DEPTH + 1) // 2
N_ODD = DEPTH // 2
D_A = D_MODEL // 2
H_A = 8
DK_A = 128
DV_A = 128
D_B = D_MODEL // 2
N_B = 64
H_B = D_B // N_B
R_W = 64
R_A = 64
R_G = 128
D_RWKV_PROJ = 3 * D_B + R_W + R_A + R_G
D_C = D_MODEL // 2
H_C = 4
DK_C = D_C // H_C
DV_C = D_C // H_C
D_D = D_MODEL // 2
H_D = 4
BW_D = D_D // H_D
CONV_W = 4
LRU_C = 8.0
D_FF = ((8 * D_MODEL + 3 * 256 - 1) // (3 * 256)) * 256
RMS_EPS = 1e-6
GN_EPS = 64e-5
TINY = 1e-30
PAST_LEN = 16384

VMEM_LIMIT_BYTES = 56 * 1024 * 1024
LANES = 128


def _params(*sem):
    return pltpu.CompilerParams(dimension_semantics=sem, vmem_limit_bytes=VMEM_LIMIT_BYTES)


def _rms(x, w):
    return x * lax.rsqrt(jnp.mean(x * x, axis=-1, keepdims=True) + RMS_EPS) * w


def _sigmoid(x):
    return jax.nn.sigmoid(x)


def _silu(x):
    return x * jax.nn.sigmoid(x)


def _dot(a, b):
    return jnp.dot(a, b, preferred_element_type=F32)


def _dot_nt(a, b):
    return lax.dot_general(a, b, (((1,), (1,)), ((), ())), preferred_element_type=F32)


def _dot_tn(a, b):
    return lax.dot_general(a, b, (((0,), (0,)), ((), ())), preferred_element_type=F32)


def _split(x, pieces):
    out = []
    for _ in range(pieces - 1):
        part = x.astype(BF16)
        out.append(part)
        x = x - part.astype(F32)
    out.append(x.astype(BF16))
    return out


def _dot_mask_lhs(mask, x):
    m = mask.astype(BF16)
    return sum(_dot(m, part) for part in _split(x, 3))


def _dot_mask_rhs(x, mask, pieces):
    m = mask.astype(BF16)
    return sum(_dot(part, m) for part in _split(x, pieces))


def _row_parts(n, size):
    size = min(size, n)
    return [slice(i, i + size) for i in range(0, n, size)]


def _lower_tri(n, strict):
    r = lax.broadcasted_iota(jnp.int32, (n, n), 0)
    c = lax.broadcasted_iota(jnp.int32, (n, n), 1)
    return (r > c) if strict else (r >= c)


def _rmsnorm_kernel(x_ref, w_ref, o_ref):
    o_ref[...] = _rms(x_ref[...], w_ref[...]).astype(o_ref.dtype)


def _rmsnorm_bf16(x, w, tm=512):
    m, d = x.shape
    return pl.pallas_call(
        _rmsnorm_kernel,
        grid=(m // tm,),
        in_specs=[pl.BlockSpec((tm, d), lambda i: (i, 0)), pl.BlockSpec((1, d), lambda i: (0, 0))],
        out_specs=pl.BlockSpec((tm, d), lambda i: (i, 0)),
        out_shape=jax.ShapeDtypeStruct((m, d), BF16),
        compiler_params=_params("parallel"),
        name="rmsnorm",
    )(x, w.reshape(1, d))


def _in_proj_kernel(a_ref, w_ref, o_ref, wb_ref):
    @pl.when(pl.program_id(1) == 0)
    def _():
        wb_ref[...] = w_ref[...].astype(BF16)

    o_ref[...] = _dot(a_ref[...], wb_ref[...])


def _in_proj(a, w_all, layer, col0, n, tn, tm):
    m, k = a.shape
    tm = min(tm, m)
    return pl.pallas_call(
        _in_proj_kernel,
        grid=(n // tn, m // tm),
        in_specs=[pl.BlockSpec((tm, k), lambda j, i: (i, 0)),
                  pl.BlockSpec((None, k, tn), lambda j, i: (layer, 0, col0 + j))],
        out_specs=[pl.BlockSpec((tm, tn), lambda j, i: (i, j)), pl.BlockSpec((k, tn), lambda j, i: (0, j))],
        out_shape=[jax.ShapeDtypeStruct((m, n), F32), jax.ShapeDtypeStruct((k, n), BF16)],
        compiler_params=_params("parallel", "arbitrary"),
        name="in_proj",
    )(a, w_all)


def _matmul_kernel(a_ref, w_ref, o_ref):
    o_ref[...] = _dot(a_ref[...], w_ref[...])


def _in_proj_rounded(a, wb, tn, tm):
    m, k = a.shape
    n = wb.shape[1]
    tm = min(tm, m)
    return pl.pallas_call(
        _matmul_kernel,
        grid=(n // tn, m // tm),
        in_specs=[pl.BlockSpec((tm, k), lambda j, i: (i, 0)), pl.BlockSpec((k, tn), lambda j, i: (0, j))],
        out_specs=pl.BlockSpec((tm, tn), lambda j, i: (i, j)),
        out_shape=jax.ShapeDtypeStruct((m, n), F32),
        compiler_params=_params("parallel", "parallel"),
        name="in_proj_rounded",
    )(a, wb)


def _outproj_kernel(oa_ref, ob_ref, wa_ref, wb_ref, x_ref, post_ref, pre_ref, xo_ref, hn_ref):
    for rows in _row_parts(x_ref.shape[0], 256):
        mix = _dot(oa_ref[rows, :], wa_ref[...]) + _dot(ob_ref[rows, :], wb_ref[...])
        xn = x_ref[rows, :] + _rms(mix, post_ref[...])
        xo_ref[rows, :] = xn
        hn_ref[rows, :] = _rms(xn, pre_ref[...]).astype(hn_ref.dtype)


def _outproj(oa, ob, w_out, x, post_w, pre_w, tm=512):
    m, d = x.shape
    ka, kb = oa.shape[1], ob.shape[1]
    row = lambda i: (i, 0)
    fixed = lambda i: (0, 0)
    return pl.pallas_call(
        _outproj_kernel,
        grid=(m // tm,),
        in_specs=[pl.BlockSpec((tm, ka), row), pl.BlockSpec((tm, kb), row),
                  pl.BlockSpec((ka, d), fixed), pl.BlockSpec((kb, d), lambda i: (1, 0)),
                  pl.BlockSpec((tm, d), row), pl.BlockSpec((1, d), fixed), pl.BlockSpec((1, d), fixed)],
        out_specs=[pl.BlockSpec((tm, d), row), pl.BlockSpec((tm, d), row)],
        out_shape=[jax.ShapeDtypeStruct((m, d), F32), jax.ShapeDtypeStruct((m, d), BF16)],
        compiler_params=_params("parallel"),
        name="out_proj",
    )(oa, ob, w_out, w_out, x, post_w.reshape(1, d), pre_w.reshape(1, d))


def _ffn_up_kernel(h_ref, wg_ref, wu_ref, act_ref, wgb_ref, wub_ref):
    @pl.when(pl.program_id(1) == 0)
    def _():
        wgb_ref[...] = wg_ref[...].astype(BF16)
        wub_ref[...] = wu_ref[...].astype(BF16)

    h = h_ref[...]
    act_ref[...] = (_silu(_dot(h, wgb_ref[...])) * _dot(h, wub_ref[...])).astype(act_ref.dtype)


def _ffn_up(hn, w_in_all, layer, tm=1024, tf=512):
    m, d = hn.shape
    tm = min(tm, m)
    nf = D_FF // tf
    wspec = pl.BlockSpec((d, tf), lambda j, i: (0, j))
    wshape = jax.ShapeDtypeStruct((d, D_FF), BF16)
    return pl.pallas_call(
        _ffn_up_kernel,
        grid=(nf, m // tm),
        in_specs=[pl.BlockSpec((tm, d), lambda j, i: (i, 0)),
                  pl.BlockSpec((None, d, tf), lambda j, i: (layer, 0, j)),
                  pl.BlockSpec((None, d, tf), lambda j, i: (layer, 0, j + nf))],
        out_specs=[pl.BlockSpec((tm, tf), lambda j, i: (i, j)), wspec, wspec],
        out_shape=[jax.ShapeDtypeStruct((m, D_FF), BF16), wshape, wshape],
        compiler_params=_params("parallel", "arbitrary"),
        name="ffn_up",
    )(hn, w_in_all, w_in_all)


def _ffn_up_rounded_kernel(h_ref, wg_ref, wu_ref, act_ref):
    h = h_ref[...]
    act_ref[...] = (_silu(_dot(h, wg_ref[...])) * _dot(h, wu_ref[...])).astype(act_ref.dtype)


def _ffn_up_rounded(hn, wg, wu, tm=1024, tf=512):
    m, d = hn.shape
    tm = min(tm, m)
    wspec = pl.BlockSpec((d, tf), lambda j, i: (0, j))
    return pl.pallas_call(
        _ffn_up_rounded_kernel,
        grid=(D_FF // tf, m // tm),
        in_specs=[pl.BlockSpec((tm, d), lambda j, i: (i, 0)), wspec, wspec],
        out_specs=pl.BlockSpec((tm, tf), lambda j, i: (i, j)),
        out_shape=jax.ShapeDtypeStruct((m, D_FF), BF16),
        compiler_params=_params("parallel", "parallel"),
        name="ffn_up_rounded",
    )(hn, wg, wu)


def _ffn_down_kernel(act_ref, wo_ref, x_ref, post_ref, pre_ref, xo_ref, hn_ref):
    xn = x_ref[...] + _rms(_dot(act_ref[...], wo_ref[...]), post_ref[...])
    xo_ref[...] = xn
    hn_ref[...] = _rms(xn, pre_ref[...]).astype(hn_ref.dtype)


def _ffn_down(act, w_out, x, post_w, pre_w, tm=256):
    m, d = x.shape
    tm = min(tm, m)
    row = lambda i: (i, 0)
    fixed = lambda i: (0, 0)
    return pl.pallas_call(
        _ffn_down_kernel,
        grid=(m // tm,),
        in_specs=[pl.BlockSpec((tm, D_FF), row),
                  pl.BlockSpec((D_FF, d), fixed, pipeline_mode=pl.Buffered(1)),
                  pl.BlockSpec((tm, d), row), pl.BlockSpec((1, d), fixed), pl.BlockSpec((1, d), fixed)],
        out_specs=[pl.BlockSpec((tm, d), row), pl.BlockSpec((tm, d), row)],
        out_shape=[jax.ShapeDtypeStruct((m, d), F32), jax.ShapeDtypeStruct((m, d), BF16)],
        compiler_params=_params("parallel"),
        name="ffn_down",
    )(act, w_out, x, post_w.reshape(1, d), pre_w.reshape(1, d))


def _hgrn_kernel(q_ref, f_ref, i_ref, g_ref, lbraw_ref, nw_ref, s0_ref, o_ref, sout_ref, st_ref,
                 *, layer, sc, n_sub, bb):
    ci = pl.program_id(1)

    @pl.when(ci == 0)
    def _():
        for b in range(bb):
            for h in range(H_A):
                st_ref[b, h] = s0_ref[b, h].T

    raw = lbraw_ref[...]
    e = jnp.exp(raw - jnp.max(raw, axis=0, keepdims=True))
    pr = e / jnp.sum(e, axis=0, keepdims=True)
    lb_all = jnp.zeros((1, D_A), F32)
    for r in range(1, layer + 1):
        lb_all = lb_all + pr[r:r + 1, :]
    nw_all = nw_ref[...]
    tri = _lower_tri(sc, strict=False).astype(F32)
    row = lax.broadcasted_iota(jnp.int32, (sc, LANES), 0)
    sub = 8
    lane8 = lax.broadcasted_iota(jnp.int32, (sub, LANES), 1)

    oml = 1.0 - lb_all
    heads = [slice(h * DK_A, (h + 1) * DK_A) for h in range(H_A)]

    def body(idx, carry):
        b = idx // n_sub
        rows = pl.ds(pl.multiple_of((idx % n_sub) * sc, sc), sc)
        zf = f_ref[b, rows, :]
        iv = i_ref[b, rows, :]
        fg = lb_all + oml * _sigmoid(zf)
        logf = jnp.log(jnp.maximum(fg, TINY))
        kk = oml * _sigmoid(-zf)
        qs = _silu(q_ref[b, rows, :])
        bc = _dot_mask_lhs(tri, logf)
        blast = bc[sc - 1:sc, :]
        qe = (qs * jnp.exp(bc)).astype(BF16)
        kh = (kk * jnp.exp(blast - bc)).astype(BF16)
        ivb = iv.astype(BF16)
        keep = jnp.exp(blast)
        sts = [st_ref[b, h] for h in range(H_A)]
        o_inter = [_dot_nt(qe[:, sl], sts[h].astype(BF16)) for h, sl in enumerate(heads)]
        upd = [_dot_tn(ivb[:, sl], kh[:, sl]) for sl in heads]
        for h, sl in enumerate(heads):
            st_ref[b, h] = sts[h] * keep[:, sl] + upd[h]
        atts = []
        for h, sl in enumerate(heads):
            bc_h, kk_h, qs_h = bc[:, sl], kk[:, sl], qs[:, sl]
            att_t = [jnp.zeros((sub, LANES), F32)] * (sc // sub)
            for t in range(sc):
                n = (t // sub + 1) * sub
                dec = jnp.exp(bc_h[t:t + 1, :] - bc_h[:n])
                pair = jnp.where(row[:n] <= t, kk_h[:n] * dec * qs_h[t:t + 1, :], 0.0)
                a_col = jnp.sum(pair, axis=-1, keepdims=True)
                for k in range(n // sub):
                    att_t[k] = jnp.where(lane8 == t, a_col[k * sub:(k + 1) * sub], att_t[k])
            atts.append(jnp.concatenate(att_t, axis=0)[:, :sc].astype(BF16))
        o_intra = [_dot_tn(atts[h], ivb[:, sl]) for h, sl in enumerate(heads)]
        outs = []
        for h in range(H_A):
            o = o_intra[h] + o_inter[h]
            outs.append(o * lax.rsqrt(jnp.mean(o * o, axis=-1, keepdims=True) + RMS_EPS))
        o = jnp.concatenate(outs, axis=-1) * nw_all
        o_ref[b, rows, :] = (o * _silu(g_ref[b, rows, :])).astype(o_ref.dtype)
        return carry

    lax.fori_loop(0, bb * n_sub, body, 0, unroll=math.gcd(bb * n_sub, 4))

    @pl.when(ci == pl.num_programs(1) - 1)
    def _():
        for b in range(bb):
            for h in range(H_A):
                sout_ref[b, h] = st_ref[b, h].T


def _stacked_state_call(kernel_fn, prev, state_out_index, in_specs, args, **kw):
    if prev is None:
        return pl.pallas_call(kernel_fn, in_specs=in_specs, **kw)(*args)

    def with_prev(prev_ref, *refs):
        del prev_ref
        kernel_fn(*refs)

    return pl.pallas_call(with_prev, in_specs=[pl.BlockSpec(memory_space=pl.ANY)] + in_specs,
                          input_output_aliases={0: state_out_index}, **kw)(prev, *args)


def _hgrn(z_a, lb_raw, norm_w, s_all, prev, layer, bb, tc, sc):
    bsz, t, _ = z_a.shape
    col = lambda k: (lambda b, c: (b, c, k))
    fixed2 = lambda b, c: (0, 0)
    state = pl.BlockSpec((None, bb, H_A, DK_A, DV_A), lambda b, c: (layer, b, 0, 0, 0))
    return _stacked_state_call(
        functools.partial(_hgrn_kernel, layer=layer, sc=sc, n_sub=tc // sc, bb=bb), prev, 1,
        [pl.BlockSpec((bb, tc, D_A), col(0)), pl.BlockSpec((bb, tc, D_A), col(1)),
         pl.BlockSpec((bb, tc, D_A), col(2)), pl.BlockSpec((bb, tc, D_A), col(3)),
         pl.BlockSpec((N_EVEN, D_A), fixed2), pl.BlockSpec((1, D_A), fixed2), state],
        (z_a, z_a, z_a, z_a, lb_raw, norm_w.reshape(1, D_A), s_all),
        grid=(bsz // bb, t // tc),
        out_specs=[pl.BlockSpec((bb, tc, D_A), lambda b, c: (b, c, 0)), state],
        out_shape=[jax.ShapeDtypeStruct((bsz, t, D_A), BF16),
                   jax.ShapeDtypeStruct(s_all.shape, F32)],
        scratch_shapes=[pltpu.VMEM((bb, H_A, DV_A, DK_A), F32)],
        compiler_params=_params("parallel", "arbitrary"),
        name="hgrn2",
    )


def _head_sums(x):
    rows = x.shape[0]
    seg = (lax.broadcasted_iota(jnp.int32, (LANES, LANES), 0) // N_B
           == lax.broadcasted_iota(jnp.int32, (LANES, LANES), 1) // N_B).astype(F32)
    n_tiles = D_B // LANES
    xs = jnp.concatenate([x[:, i * LANES:(i + 1) * LANES] for i in range(n_tiles)], axis=0)
    s = _dot_mask_rhs(xs, seg, pieces=2)
    return jnp.concatenate([s[i * rows:(i + 1) * rows] for i in range(n_tiles)], axis=1)


def _rwkv_kernel(z_ref, sh0_ref, s0_ref, mu_ref, w0_ref, w2_ref, a0_ref, a2_ref, g2_ref, kk_ref, ka_ref,
                 rk_ref, lnw_ref, lnb_ref, o_ref, sout_ref, shout_ref, s_ref, prev_ref, *, c, nb):
    ci = pl.program_id(1)
    rows = nb * c
    dz = D_RWKV_PROJ

    @pl.when(ci == 0)
    def _():
        s_ref[...] = s0_ref[...]
        prev_ref[...] = sh0_ref[...]

    blocks = [slice(b * c, (b + 1) * c) for b in range(nb)]
    last = [slice((b + 1) * c - 1, (b + 1) * c) for b in range(nb)]

    def per_block(rows_of):
        n = rows_of[0].shape[-1]
        blk = lax.broadcasted_iota(jnp.int32, (rows, n), 0) // c
        out = jnp.broadcast_to(rows_of[0], (rows, n))
        for b in range(1, nb):
            out = jnp.where(blk == b, rows_of[b], out)
        return out

    zb = z_ref[...].reshape(rows, dz)
    row = lax.broadcasted_iota(jnp.int32, (rows, dz), 0)
    prev = jnp.where(row % c == 0, per_block([prev_ref[b] for b in range(nb)]), pltpu.roll(zb, 1, axis=0))
    for b in range(nb):
        prev_ref[b] = zb[last[b], :]
    zs = zb + (prev - zb) * mu_ref[...]
    r = zs[:, 0:D_B]
    kb = zs[:, D_B:2 * D_B]
    v = zs[:, 2 * D_B:3 * D_B]
    o0 = 3 * D_B
    wl = zs[:, o0:o0 + R_W]
    al = zs[:, o0 + R_W:o0 + R_W + R_A]
    gl = zs[:, o0 + R_W + R_A:o0 + R_W + R_A + R_G]
    wlin = w0_ref[...] + _dot(jnp.tanh(wl).astype(BF16), w2_ref[...])
    w_log = -math.exp(-0.5) * _sigmoid(wlin)
    a = _sigmoid(a0_ref[...] + _dot(al.astype(BF16), a2_ref[...]))
    gate = _dot(_sigmoid(gl).astype(BF16), g2_ref[...])
    kkr = kb * kk_ref[...]
    kmod = kb * (1.0 + (a - 1.0) * ka_ref[...])

    ri = lax.broadcasted_iota(jnp.int32, (rows, rows), 0)
    cj = lax.broadcasted_iota(jnp.int32, (rows, rows), 1)
    tri = jnp.where(ri // c == cj // c, (ri >= cj).astype(F32), 0.0)
    bc = _dot_mask_lhs(tri, w_log)
    b_last = [bc[l, :] for l in last]
    blast = per_block(b_last)
    gam = jnp.exp(bc)
    inv_gam = jnp.exp(-bc)
    gam_prev = jnp.exp(bc - w_log)
    to_end = jnp.exp(blast - bc)
    gam_last = [jnp.exp(x) for x in b_last]
    kkn = kkr / jnp.maximum(jnp.sqrt(_head_sums(kkr * kkr)), 1e-12)
    beta = kkn * a
    xa = -kkn * gam_prev
    xr = r * gam
    yb = beta * inv_gam
    yk = kmod * inv_gam
    eb = beta * to_end
    ek = kmod * to_end
    lhs = [jnp.concatenate([xa[bl], xr[bl]], axis=0).astype(BF16) for bl in blocks]
    rhs = [jnp.concatenate([yb[bl], yk[bl]], axis=0).astype(BF16) for bl in blocks]
    end = [jnp.concatenate([eb[bl], ek[bl]], axis=0).astype(BF16) for bl in blocks]
    vb = [v[bl].astype(BF16) for bl in blocks]

    ri2 = lax.broadcasted_iota(jnp.int32, (2 * c, 2 * c), 0)
    cj2 = lax.broadcasted_iota(jnp.int32, (2 * c, 2 * c), 1)
    keep = (ri2 % c + ri2 // c) > (cj2 % c)
    n_steps = max(1, int(math.log2(c)))
    heads = [slice(h * N_B, (h + 1) * N_B) for h in range(H_B)]
    pairs = [(b, h) for b in range(nb) for h in range(H_B)]
    idx = range(len(pairs))
    s0 = [s_ref[b, h] for b, h in pairs]
    res = [_dot_nt(lhs[b][:, heads[h]], jnp.concatenate([rhs[b][:, heads[h]], s0[i].astype(BF16)], axis=0))
           for i, (b, h) in enumerate(pairs)]
    att = [jnp.where(keep, res[i][:, :2 * c], 0.0) for i in idx]
    from_s0 = [res[i][:, 2 * c:] for i in idx]
    u = [from_s0[i][:c] + _dot(att[i][:c, c:].astype(BF16), vb[b][:, heads[h]]) for i, (b, h) in enumerate(pairs)]
    pb = [att[i][:c, :c].astype(BF16) for i in idx]
    for it in range(n_steps - 1):
        both = [_dot(pb[i], jnp.concatenate([u[i].astype(BF16), pb[i]], axis=1)) for i in idx]
        u = [u[i] + both[i][:, :N_B] for i in idx]
        pb = [both[i][:, N_B:].astype(BF16) for i in idx]
    u = [u[i] + _dot(pb[i], u[i].astype(BF16)) for i in idx]
    uv = [jnp.concatenate([u[i].astype(BF16), vb[b][:, heads[h]]], axis=0) for i, (b, h) in enumerate(pairs)]
    ys = [from_s0[i][c:] + _dot(att[i][c:].astype(BF16), uv[i]) for i in idx]
    s_new = [_dot_tn(uv[i], end[b][:, heads[h]]) for i, (b, h) in enumerate(pairs)]
    for i, (b, h) in enumerate(pairs):
        s_ref[b, h] = s0[i] * gam_last[b][:, heads[h]] + s_new[i]

    y = jnp.concatenate([jnp.concatenate(ys[b * H_B:(b + 1) * H_B], axis=-1) for b in range(nb)], axis=0)
    yc = y - _head_sums(y) * (1.0 / N_B)
    var = _head_sums(yc * yc) * (1.0 / N_B)
    yn = yc * lax.rsqrt(var + GN_EPS) * lnw_ref[...] + lnb_ref[...]
    out = (yn + _head_sums(r * kmod * rk_ref[...]) * v) * gate
    o_ref[...] = out.reshape(nb, c, D_B).astype(o_ref.dtype)

    @pl.when(ci == pl.num_programs(1) - 1)
    def _():
        sout_ref[...] = s_ref[...]
        for b in range(nb):
            shout_ref[b] = zb[last[b], :]


def _rwkv(z_b, s_all, prev, shift0, p, j, c, nb):
    bsz, t, dz = z_b.shape
    vec = lambda x: x.reshape(1, -1)
    fixed2 = lambda b, ci: (0, 0)
    vspec = lambda n: pl.BlockSpec((1, n), fixed2)
    state = pl.BlockSpec((None, nb, H_B, N_B, N_B), lambda b, ci: (j, b, 0, 0, 0))
    shift = pl.BlockSpec((nb, 1, dz), lambda b, ci: (b, 0, 0))
    return _stacked_state_call(
        functools.partial(_rwkv_kernel, c=c, nb=nb), prev, 1,
        [pl.BlockSpec((nb, c, dz), lambda b, ci: (b, ci, 0)), shift, state,
         vspec(dz), vspec(D_B),
         pl.BlockSpec((R_W, D_B), fixed2), vspec(D_B),
         pl.BlockSpec((R_A, D_B), fixed2), pl.BlockSpec((R_G, D_B), fixed2),
         vspec(D_B), vspec(D_B), vspec(D_B), vspec(D_B), vspec(D_B)],
        (z_b, shift0.reshape(bsz, 1, dz), s_all, vec(p['rwkv_mu'][j]), vec(p['rwkv_w0'][j]),
         p['rwkv_w2'][j].astype(BF16), vec(p['rwkv_a0'][j]), p['rwkv_a2'][j].astype(BF16),
         p['rwkv_g2'][j].astype(BF16), vec(p['rwkv_k_k'][j]), vec(p['rwkv_k_a'][j]),
         vec(p['rwkv_r_k'][j]), vec(p['rwkv_ln_w'][j]), vec(p['rwkv_ln_b'][j])),
        grid=(bsz // nb, t // c),
        out_specs=[pl.BlockSpec((nb, c, D_B), lambda b, ci: (b, ci, 0)), state, shift],
        out_shape=[jax.ShapeDtypeStruct((bsz, t, D_B), BF16),
                   jax.ShapeDtypeStruct(s_all.shape, F32),
                   jax.ShapeDtypeStruct((bsz, 1, dz), F32)],
        scratch_shapes=[pltpu.VMEM((nb, H_B, N_B, N_B), F32), pltpu.VMEM((nb, 1, dz), F32)],
        compiler_params=_params("parallel", "arbitrary"),
        name="rwkv7",
    )


def _rope_kernel(ang_ref, sin_ref, cos_ref, *, tt, pos0):
    row = lax.broadcasted_iota(jnp.int32, (tt, DK_C), 0)
    ang = (pos0 + pl.program_id(0) * tt + row).astype(F32) * ang_ref[...]
    sin_ref[...] = jnp.sin(ang)
    cos_ref[...] = jnp.cos(ang)


def _rope_table(pos0, t):
    angle = 1.0 / (10000.0 ** jnp.linspace(0.0, 1.0, DK_C // 2, dtype=F32))
    angle = jnp.repeat(angle, 2).reshape(1, DK_C)
    tt = min(t, 256)
    out = pl.BlockSpec((tt, DK_C), lambda i: (i, 0))
    return pl.pallas_call(
        functools.partial(_rope_kernel, tt=tt, pos0=pos0),
        grid=(t // tt,),
        in_specs=[pl.BlockSpec((1, DK_C), lambda i: (0, 0))],
        out_specs=[out, out],
        out_shape=[jax.ShapeDtypeStruct((t, DK_C), F32)] * 2,
        compiler_params=_params("parallel"),
        name="rope_table",
    )(angle)


def _ret_kernel(q_ref, k_ref, v_ref, g_ref, sin_ref, cos_ref, s0_ref, o_ref, sout_ref, s_ref, *, c, nb):
    ci = pl.program_id(1)

    @pl.when(ci == 0)
    def _():
        s_ref[...] = s0_ref[...]

    sin = jnp.concatenate([sin_ref[...]] * H_C, axis=1)
    cos = jnp.concatenate([cos_ref[...]] * H_C, axis=1)
    even = (lax.broadcasted_iota(jnp.int32, (c, D_C), 1) % 2) == 0

    def rope(x):
        rot = jnp.where(even, -pltpu.roll(x, D_C - 1, axis=1), pltpu.roll(x, 1, axis=1))
        return x * cos + rot * sin

    ti = lax.broadcasted_iota(jnp.int32, (c, c), 0)
    si = lax.broadcasted_iota(jnp.int32, (c, c), 1)
    causal = ti >= si
    dist = jnp.where(causal, (ti - si).astype(F32), 0.0)
    tcol = lax.broadcasted_iota(jnp.int32, (c, 1), 0).astype(F32)
    heads = [slice(h * DK_C, (h + 1) * DK_C) for h in range(H_C)]
    log_gamma = [math.log1p(-2.0 ** (-5.0 - h)) for h in range(H_C)]
    dec = [jnp.where(causal, jnp.exp(dist * lg), 0.0) for lg in log_gamma]
    from_start = [jnp.exp((tcol + 1.0) * lg) for lg in log_gamma]
    to_end = [jnp.exp((float(c) - 1.0 - tcol) * lg) for lg in log_gamma]
    keep = [math.exp(float(c) * lg) for lg in log_gamma]

    pairs = [(b, h) for b in range(nb) for h in range(H_C)]
    qs = [rope(q_ref[b]) for b in range(nb)]
    ks = [rope(k_ref[b]) * (DK_C ** -0.5) for b in range(nb)]
    vb = [v_ref[b].astype(BF16) for b in range(nb)]
    qb = [q.astype(BF16) for q in qs]
    kb = [k.astype(BF16) for k in ks]
    s_old = [s_ref[b, h] for b, h in pairs]
    att = [_dot_nt(qb[b][:, heads[h]], kb[b][:, heads[h]]) for b, h in pairs]
    o_st = [_dot((qs[b][:, heads[h]] * from_start[h]).astype(BF16), s_old[i].astype(BF16))
            for i, (b, h) in enumerate(pairs)]
    o_in = [_dot((att[i] * dec[h]).astype(BF16), vb[b][:, heads[h]]) for i, (b, h) in enumerate(pairs)]
    upd = [_dot_tn((ks[b][:, heads[h]] * to_end[h]).astype(BF16), vb[b][:, heads[h]]) for b, h in pairs]
    s_new = [keep[h] * s_old[i] + upd[i] for i, (b, h) in enumerate(pairs)]
    for i, (b, h) in enumerate(pairs):
        s_ref[b, h] = s_new[i]
    for b in range(nb):
        outs = []
        for h in range(H_C):
            o = o_in[b * H_C + h] + o_st[b * H_C + h]
            outs.append(o * lax.rsqrt(jnp.mean(o * o, axis=-1, keepdims=True) + RMS_EPS))
        o_ref[b] = (jnp.concatenate(outs, axis=-1) * _silu(g_ref[b])).astype(o_ref.dtype)

    @pl.when(ci == pl.num_programs(1) - 1)
    def _():
        for i, (b, h) in enumerate(pairs):
            sout_ref[b, h] = s_new[i]


def _retention(z, s_all, prev, j, sin, cos, c, nb):
    bsz, t, _ = z.shape
    col = lambda k: (lambda b, ci: (b, ci, k))
    state = pl.BlockSpec((None, nb, H_C, DK_C, DV_C), lambda b, ci: (j, b, 0, 0, 0))
    table = pl.BlockSpec((c, DK_C), lambda b, ci: (ci, 0))
    return _stacked_state_call(
        functools.partial(_ret_kernel, c=c, nb=nb), prev, 1,
        [pl.BlockSpec((nb, c, D_C), col(0)), pl.BlockSpec((nb, c, D_C), col(1)),
         pl.BlockSpec((nb, c, D_C), col(2)), pl.BlockSpec((nb, c, D_C), col(3)),
         table, table, state],
        (z, z, z, z, sin, cos, s_all),
        grid=(bsz // nb, t // c),
        out_specs=[pl.BlockSpec((nb, c, D_C), lambda b, ci: (b, ci, 0)), state],
        out_shape=[jax.ShapeDtypeStruct((bsz, t, D_C), BF16), jax.ShapeDtypeStruct(s_all.shape, F32)],
        scratch_shapes=[pltpu.VMEM((nb, H_C, DK_C, DV_C), F32)],
        compiler_params=_params("parallel", "arbitrary"),
        name="retention",
    )


def _rglru_kernel(y_ref, x_ref, conv0_ref, h0_ref, cw_ref, cb_ref, wa_ref, wx_ref, ba_ref, bx_ref, lam_ref,
                  o_ref, convout_ref, hout_ref, tail_ref, h_ref, *, tc, nb, pos0):
    ci = pl.program_id(1)
    rows = nb * tc
    sub = 8

    @pl.when(ci == 0)
    def _():
        tail_ref[...] = conv0_ref[...]
        h_ref[...] = h0_ref[...]

    x = x_ref[...].reshape(rows, D_D)
    row = lax.broadcasted_iota(jnp.int32, (rows, D_D), 0)
    t_in = row % tc
    tails = [tail_ref[b] for b in range(nb)]

    def shifted(d):
        r = pltpu.roll(x, d, axis=0)
        for m in range(d):
            k = CONV_W - 1 - d + m
            for b in range(nb):
                r = jnp.where(row == b * tc + m, tails[b][k:k + 1, :], r)
        return r

    cw = cw_ref[...]
    xc = cb_ref[...] + cw[CONV_W - 1:CONV_W, :] * x
    for d in range(1, CONV_W):
        xc = xc + cw[CONV_W - 1 - d:CONV_W - d, :] * shifted(d)
    new_tails = [x[(b + 1) * tc - (CONV_W - 1):(b + 1) * tc, :] for b in range(nb)]
    for b in range(nb):
        tail_ref[b] = new_tails[b]

    ra, rx = [], []
    for hd in range(H_D):
        xh = xc[:, hd * BW_D:(hd + 1) * BW_D].astype(BF16)
        ra.append(_dot(xh, wa_ref[hd]))
        rx.append(_dot(xh, wx_ref[hd]))
    rg = _sigmoid(jnp.concatenate(ra, axis=-1) + ba_ref[...])
    ig = _sigmoid(jnp.concatenate(rx, axis=-1) + bx_ref[...])
    nlam = -lam_ref[...]
    softplus = jnp.maximum(nlam, 0.0) + jnp.log1p(jnp.exp(-jnp.abs(nlam)))
    log_a = -LRU_C * rg * softplus
    a = jnp.exp(log_a)
    mult = jnp.sqrt(-jnp.tanh(log_a) * (a * a + 1.0))
    mult = jnp.where(pos0 + ci * tc + t_in == 0, 1.0, mult)
    bt = mult * ig * xc

    in_sub = row % sub
    d = 1
    while d < sub:
        m = in_sub >= d
        bt = jnp.where(m, a * pltpu.roll(bt, d, axis=0) + bt, bt)
        a = jnp.where(m, a * pltpu.roll(a, d, axis=0), a)
        d *= 2
    hs = []
    for b in range(nb):
        carry = h_ref[b]
        for k in range(tc // sub):
            sl = slice(b * tc + k * sub, b * tc + (k + 1) * sub)
            h = bt[sl] + a[sl] * carry
            hs.append(h)
            carry = h[sub - 1:sub, :]
        h_ref[b] = carry
    hseq = jnp.concatenate(hs, axis=0)
    y = y_ref[...].reshape(rows, D_D)
    gelu = 0.5 * y * (1.0 + jnp.tanh(math.sqrt(2.0 / math.pi) * (y + 0.044715 * (y * y * y))))
    o_ref[...] = (hseq * gelu).reshape(nb, tc, D_D).astype(o_ref.dtype)

    @pl.when(ci == pl.num_programs(1) - 1)
    def _():
        for b in range(nb):
            convout_ref[b] = new_tails[b]
        hout_ref[...] = h_ref[...]


def _rglru(z, conv0, h0, p, j, pos0, tc, nb):
    bsz, t, _ = z.shape
    vec = lambda x: x.reshape(1, -1)
    fixed2 = lambda b, ci: (0, 0)
    fixed3 = lambda b, ci: (0, 0, 0)
    vspec = pl.BlockSpec((1, D_D), fixed2)
    ycol = 4 * D_C // D_D
    conv = pl.BlockSpec((nb, CONV_W - 1, D_D), lambda b, ci: (b, 0, 0))
    hid = pl.BlockSpec((nb, 1, D_D), lambda b, ci: (b, 0, 0))
    return pl.pallas_call(
        functools.partial(_rglru_kernel, tc=tc, nb=nb, pos0=pos0),
        grid=(bsz // nb, t // tc),
        in_specs=[pl.BlockSpec((nb, tc, D_D), lambda b, ci: (b, ci, ycol)),
                  pl.BlockSpec((nb, tc, D_D), lambda b, ci: (b, ci, ycol + 1)),
                  conv, hid,
                  pl.BlockSpec((CONV_W, D_D), fixed2), vspec,
                  pl.BlockSpec((H_D, BW_D, BW_D), fixed3), pl.BlockSpec((H_D, BW_D, BW_D), fixed3),
                  vspec, vspec, vspec],
        out_specs=[pl.BlockSpec((nb, tc, D_D), lambda b, ci: (b, ci, 0)), conv, hid],
        out_shape=[jax.ShapeDtypeStruct((bsz, t, D_D), BF16),
                   jax.ShapeDtypeStruct((bsz, CONV_W - 1, D_D), F32),
                   jax.ShapeDtypeStruct((bsz, 1, D_D), F32)],
        scratch_shapes=[pltpu.VMEM((nb, CONV_W - 1, D_D), F32), pltpu.VMEM((nb, 1, D_D), F32)],
        compiler_params=_params("parallel", "arbitrary"),
        name="rglru",
    )(z, z, conv0, h0.reshape(bsz, 1, D_D), p['conv_w'][j], vec(p['conv_b'][j]),
      p['rglru_wa'][j].astype(BF16), p['rglru_wx'][j].astype(BF16),
      vec(p['rglru_ba'][j]), vec(p['rglru_bx'][j]), vec(p['rglru_lambda'][j]))


def _tiles(bsz, t):
    if t >= 256:
        return dict(hgrn=(1, 256, 16), rwkv=(64, math.gcd(bsz, 2)), ret=(256, 1), lru=(256, 1))
    nb = math.gcd(bsz, 4)
    return dict(hgrn=(nb, t, min(t, 8)), rwkv=(t, nb), ret=(t, nb), lru=(t, nb))


def _run_trunk(x, pos0, st_hgrn, st_rwkv, st_shift, st_ret, st_conv, st_h, p, wb):
    bsz, t, d = x.shape
    m = bsz * t
    tl = _tiles(bsz, t)
    x2 = x.reshape(m, d)
    hn = _rmsnorm_bf16(x2, p['norm_mix_pre'][0], tm=min(512, m))
    n_hgrn, n_rwkv, n_ret = None, None, None
    n_shift, n_conv, n_h = [], [], []
    sin, cos = _rope_table(pos0, t)

    def project(key, a, w_all, layer, col0, n, tn, tm, tn_rounded):
        if key in wb:
            return _in_proj_rounded(a, wb[key], tn=tn_rounded, tm=1024)
        z, wb[key] = _in_proj(a, w_all, layer, col0, n, tn=tn, tm=tm)
        return z

    for l in range(DEPTH):
        j = l // 2
        if l % 2 == 0:
            z_a = project(('a', j), hn, p['w_in_even'], j, 0, 4 * D_A, tn=1024, tm=1024, tn_rounded=1024)
            z_a = z_a.reshape(bsz, t, 4 * D_A)
            z_b = project(('b', j), hn, p['w_in_even'], j, 4 * D_A // 256, D_RWKV_PROJ, tn=256, tm=2048,
                          tn_rounded=D_RWKV_PROJ // 2)
            z_b = z_b.reshape(bsz, t, D_RWKV_PROJ)
            bb, tc, sc = tl['hgrn']
            o_a, n_hgrn = _hgrn(z_a, p['hgrn_lb_raw'], p['hgrn_norm_w'][j], st_hgrn, n_hgrn, j, bb, tc, sc)
            o_b, n_rwkv, sh = _rwkv(z_b, st_rwkv, n_rwkv, st_shift[j], p, j, *tl['rwkv'])
            n_shift.append(sh.reshape(bsz, D_RWKV_PROJ))
            w_out = wb['w_out_even'][j]
        else:
            z = project(('odd', j), hn, p['w_in_odd'], j, 0, 4 * D_C + 2 * D_D, tn=1024, tm=1024, tn_rounded=1024)
            z = z.reshape(bsz, t, 4 * D_C + 2 * D_D)
            o_a, n_ret = _retention(z, st_ret, n_ret, j, sin, cos, *tl['ret'])
            o_b, scv, shh = _rglru(z, st_conv[j], st_h[j], p, j, pos0, *tl['lru'])
            n_conv.append(scv)
            n_h.append(shh.reshape(bsz, D_D))
            w_out = wb['w_out_odd'][j]
        x2, hn = _outproj(o_a.reshape(m, -1), o_b.reshape(m, -1), w_out, x2,
                          p['norm_mix_post'][l], p['norm_ffn_pre'][l], tm=min(512, m))
        next_pre = p['norm_mix_pre'][(l + 1) % DEPTH]
        if ('ffn', l) in wb:
            act = _ffn_up_rounded(hn, *wb[('ffn', l)])
        else:
            act, w_gate, w_up = _ffn_up(hn, p['w_ffn_in'], l)
            wb[('ffn', l)] = (w_gate, w_up)
        x2, hn = _ffn_down(act, wb['w_ffn_out'][l], x2, p['norm_ffn_post'][l], next_pre)
    return x2.reshape(bsz, t, d), (n_hgrn, n_rwkv, jnp.stack(n_shift),
                                   n_ret, jnp.stack(n_conv), jnp.stack(n_h))


def _zero_state(s, batch):
    return jnp.zeros((s.shape[0], batch) + s.shape[2:], s.dtype)


def kernel(x_prompt, x_sample, state_hgrn, state_rwkv, state_rwkv_shift, state_ret, state_rglru_conv, state_rglru_h, norm_mix_pre, norm_mix_post, norm_ffn_pre, norm_ffn_post, w_in_even, w_out_even, hgrn_lb_raw, hgrn_norm_w, rwkv_mu, rwkv_w0, rwkv_w2, rwkv_a0, rwkv_a2, rwkv_g2, rwkv_k_k, rwkv_k_a, rwkv_r_k, rwkv_ln_w, rwkv_ln_b, w_in_odd, w_out_odd, conv_w, conv_b, rglru_wa, rglru_ba, rglru_wx, rglru_bx, rglru_lambda, w_ffn_in, w_ffn_out):
    p = dict(norm_mix_pre=norm_mix_pre, norm_mix_post=norm_mix_post, norm_ffn_pre=norm_ffn_pre,
             norm_ffn_post=norm_ffn_post, hgrn_lb_raw=hgrn_lb_raw, hgrn_norm_w=hgrn_norm_w,
             rwkv_mu=rwkv_mu, rwkv_w0=rwkv_w0, rwkv_w2=rwkv_w2, rwkv_a0=rwkv_a0, rwkv_a2=rwkv_a2,
             rwkv_g2=rwkv_g2, rwkv_k_k=rwkv_k_k, rwkv_k_a=rwkv_k_a, rwkv_r_k=rwkv_r_k,
             rwkv_ln_w=rwkv_ln_w, rwkv_ln_b=rwkv_ln_b, conv_w=conv_w, conv_b=conv_b,
             rglru_wa=rglru_wa, rglru_ba=rglru_ba, rglru_wx=rglru_wx, rglru_bx=rglru_bx,
             rglru_lambda=rglru_lambda, w_in_even=w_in_even, w_in_odd=w_in_odd, w_ffn_in=w_ffn_in)
    wb = dict(w_out_even=w_out_even.astype(BF16), w_out_odd=w_out_odd.astype(BF16),
              w_ffn_out=w_ffn_out.astype(BF16))
    bp = x_prompt.shape[0]
    y_prompt, st_p = _run_trunk(x_prompt, 0,
                                _zero_state(state_hgrn, bp), _zero_state(state_rwkv, bp),
                                _zero_state(state_rwkv_shift, bp), _zero_state(state_ret, bp),
                                _zero_state(state_rglru_conv, bp), _zero_state(state_rglru_h, bp), p, wb)
    y_sample, st_s = _run_trunk(x_sample, PAST_LEN, state_hgrn, state_rwkv, state_rwkv_shift,
                                state_ret, state_rglru_conv, state_rglru_h, p, wb)
    hgrn_p, rwkv_p, shift_p, ret_p, conv_p, h_p = st_p
    hgrn_s, rwkv_s, shift_s, ret_s, conv_s, h_s = st_s
    return (y_prompt, y_sample, hgrn_p, hgrn_s, rwkv_p, rwkv_s, shift_p, shift_s,
            ret_p, ret_s, conv_p, conv_s, h_p, h_s)
```

```python
import functools
import math

import jax
import jax.numpy as jnp
from jax import lax
from jax.experimental import pallas as pl
from jax.experimental.pallas import tpu as pltpu

F32 = jnp.float32
BF16 = jnp.bfloat16

D_MODEL = 2048
DEPTH = 4
N_EVEN = (DEPTH + 1) // 2
N_ODD = DEPTH // 2
D_A = D_MODEL // 2
H_A = 8
DK_A = 128
DV_A = 128
D_B = D_MODEL // 2
N_B = 64
H_B = D_B // N_B
R_W = 64
R_A = 64
R_G = 128
D_RWKV_PROJ = 3 * D_B + R_W + R_A + R_G
D_C = D_MODEL // 2
H_C = 4
DK_C = D_C // H_C
DV_C = D_C // H_C
D_D = D_MODEL // 2
H_D = 4
BW_D = D_D // H_D
CONV_W = 4
LRU_C = 8.0
D_FF = ((8 * D_MODEL + 3 * 256 - 1) // (3 * 256)) * 256
RMS_EPS = 1e-6
GN_EPS = 64e-5
TINY = 1e-30
PAST_LEN = 16384

VMEM_LIMIT_BYTES = 56 * 1024 * 1024
LANES = 128


def _params(*sem):
    return pltpu.CompilerParams(dimension_semantics=sem, vmem_limit_bytes=VMEM_LIMIT_BYTES)


def _rms(x, w):
    return x * lax.rsqrt(jnp.mean(x * x, axis=-1, keepdims=True) + RMS_EPS) * w


def _sigmoid(x):
    return jax.nn.sigmoid(x)


def _silu(x):
    return x * jax.nn.sigmoid(x)


def _dot(a, b):
    return jnp.dot(a, b, preferred_element_type=F32)


def _dot_nt(a, b):
    return lax.dot_general(a, b, (((1,), (1,)), ((), ())), preferred_element_type=F32)


def _dot_tn(a, b):
    return lax.dot_general(a, b, (((0,), (0,)), ((), ())), preferred_element_type=F32)


def _split(x, pieces):
    out = []
    for _ in range(pieces - 1):
        part = x.astype(BF16)
        out.append(part)
        x = x - part.astype(F32)
    out.append(x.astype(BF16))
    return out


def _dot_mask_lhs(mask, x):
    m = mask.astype(BF16)
    return sum(_dot(m, part) for part in _split(x, 3))


def _dot_mask_rhs(x, mask, pieces):
    m = mask.astype(BF16)
    return sum(_dot(part, m) for part in _split(x, pieces))


def _row_parts(n, size):
    size = min(size, n)
    return [slice(i, i + size) for i in range(0, n, size)]


def _lower_tri(n, strict):
    r = lax.broadcasted_iota(jnp.int32, (n, n), 0)
    c = lax.broadcasted_iota(jnp.int32, (n, n), 1)
    return (r > c) if strict else (r >= c)


def _rmsnorm_kernel(x_ref, w_ref, o_ref):
    o_ref[...] = _rms(x_ref[...], w_ref[...]).astype(o_ref.dtype)


def _rmsnorm_bf16(x, w, tm=512):
    m, d = x.shape
    return pl.pallas_call(
        _rmsnorm_kernel,
        grid=(m // tm,),
        in_specs=[pl.BlockSpec((tm, d), lambda i: (i, 0)), pl.BlockSpec((1, d), lambda i: (0, 0))],
        out_specs=pl.BlockSpec((tm, d), lambda i: (i, 0)),
        out_shape=jax.ShapeDtypeStruct((m, d), BF16),
        compiler_params=_params("parallel"),
        name="rmsnorm",
    )(x, w.reshape(1, d))


def _in_proj_kernel(a_ref, w_ref, o_ref, wb_ref):
    @pl.when(pl.program_id(1) == 0)
    def _():
        wb_ref[...] = w_ref[...].astype(BF16)

    o_ref[...] = _dot(a_ref[...], wb_ref[...])


def _in_proj(a, w_all, layer, n, tn, tm):
    m, k = a.shape
    tm = min(tm, m)
    return pl.pallas_call(
        _in_proj_kernel,
        grid=(n // tn, m // tm),
        in_specs=[pl.BlockSpec((tm, k), lambda j, i: (i, 0)),
                  pl.BlockSpec((None, k, tn), lambda j, i: (layer, 0, j))],
        out_specs=pl.BlockSpec((tm, tn), lambda j, i: (i, j)),
        out_shape=jax.ShapeDtypeStruct((m, n), F32),
        scratch_shapes=[pltpu.VMEM((k, tn), BF16)],
        compiler_params=_params("parallel", "arbitrary"),
        name="in_proj",
    )(a, w_all)


def _outproj_kernel(oa_ref, ob_ref, wa_ref, wb_ref, x_ref, post_ref, pre_ref, xo_ref, hn_ref):
    for rows in _row_parts(x_ref.shape[0], 256):
        mix = _dot(oa_ref[rows, :], wa_ref[...]) + _dot(ob_ref[rows, :], wb_ref[...])
        xn = x_ref[rows, :] + _rms(mix, post_ref[...])
        xo_ref[rows, :] = xn
        hn_ref[rows, :] = _rms(xn, pre_ref[...]).astype(hn_ref.dtype)


def _outproj(oa, ob, w_out, x, post_w, pre_w, tm=512):
    m, d = x.shape
    ka, kb = oa.shape[1], ob.shape[1]
    row = lambda i: (i, 0)
    fixed = lambda i: (0, 0)
    return pl.pallas_call(
        _outproj_kernel,
        grid=(m // tm,),
        in_specs=[pl.BlockSpec((tm, ka), row), pl.BlockSpec((tm, kb), row),
                  pl.BlockSpec((ka, d), fixed), pl.BlockSpec((kb, d), lambda i: (1, 0)),
                  pl.BlockSpec((tm, d), row), pl.BlockSpec((1, d), fixed), pl.BlockSpec((1, d), fixed)],
        out_specs=[pl.BlockSpec((tm, d), row), pl.BlockSpec((tm, d), row)],
        out_shape=[jax.ShapeDtypeStruct((m, d), F32), jax.ShapeDtypeStruct((m, d), BF16)],
        compiler_params=_params("parallel"),
        name="out_proj",
    )(oa, ob, w_out, w_out, x, post_w.reshape(1, d), pre_w.reshape(1, d))


def _ffn_up_kernel(h_ref, wg_ref, wu_ref, act_ref, wgb_ref, wub_ref):
    @pl.when(pl.program_id(1) == 0)
    def _():
        wgb_ref[...] = wg_ref[...].astype(BF16)
        wub_ref[...] = wu_ref[...].astype(BF16)

    h = h_ref[...]
    act_ref[...] = (_silu(_dot(h, wgb_ref[...])) * _dot(h, wub_ref[...])).astype(act_ref.dtype)


def _ffn_up(hn, w_in_all, layer, tm=1024, tf=512):
    m, d = hn.shape
    tm = min(tm, m)
    nf = D_FF // tf
    return pl.pallas_call(
        _ffn_up_kernel,
        grid=(nf, m // tm),
        in_specs=[pl.BlockSpec((tm, d), lambda j, i: (i, 0)),
                  pl.BlockSpec((None, d, tf), lambda j, i: (layer, 0, j)),
                  pl.BlockSpec((None, d, tf), lambda j, i: (layer, 0, j + nf))],
        out_specs=pl.BlockSpec((tm, tf), lambda j, i: (i, j)),
        out_shape=jax.ShapeDtypeStruct((m, D_FF), BF16),
        scratch_shapes=[pltpu.VMEM((d, tf), BF16), pltpu.VMEM((d, tf), BF16)],
        compiler_params=_params("parallel", "arbitrary"),
        name="ffn_up",
    )(hn, w_in_all, w_in_all)


def _ffn_down_kernel(act_ref, wo_ref, x_ref, post_ref, pre_ref, xo_ref, hn_ref):
    xn = x_ref[...] + _rms(_dot(act_ref[...], wo_ref[...]), post_ref[...])
    xo_ref[...] = xn
    hn_ref[...] = _rms(xn, pre_ref[...]).astype(hn_ref.dtype)


def _ffn_down(act, w_out, x, post_w, pre_w, tm=256):
    m, d = x.shape
    tm = min(tm, m)
    row = lambda i: (i, 0)
    fixed = lambda i: (0, 0)
    return pl.pallas_call(
        _ffn_down_kernel,
        grid=(m // tm,),
        in_specs=[pl.BlockSpec((tm, D_FF), row),
                  pl.BlockSpec((D_FF, d), fixed, pipeline_mode=pl.Buffered(1)),
                  pl.BlockSpec((tm, d), row), pl.BlockSpec((1, d), fixed), pl.BlockSpec((1, d), fixed)],
        out_specs=[pl.BlockSpec((tm, d), row), pl.BlockSpec((tm, d), row)],
        out_shape=[jax.ShapeDtypeStruct((m, d), F32), jax.ShapeDtypeStruct((m, d), BF16)],
        compiler_params=_params("parallel"),
        name="ffn_down",
    )(act, w_out, x, post_w.reshape(1, d), pre_w.reshape(1, d))


def _hgrn_kernel(q_ref, f_ref, i_ref, g_ref, lbraw_ref, nw_ref, s0_ref, o_ref, sout_ref, st_ref,
                 *, layer, sc, n_sub, bb):
    ci = pl.program_id(1)

    @pl.when(ci == 0)
    def _():
        for b in range(bb):
            for h in range(H_A):
                st_ref[b, h] = s0_ref[b, h].T

    raw = lbraw_ref[...]
    e = jnp.exp(raw - jnp.max(raw, axis=0, keepdims=True))
    pr = e / jnp.sum(e, axis=0, keepdims=True)
    lb_all = jnp.zeros((1, D_A), F32)
    for r in range(1, layer + 1):
        lb_all = lb_all + pr[r:r + 1, :]
    nw_all = nw_ref[...]
    tri = _lower_tri(sc, strict=False).astype(F32)
    row = lax.broadcasted_iota(jnp.int32, (sc, LANES), 0)
    sub = 8
    lane8 = lax.broadcasted_iota(jnp.int32, (sub, LANES), 1)

    oml = 1.0 - lb_all
    heads = [slice(h * DK_A, (h + 1) * DK_A) for h in range(H_A)]

    def body(idx, carry):
        b = idx // n_sub
        rows = pl.ds(pl.multiple_of((idx % n_sub) * sc, sc), sc)
        zf = f_ref[b, rows, :]
        iv = i_ref[b, rows, :]
        fg = lb_all + oml * _sigmoid(zf)
        logf = jnp.log(jnp.maximum(fg, TINY))
        kk = oml * _sigmoid(-zf)
        qs = _silu(q_ref[b, rows, :])
        bc = _dot_mask_lhs(tri, logf)
        blast = bc[sc - 1:sc, :]
        qe = (qs * jnp.exp(bc)).astype(BF16)
        kh = (kk * jnp.exp(blast - bc)).astype(BF16)
        ivb = iv.astype(BF16)
        keep = jnp.exp(blast)
        sts = [st_ref[b, h] for h in range(H_A)]
        o_inter = [_dot_nt(qe[:, sl], sts[h].astype(BF16)) for h, sl in enumerate(heads)]
        upd = [_dot_tn(ivb[:, sl], kh[:, sl]) for sl in heads]
        for h, sl in enumerate(heads):
            st_ref[b, h] = sts[h] * keep[:, sl] + upd[h]
        atts = []
        for h, sl in enumerate(heads):
            bc_h, kk_h, qs_h = bc[:, sl], kk[:, sl], qs[:, sl]
            att_t = [jnp.zeros((sub, LANES), F32)] * (sc // sub)
            for t in range(sc):
                n = (t // sub + 1) * sub
                dec = jnp.exp(bc_h[t:t + 1, :] - bc_h[:n])
                pair = jnp.where(row[:n] <= t, kk_h[:n] * dec * qs_h[t:t + 1, :], 0.0)
                a_col = jnp.sum(pair, axis=-1, keepdims=True)
                for k in range(n // sub):
                    att_t[k] = jnp.where(lane8 == t, a_col[k * sub:(k + 1) * sub], att_t[k])
            atts.append(jnp.concatenate(att_t, axis=0)[:, :sc].astype(BF16))
        o_intra = [_dot_tn(atts[h], ivb[:, sl]) for h, sl in enumerate(heads)]
        outs = []
        for h in range(H_A):
            o = o_intra[h] + o_inter[h]
            outs.append(o * lax.rsqrt(jnp.mean(o * o, axis=-1, keepdims=True) + RMS_EPS))
        o = jnp.concatenate(outs, axis=-1) * nw_all
        o_ref[b, rows, :] = (o * _silu(g_ref[b, rows, :])).astype(o_ref.dtype)
        return carry

    lax.fori_loop(0, bb * n_sub, body, 0, unroll=math.gcd(bb * n_sub, 4))

    @pl.when(ci == pl.num_programs(1) - 1)
    def _():
        for b in range(bb):
            for h in range(H_A):
                sout_ref[b, h] = st_ref[b, h].T


def _stacked_state_call(kernel_fn, prev, state_out_index, in_specs, args, **kw):
    if prev is None:
        return pl.pallas_call(kernel_fn, in_specs=in_specs, **kw)(*args)

    def with_prev(prev_ref, *refs):
        del prev_ref
        kernel_fn(*refs)

    return pl.pallas_call(with_prev, in_specs=[pl.BlockSpec(memory_space=pl.ANY)] + in_specs,
                          input_output_aliases={0: state_out_index}, **kw)(prev, *args)


def _hgrn(z_a, lb_raw, norm_w, s_all, prev, layer, bb, tc, sc):
    bsz, t, _ = z_a.shape
    col = lambda k: (lambda b, c: (b, c, k))
    fixed2 = lambda b, c: (0, 0)
    state = pl.BlockSpec((None, bb, H_A, DK_A, DV_A), lambda b, c: (layer, b, 0, 0, 0))
    return _stacked_state_call(
        functools.partial(_hgrn_kernel, layer=layer, sc=sc, n_sub=tc // sc, bb=bb), prev, 1,
        [pl.BlockSpec((bb, tc, D_A), col(0)), pl.BlockSpec((bb, tc, D_A), col(1)),
         pl.BlockSpec((bb, tc, D_A), col(2)), pl.BlockSpec((bb, tc, D_A), col(3)),
         pl.BlockSpec((N_EVEN, D_A), fixed2), pl.BlockSpec((1, D_A), fixed2), state],
        (z_a, z_a, z_a, z_a, lb_raw, norm_w.reshape(1, D_A), s_all),
        grid=(bsz // bb, t // tc),
        out_specs=[pl.BlockSpec((bb, tc, D_A), lambda b, c: (b, c, 0)), state],
        out_shape=[jax.ShapeDtypeStruct((bsz, t, D_A), BF16),
                   jax.ShapeDtypeStruct(s_all.shape, F32)],
        scratch_shapes=[pltpu.VMEM((bb, H_A, DV_A, DK_A), F32)],
        compiler_params=_params("parallel", "arbitrary"),
        name="hgrn2",
    )


def _head_sums(x):
    rows = x.shape[0]
    seg = (lax.broadcasted_iota(jnp.int32, (LANES, LANES), 0) // N_B
           == lax.broadcasted_iota(jnp.int32, (LANES, LANES), 1) // N_B).astype(F32)
    n_tiles = D_B // LANES
    xs = jnp.concatenate([x[:, i * LANES:(i + 1) * LANES] for i in range(n_tiles)], axis=0)
    s = _dot_mask_rhs(xs, seg, pieces=2)
    return jnp.concatenate([s[i * rows:(i + 1) * rows] for i in range(n_tiles)], axis=1)


def _rwkv_kernel(z_ref, sh0_ref, s0_ref, mu_ref, w0_ref, w2_ref, a0_ref, a2_ref, g2_ref, kk_ref, ka_ref,
                 rk_ref, lnw_ref, lnb_ref, o_ref, sout_ref, shout_ref, s_ref, prev_ref, *, c, nb):
    ci = pl.program_id(1)
    rows = nb * c
    dz = D_RWKV_PROJ

    @pl.when(ci == 0)
    def _():
        s_ref[...] = s0_ref[...]
        prev_ref[...] = sh0_ref[...]

    blocks = [slice(b * c, (b + 1) * c) for b in range(nb)]
    last = [slice((b + 1) * c - 1, (b + 1) * c) for b in range(nb)]

    def per_block(rows_of):
        n = rows_of[0].shape[-1]
        blk = lax.broadcasted_iota(jnp.int32, (rows, n), 0) // c
        out = jnp.broadcast_to(rows_of[0], (rows, n))
        for b in range(1, nb):
            out = jnp.where(blk == b, rows_of[b], out)
        return out

    zb = z_ref[...].reshape(rows, dz)
    row = lax.broadcasted_iota(jnp.int32, (rows, dz), 0)
    prev = jnp.where(row % c == 0, per_block([prev_ref[b] for b in range(nb)]), pltpu.roll(zb, 1, axis=0))
    for b in range(nb):
        prev_ref[b] = zb[last[b], :]
    zs = zb + (prev - zb) * mu_ref[...]
    r = zs[:, 0:D_B]
    kb = zs[:, D_B:2 * D_B]
    v = zs[:, 2 * D_B:3 * D_B]
    o0 = 3 * D_B
    wl = zs[:, o0:o0 + R_W]
    al = zs[:, o0 + R_W:o0 + R_W + R_A]
    gl = zs[:, o0 + R_W + R_A:o0 + R_W + R_A + R_G]
    wlin = w0_ref[...] + _dot(jnp.tanh(wl).astype(BF16), w2_ref[...])
    w_log = -math.exp(-0.5) * _sigmoid(wlin)
    a = _sigmoid(a0_ref[...] + _dot(al.astype(BF16), a2_ref[...]))
    gate = _dot(_sigmoid(gl).astype(BF16), g2_ref[...])
    kkr = kb * kk_ref[...]
    kmod = kb * (1.0 + (a - 1.0) * ka_ref[...])

    ri = lax.broadcasted_iota(jnp.int32, (rows, rows), 0)
    cj = lax.broadcasted_iota(jnp.int32, (rows, rows), 1)
    tri = jnp.where(ri // c == cj // c, (ri >= cj).astype(F32), 0.0)
    bc = _dot_mask_lhs(tri, w_log)
    b_last = [bc[l, :] for l in last]
    blast = per_block(b_last)
    gam = jnp.exp(bc)
    inv_gam = jnp.exp(-bc)
    gam_prev = jnp.exp(bc - w_log)
    to_end = jnp.exp(blast - bc)
    gam_last = [jnp.exp(x) for x in b_last]
    kkn = kkr / jnp.maximum(jnp.sqrt(_head_sums(kkr * kkr)), 1e-12)
    beta = kkn * a
    xa = -kkn * gam_prev
    xr = r * gam
    yb = beta * inv_gam
    yk = kmod * inv_gam
    eb = beta * to_end
    ek = kmod * to_end
    lhs = [jnp.concatenate([xa[bl], xr[bl]], axis=0).astype(BF16) for bl in blocks]
    rhs = [jnp.concatenate([yb[bl], yk[bl]], axis=0).astype(BF16) for bl in blocks]
    end = [jnp.concatenate([eb[bl], ek[bl]], axis=0).astype(BF16) for bl in blocks]
    vb = [v[bl].astype(BF16) for bl in blocks]

    ri2 = lax.broadcasted_iota(jnp.int32, (2 * c, 2 * c), 0)
    cj2 = lax.broadcasted_iota(jnp.int32, (2 * c, 2 * c), 1)
    keep = (ri2 % c + ri2 // c) > (cj2 % c)
    n_steps = max(1, int(math.log2(c)))
    heads = [slice(h * N_B, (h + 1) * N_B) for h in range(H_B)]
    pairs = [(b, h) for b in range(nb) for h in range(H_B)]
    idx = range(len(pairs))
    s0 = [s_ref[b, h] for b, h in pairs]
    res = [_dot_nt(lhs[b][:, heads[h]], jnp.concatenate([rhs[b][:, heads[h]], s0[i].astype(BF16)], axis=0))
           for i, (b, h) in enumerate(pairs)]
    att = [jnp.where(keep, res[i][:, :2 * c], 0.0) for i in idx]
    from_s0 = [res[i][:, 2 * c:] for i in idx]
    u = [from_s0[i][:c] + _dot(att[i][:c, c:].astype(BF16), vb[b][:, heads[h]]) for i, (b, h) in enumerate(pairs)]
    a_ab = [att[i][:c, :c] for i in idx]
    tq = lax.broadcasted_iota(jnp.int32, (c, c), 0)
    sq = lax.broadcasted_iota(jnp.int32, (c, c), 1)

    def joins(m):
        return ((tq // m - sq // m) * 2 + (tq // m) % 2) == 3

    a_k = [jnp.where(joins(1), a_ab[i], 0.0) for i in idx]
    u = [u[i] + _dot(a_k[i].astype(BF16), u[i].astype(BF16)) for i in idx]
    tinv = [jnp.where(tq == sq, 1.0, a_k[i]) for i in idx]
    for level in range(1, n_steps):
        a_k = [jnp.where(joins(1 << level), a_ab[i], 0.0).astype(BF16) for i in idx]
        tb = [tinv[i].astype(BF16) for i in idx]
        if level + 1 < n_steps:
            ut = [jnp.concatenate([u[i], tinv[i]], axis=1) for i in idx]
            x = [_dot(a_k[i], ut[i].astype(BF16)) for i in idx]
            ut = [ut[i] + _dot(tb[i], x[i].astype(BF16)) for i in idx]
            u = [ut[i][:, :N_B] for i in idx]
            tinv = [ut[i][:, N_B:] for i in idx]
        else:
            x = [_dot(a_k[i], u[i].astype(BF16)) for i in idx]
            u = [u[i] + _dot(tb[i], x[i].astype(BF16)) for i in idx]
    uv = [jnp.concatenate([u[i].astype(BF16), vb[b][:, heads[h]]], axis=0) for i, (b, h) in enumerate(pairs)]
    ys = [from_s0[i][c:] + _dot(att[i][c:].astype(BF16), uv[i]) for i in idx]
    s_new = [_dot_tn(uv[i], end[b][:, heads[h]]) for i, (b, h) in enumerate(pairs)]
    for i, (b, h) in enumerate(pairs):
        s_ref[b, h] = s0[i] * gam_last[b][:, heads[h]] + s_new[i]

    y = jnp.concatenate([jnp.concatenate(ys[b * H_B:(b + 1) * H_B], axis=-1) for b in range(nb)], axis=0)
    yc = y - _head_sums(y) * (1.0 / N_B)
    var = _head_sums(yc * yc) * (1.0 / N_B)
    yn = yc * lax.rsqrt(var + GN_EPS) * lnw_ref[...] + lnb_ref[...]
    out = (yn + _head_sums(r * kmod * rk_ref[...]) * v) * gate
    o_ref[...] = out.reshape(nb, c, D_B).astype(o_ref.dtype)

    @pl.when(ci == pl.num_programs(1) - 1)
    def _():
        sout_ref[...] = s_ref[...]
        for b in range(nb):
            shout_ref[b] = zb[last[b], :]


def _rwkv(z_b, s_all, prev, shift0, p, j, c, nb):
    bsz, t, dz = z_b.shape
    vec = lambda x: x.reshape(1, -1)
    fixed2 = lambda b, ci: (0, 0)
    vspec = lambda n: pl.BlockSpec((1, n), fixed2)
    state = pl.BlockSpec((None, nb, H_B, N_B, N_B), lambda b, ci: (j, b, 0, 0, 0))
    shift = pl.BlockSpec((nb, 1, dz), lambda b, ci: (b, 0, 0))
    return _stacked_state_call(
        functools.partial(_rwkv_kernel, c=c, nb=nb), prev, 1,
        [pl.BlockSpec((nb, c, dz), lambda b, ci: (b, ci, 0)), shift, state,
         vspec(dz), vspec(D_B),
         pl.BlockSpec((R_W, D_B), fixed2), vspec(D_B),
         pl.BlockSpec((R_A, D_B), fixed2), pl.BlockSpec((R_G, D_B), fixed2),
         vspec(D_B), vspec(D_B), vspec(D_B), vspec(D_B), vspec(D_B)],
        (z_b, shift0.reshape(bsz, 1, dz), s_all, vec(p['rwkv_mu'][j]), vec(p['rwkv_w0'][j]),
         p['rwkv_w2'][j].astype(BF16), vec(p['rwkv_a0'][j]), p['rwkv_a2'][j].astype(BF16),
         p['rwkv_g2'][j].astype(BF16), vec(p['rwkv_k_k'][j]), vec(p['rwkv_k_a'][j]),
         vec(p['rwkv_r_k'][j]), vec(p['rwkv_ln_w'][j]), vec(p['rwkv_ln_b'][j])),
        grid=(bsz // nb, t // c),
        out_specs=[pl.BlockSpec((nb, c, D_B), lambda b, ci: (b, ci, 0)), state, shift],
        out_shape=[jax.ShapeDtypeStruct((bsz, t, D_B), BF16),
                   jax.ShapeDtypeStruct(s_all.shape, F32),
                   jax.ShapeDtypeStruct((bsz, 1, dz), F32)],
        scratch_shapes=[pltpu.VMEM((nb, H_B, N_B, N_B), F32), pltpu.VMEM((nb, 1, dz), F32)],
        compiler_params=_params("parallel", "arbitrary"),
        name="rwkv7",
    )


def _rope_kernel(ang_ref, sin_ref, cos_ref, *, tt, pos0):
    row = lax.broadcasted_iota(jnp.int32, (tt, DK_C), 0)
    ang = (pos0 + pl.program_id(0) * tt + row).astype(F32) * ang_ref[...]
    sin_ref[...] = jnp.sin(ang)
    cos_ref[...] = jnp.cos(ang)


def _rope_table(pos0, t):
    angle = 1.0 / (10000.0 ** jnp.linspace(0.0, 1.0, DK_C // 2, dtype=F32))
    angle = jnp.repeat(angle, 2).reshape(1, DK_C)
    tt = min(t, 256)
    out = pl.BlockSpec((tt, DK_C), lambda i: (i, 0))
    return pl.pallas_call(
        functools.partial(_rope_kernel, tt=tt, pos0=pos0),
        grid=(t // tt,),
        in_specs=[pl.BlockSpec((1, DK_C), lambda i: (0, 0))],
        out_specs=[out, out],
        out_shape=[jax.ShapeDtypeStruct((t, DK_C), F32)] * 2,
        compiler_params=_params("parallel"),
        name="rope_table",
    )(angle)


def _ret_kernel(q_ref, k_ref, v_ref, g_ref, sin_ref, cos_ref, s0_ref, o_ref, sout_ref, s_ref, *, c, nb):
    ci = pl.program_id(1)

    @pl.when(ci == 0)
    def _():
        s_ref[...] = s0_ref[...]

    sin = jnp.concatenate([sin_ref[...]] * H_C, axis=1)
    cos = jnp.concatenate([cos_ref[...]] * H_C, axis=1)
    even = (lax.broadcasted_iota(jnp.int32, (c, D_C), 1) % 2) == 0

    def rope(x):
        rot = jnp.where(even, -pltpu.roll(x, D_C - 1, axis=1), pltpu.roll(x, 1, axis=1))
        return x * cos + rot * sin

    ti = lax.broadcasted_iota(jnp.int32, (c, c), 0)
    si = lax.broadcasted_iota(jnp.int32, (c, c), 1)
    causal = ti >= si
    dist = jnp.where(causal, (ti - si).astype(F32), 0.0)
    tcol = lax.broadcasted_iota(jnp.int32, (c, 1), 0).astype(F32)
    heads = [slice(h * DK_C, (h + 1) * DK_C) for h in range(H_C)]
    log_gamma = [math.log1p(-2.0 ** (-5.0 - h)) for h in range(H_C)]
    dec = [jnp.where(causal, jnp.exp(dist * lg), 0.0) for lg in log_gamma]
    from_start = [jnp.exp((tcol + 1.0) * lg) for lg in log_gamma]
    to_end = [jnp.exp((float(c) - 1.0 - tcol) * lg) for lg in log_gamma]
    keep = [math.exp(float(c) * lg) for lg in log_gamma]

    pairs = [(b, h) for b in range(nb) for h in range(H_C)]
    qs = [rope(q_ref[b]) for b in range(nb)]
    ks = [rope(k_ref[b]) * (DK_C ** -0.5) for b in range(nb)]
    vb = [v_ref[b].astype(BF16) for b in range(nb)]
    qb = [q.astype(BF16) for q in qs]
    kb = [k.astype(BF16) for k in ks]
    s_old = [s_ref[b, h] for b, h in pairs]
    att = [_dot_nt(qb[b][:, heads[h]], kb[b][:, heads[h]]) for b, h in pairs]
    o_st = [_dot((qs[b][:, heads[h]] * from_start[h]).astype(BF16), s_old[i].astype(BF16))
            for i, (b, h) in enumerate(pairs)]
    o_in = [_dot((att[i] * dec[h]).astype(BF16), vb[b][:, heads[h]]) for i, (b, h) in enumerate(pairs)]
    upd = [_dot_tn((ks[b][:, heads[h]] * to_end[h]).astype(BF16), vb[b][:, heads[h]]) for b, h in pairs]
    s_new = [keep[h] * s_old[i] + upd[i] for i, (b, h) in enumerate(pairs)]
    for i, (b, h) in enumerate(pairs):
        s_ref[b, h] = s_new[i]
    for b in range(nb):
        outs = []
        for h in range(H_C):
            o = o_in[b * H_C + h] + o_st[b * H_C + h]
            outs.append(o * lax.rsqrt(jnp.mean(o * o, axis=-1, keepdims=True) + RMS_EPS))
        o_ref[b] = (jnp.concatenate(outs, axis=-1) * _silu(g_ref[b])).astype(o_ref.dtype)

    @pl.when(ci == pl.num_programs(1) - 1)
    def _():
        for i, (b, h) in enumerate(pairs):
            sout_ref[b, h] = s_new[i]


def _retention(z, s_all, prev, j, sin, cos, c, nb):
    bsz, t, _ = z.shape
    col = lambda k: (lambda b, ci: (b, ci, k))
    state = pl.BlockSpec((None, nb, H_C, DK_C, DV_C), lambda b, ci: (j, b, 0, 0, 0))
    table = pl.BlockSpec((c, DK_C), lambda b, ci: (ci, 0))
    return _stacked_state_call(
        functools.partial(_ret_kernel, c=c, nb=nb), prev, 1,
        [pl.BlockSpec((nb, c, D_C), col(0)), pl.BlockSpec((nb, c, D_C), col(1)),
         pl.BlockSpec((nb, c, D_C), col(2)), pl.BlockSpec((nb, c, D_C), col(3)),
         table, table, state],
        (z, z, z, z, sin, cos, s_all),
        grid=(bsz // nb, t // c),
        out_specs=[pl.BlockSpec((nb, c, D_C), lambda b, ci: (b, ci, 0)), state],
        out_shape=[jax.ShapeDtypeStruct((bsz, t, D_C), BF16), jax.ShapeDtypeStruct(s_all.shape, F32)],
        scratch_shapes=[pltpu.VMEM((nb, H_C, DK_C, DV_C), F32)],
        compiler_params=_params("parallel", "arbitrary"),
        name="retention",
    )


def _rglru_kernel(y_ref, x_ref, conv0_ref, h0_ref, cw_ref, cb_ref, wa_ref, wx_ref, ba_ref, bx_ref, lam_ref,
                  o_ref, convout_ref, hout_ref, tail_ref, h_ref, *, tc, nb, pos0):
    ci = pl.program_id(1)
    rows = nb * tc
    sub = 8

    @pl.when(ci == 0)
    def _():
        tail_ref[...] = conv0_ref[...]
        h_ref[...] = h0_ref[...]

    x = x_ref[...].reshape(rows, D_D)
    row = lax.broadcasted_iota(jnp.int32, (rows, D_D), 0)
    t_in = row % tc
    tails = [tail_ref[b] for b in range(nb)]

    def shifted(d):
        r = pltpu.roll(x, d, axis=0)
        for m in range(d):
            k = CONV_W - 1 - d + m
            for b in range(nb):
                r = jnp.where(row == b * tc + m, tails[b][k:k + 1, :], r)
        return r

    cw = cw_ref[...]
    xc = cb_ref[...] + cw[CONV_W - 1:CONV_W, :] * x
    for d in range(1, CONV_W):
        xc = xc + cw[CONV_W - 1 - d:CONV_W - d, :] * shifted(d)
    new_tails = [x[(b + 1) * tc - (CONV_W - 1):(b + 1) * tc, :] for b in range(nb)]
    for b in range(nb):
        tail_ref[b] = new_tails[b]

    ra, rx = [], []
    for hd in range(H_D):
        xh = xc[:, hd * BW_D:(hd + 1) * BW_D].astype(BF16)
        ra.append(_dot(xh, wa_ref[hd]))
        rx.append(_dot(xh, wx_ref[hd]))
    rg = _sigmoid(jnp.concatenate(ra, axis=-1) + ba_ref[...])
    ig = _sigmoid(jnp.concatenate(rx, axis=-1) + bx_ref[...])
    nlam = -lam_ref[...]
    softplus = jnp.maximum(nlam, 0.0) + jnp.log1p(jnp.exp(-jnp.abs(nlam)))
    log_a = -LRU_C * rg * softplus
    a = jnp.exp(log_a)
    mult = jnp.sqrt(-jnp.tanh(log_a) * (a * a + 1.0))
    mult = jnp.where(pos0 + ci * tc + t_in == 0, 1.0, mult)
    bt = mult * ig * xc

    in_sub = row % sub
    d = 1
    while d < sub:
        m = in_sub >= d
        bt = jnp.where(m, a * pltpu.roll(bt, d, axis=0) + bt, bt)
        a = jnp.where(m, a * pltpu.roll(a, d, axis=0), a)
        d *= 2
    hs = []
    for b in range(nb):
        carry = h_ref[b]
        for k in range(tc // sub):
            sl = slice(b * tc + k * sub, b * tc + (k + 1) * sub)
            h = bt[sl] + a[sl] * carry
            hs.append(h)
            carry = h[sub - 1:sub, :]
        h_ref[b] = carry
    hseq = jnp.concatenate(hs, axis=0)
    y = y_ref[...].reshape(rows, D_D)
    gelu = 0.5 * y * (1.0 + jnp.tanh(math.sqrt(2.0 / math.pi) * (y + 0.044715 * (y * y * y))))
    o_ref[...] = (hseq * gelu).reshape(nb, tc, D_D).astype(o_ref.dtype)

    @pl.when(ci == pl.num_programs(1) - 1)
    def _():
        for b in range(nb):
            convout_ref[b] = new_tails[b]
        hout_ref[...] = h_ref[...]


def _rglru(z, conv0, h0, p, j, pos0, tc, nb):
    bsz, t, _ = z.shape
    vec = lambda x: x.reshape(1, -1)
    fixed2 = lambda b, ci: (0, 0)
    fixed3 = lambda b, ci: (0, 0, 0)
    vspec = pl.BlockSpec((1, D_D), fixed2)
    ycol = 4 * D_C // D_D
    conv = pl.BlockSpec((nb, CONV_W - 1, D_D), lambda b, ci: (b, 0, 0))
    hid = pl.BlockSpec((nb, 1, D_D), lambda b, ci: (b, 0, 0))
    return pl.pallas_call(
        functools.partial(_rglru_kernel, tc=tc, nb=nb, pos0=pos0),
        grid=(bsz // nb, t // tc),
        in_specs=[pl.BlockSpec((nb, tc, D_D), lambda b, ci: (b, ci, ycol)),
                  pl.BlockSpec((nb, tc, D_D), lambda b, ci: (b, ci, ycol + 1)),
                  conv, hid,
                  pl.BlockSpec((CONV_W, D_D), fixed2), vspec,
                  pl.BlockSpec((H_D, BW_D, BW_D), fixed3), pl.BlockSpec((H_D, BW_D, BW_D), fixed3),
                  vspec, vspec, vspec],
        out_specs=[pl.BlockSpec((nb, tc, D_D), lambda b, ci: (b, ci, 0)), conv, hid],
        out_shape=[jax.ShapeDtypeStruct((bsz, t, D_D), BF16),
                   jax.ShapeDtypeStruct((bsz, CONV_W - 1, D_D), F32),
                   jax.ShapeDtypeStruct((bsz, 1, D_D), F32)],
        scratch_shapes=[pltpu.VMEM((nb, CONV_W - 1, D_D), F32), pltpu.VMEM((nb, 1, D_D), F32)],
        compiler_params=_params("parallel", "arbitrary"),
        name="rglru",
    )(z, z, conv0, h0.reshape(bsz, 1, D_D), p['conv_w'][j], vec(p['conv_b'][j]),
      p['rglru_wa'][j].astype(BF16), p['rglru_wx'][j].astype(BF16),
      vec(p['rglru_ba'][j]), vec(p['rglru_bx'][j]), vec(p['rglru_lambda'][j]))


def _tiles(bsz, t):
    if t >= 256:
        return dict(hgrn=(1, 256, 16), rwkv=(64, math.gcd(bsz, 2)), ret=(256, 1), lru=(256, 1))
    nb = math.gcd(bsz, 4)
    return dict(hgrn=(nb, t, min(t, 8)), rwkv=(t, nb), ret=(t, nb), lru=(t, nb))


def _run_trunk(x, pos0, st_hgrn, st_rwkv, st_shift, st_ret, st_conv, st_h, p, wb):
    bsz, t, d = x.shape
    m = bsz * t
    tl = _tiles(bsz, t)
    x2 = x.reshape(m, d)
    hn = _rmsnorm_bf16(x2, p['norm_mix_pre'][0], tm=min(512, m))
    n_hgrn, n_rwkv, n_ret = None, None, None
    n_shift, n_conv, n_h = [], [], []
    sin, cos = _rope_table(pos0, t)
    for l in range(DEPTH):
        j = l // 2
        if l % 2 == 0:
            z_a = _in_proj(hn, p['w_in_even'], j, 4 * D_A, tn=1024, tm=1024).reshape(bsz, t, 4 * D_A)
            z_b = _in_proj(hn, wb['w_in_rwkv'], j, D_RWKV_PROJ, tn=D_RWKV_PROJ // 2, tm=512)
            z_b = z_b.reshape(bsz, t, D_RWKV_PROJ)
            bb, tc, sc = tl['hgrn']
            o_a, n_hgrn = _hgrn(z_a, p['hgrn_lb_raw'], p['hgrn_norm_w'][j], st_hgrn, n_hgrn, j, bb, tc, sc)
            o_b, n_rwkv, sh = _rwkv(z_b, st_rwkv, n_rwkv, st_shift[j], p, j, *tl['rwkv'])
            n_shift.append(sh.reshape(bsz, D_RWKV_PROJ))
            w_out = wb['w_out_even'][j]
        else:
            z = _in_proj(hn, p['w_in_odd'], j, 4 * D_C + 2 * D_D, tn=1024, tm=1024)
            z = z.reshape(bsz, t, 4 * D_C + 2 * D_D)
            o_a, n_ret = _retention(z, st_ret, n_ret, j, sin, cos, *tl['ret'])
            o_b, scv, shh = _rglru(z, st_conv[j], st_h[j], p, j, pos0, *tl['lru'])
            n_conv.append(scv)
            n_h.append(shh.reshape(bsz, D_D))
            w_out = wb['w_out_odd'][j]
        x2, hn = _outproj(o_a.reshape(m, -1), o_b.reshape(m, -1), w_out, x2,
                          p['norm_mix_post'][l], p['norm_ffn_pre'][l], tm=min(512, m))
        next_pre = p['norm_mix_pre'][(l + 1) % DEPTH]
        act = _ffn_up(hn, p['w_ffn_in'], l)
        x2, hn = _ffn_down(act, wb['w_ffn_out'][l], x2, p['norm_ffn_post'][l], next_pre)
    return x2.reshape(bsz, t, d), (n_hgrn, n_rwkv, jnp.stack(n_shift),
                                   n_ret, jnp.stack(n_conv), jnp.stack(n_h))


def _zero_state(s, batch):
    return jnp.zeros((s.shape[0], batch) + s.shape[2:], s.dtype)


def kernel(x_prompt, x_sample, state_hgrn, state_rwkv, state_rwkv_shift, state_ret, state_rglru_conv, state_rglru_h, norm_mix_pre, norm_mix_post, norm_ffn_pre, norm_ffn_post, w_in_even, w_out_even, hgrn_lb_raw, hgrn_norm_w, rwkv_mu, rwkv_w0, rwkv_w2, rwkv_a0, rwkv_a2, rwkv_g2, rwkv_k_k, rwkv_k_a, rwkv_r_k, rwkv_ln_w, rwkv_ln_b, w_in_odd, w_out_odd, conv_w, conv_b, rglru_wa, rglru_ba, rglru_wx, rglru_bx, rglru_lambda, w_ffn_in, w_ffn_out):
    p = dict(norm_mix_pre=norm_mix_pre, norm_mix_post=norm_mix_post, norm_ffn_pre=norm_ffn_pre,
             norm_ffn_post=norm_ffn_post, hgrn_lb_raw=hgrn_lb_raw, hgrn_norm_w=hgrn_norm_w,
             rwkv_mu=rwkv_mu, rwkv_w0=rwkv_w0, rwkv_w2=rwkv_w2, rwkv_a0=rwkv_a0, rwkv_a2=rwkv_a2,
             rwkv_g2=rwkv_g2, rwkv_k_k=rwkv_k_k, rwkv_k_a=rwkv_k_a, rwkv_r_k=rwkv_r_k,
             rwkv_ln_w=rwkv_ln_w, rwkv_ln_b=rwkv_ln_b, conv_w=conv_w, conv_b=conv_b,
             rglru_wa=rglru_wa, rglru_ba=rglru_ba, rglru_wx=rglru_wx, rglru_bx=rglru_bx,
             rglru_lambda=rglru_lambda, w_in_even=w_in_even, w_in_odd=w_in_odd, w_ffn_in=w_ffn_in)
    wb = dict(w_in_rwkv=w_in_even[:, :, 4 * D_A:], w_out_even=w_out_even.astype(BF16),
              w_out_odd=w_out_odd.astype(BF16), w_ffn_out=w_ffn_out.astype(BF16))
    bp = x_prompt.shape[0]
    y_prompt, st_p = _run_trunk(x_prompt, 0,
                                _zero_state(state_hgrn, bp), _zero_state(state_rwkv, bp),
                                _zero_state(state_rwkv_shift, bp), _zero_state(state_ret, bp),
                                _zero_state(state_rglru_conv, bp), _zero_state(state_rglru_h, bp), p, wb)
    y_sample, st_s = _run_trunk(x_sample, PAST_LEN, state_hgrn, state_rwkv, state_rwkv_shift,
                                state_ret, state_rglru_conv, state_rglru_h, p, wb)
    hgrn_p, rwkv_p, shift_p, ret_p, conv_p, h_p = st_p
    hgrn_s, rwkv_s, shift_s, ret_s, conv_s, h_s = st_s
    return (y_prompt, y_sample, hgrn_p, hgrn_s, rwkv_p, rwkv_s, shift_p, shift_s,
            ret_p, ret_s, conv_p, conv_s, h_p, h_s)
```

```python
import functools
import math

import jax
import jax.numpy as jnp
from jax import lax
from jax.experimental import pallas as pl
from jax.experimental.pallas import tpu as pltpu

F32 = jnp.float32
BF16 = jnp.bfloat16

D_MODEL = 2048
DEPTH = 4
N_EVEN = (DEPTH + 1) // 2
N_ODD = DEPTH // 2
D_A = D_MODEL // 2
H_A = 8
DK_A = 128
DV_A = 128
D_B = D_MODEL // 2
N_B = 64
H_B = D_B // N_B
R_W = 64
R_A = 64
R_G = 128
D_RWKV_PROJ = 3 * D_B + R_W + R_A + R_G
D_C = D_MODEL // 2
H_C = 4
DK_C = D_C // H_C
DV_C = D_C // H_C
D_D = D_MODEL // 2
H_D = 4
BW_D = D_D // H_D
CONV_W = 4
LRU_C = 8.0
D_FF = ((8 * D_MODEL + 3 * 256 - 1) // (3 * 256)) * 256
RMS_EPS = 1e-6
GN_EPS = 64e-5
TINY = 1e-30
PAST_LEN = 16384

VMEM_LIMIT_BYTES = 60 * 1024 * 1024
LANES = 128


def _params(*sem):
    return pltpu.CompilerParams(dimension_semantics=sem, vmem_limit_bytes=VMEM_LIMIT_BYTES)


def _rms(x, w):
    return x * lax.rsqrt(jnp.mean(x * x, axis=-1, keepdims=True) + RMS_EPS) * w


def _sigmoid(x):
    return jax.nn.sigmoid(x)


def _silu(x):
    return x * jax.nn.sigmoid(x)


def _dot(a, b):
    return jnp.dot(a, b, preferred_element_type=F32)


def _dot_nt(a, b):
    return lax.dot_general(a, b, (((1,), (1,)), ((), ())), preferred_element_type=F32)


def _dot_tn(a, b):
    return lax.dot_general(a, b, (((0,), (0,)), ((), ())), preferred_element_type=F32)


def _split(x, pieces):
    out = []
    for _ in range(pieces - 1):
        part = x.astype(BF16)
        out.append(part)
        x = x - part.astype(F32)
    out.append(x.astype(BF16))
    return out


def _dot_mask_lhs(mask, x):
    m = mask.astype(BF16)
    return sum(_dot(m, part) for part in _split(x, 3))


def _dot_mask_rhs(x, mask, pieces):
    m = mask.astype(BF16)
    return sum(_dot(part, m) for part in _split(x, pieces))


def _row_parts(n, size):
    size = min(size, n)
    return [slice(i, i + size) for i in range(0, n, size)]


def _lower_tri(n, strict):
    r = lax.broadcasted_iota(jnp.int32, (n, n), 0)
    c = lax.broadcasted_iota(jnp.int32, (n, n), 1)
    return (r > c) if strict else (r >= c)


def _rmsnorm_kernel(x_ref, w_ref, o_ref):
    o_ref[...] = _rms(x_ref[...], w_ref[...]).astype(o_ref.dtype)


def _rmsnorm_bf16(x, w, tm=512):
    m, d = x.shape
    return pl.pallas_call(
        _rmsnorm_kernel,
        grid=(m // tm,),
        in_specs=[pl.BlockSpec((tm, d), lambda i: (i, 0)), pl.BlockSpec((1, d), lambda i: (0, 0))],
        out_specs=pl.BlockSpec((tm, d), lambda i: (i, 0)),
        out_shape=jax.ShapeDtypeStruct((m, d), BF16),
        compiler_params=_params("parallel"),
        name="rmsnorm",
    )(x, w.reshape(1, d))


def _in_proj_kernel(a_ref, w_ref, o_ref, wb_ref):
    @pl.when(pl.program_id(1) == 0)
    def _():
        wb_ref[...] = w_ref[...].astype(BF16)

    o_ref[...] = _dot(a_ref[...], wb_ref[...])


def _in_proj(a, w_all, layer, n, tn, tm):
    m, k = a.shape
    tm = min(tm, m)
    return pl.pallas_call(
        _in_proj_kernel,
        grid=(n // tn, m // tm),
        in_specs=[pl.BlockSpec((tm, k), lambda j, i: (i, 0)),
                  pl.BlockSpec((None, k, tn), lambda j, i: (layer, 0, j))],
        out_specs=pl.BlockSpec((tm, tn), lambda j, i: (i, j)),
        out_shape=jax.ShapeDtypeStruct((m, n), F32),
        scratch_shapes=[pltpu.VMEM((k, tn), BF16)],
        compiler_params=_params("parallel", "arbitrary"),
        name="in_proj",
    )(a, w_all)


def _outproj_kernel(oa_ref, ob_ref, wa_ref, wb_ref, x_ref, post_ref, pre_ref, xo_ref, hn_ref):
    for rows in _row_parts(x_ref.shape[0], 256):
        mix = _dot(oa_ref[rows, :], wa_ref[...]) + _dot(ob_ref[rows, :], wb_ref[...])
        xn = x_ref[rows, :] + _rms(mix, post_ref[...])
        xo_ref[rows, :] = xn
        hn_ref[rows, :] = _rms(xn, pre_ref[...]).astype(hn_ref.dtype)


def _outproj(oa, ob, w_out, x, post_w, pre_w, tm=512):
    m, d = x.shape
    ka, kb = oa.shape[1], ob.shape[1]
    row = lambda i: (i, 0)
    fixed = lambda i: (0, 0)
    return pl.pallas_call(
        _outproj_kernel,
        grid=(m // tm,),
        in_specs=[pl.BlockSpec((tm, ka), row), pl.BlockSpec((tm, kb), row),
                  pl.BlockSpec((ka, d), fixed), pl.BlockSpec((kb, d), lambda i: (1, 0)),
                  pl.BlockSpec((tm, d), row), pl.BlockSpec((1, d), fixed), pl.BlockSpec((1, d), fixed)],
        out_specs=[pl.BlockSpec((tm, d), row), pl.BlockSpec((tm, d), row)],
        out_shape=[jax.ShapeDtypeStruct((m, d), F32), jax.ShapeDtypeStruct((m, d), BF16)],
        compiler_params=_params("parallel"),
        name="out_proj",
    )(oa, ob, w_out, w_out, x, post_w.reshape(1, d), pre_w.reshape(1, d))


def _ffn_up_kernel(h_ref, wg_ref, wu_ref, act_ref, wgb_ref, wub_ref):
    @pl.when(pl.program_id(1) == 0)
    def _():
        wgb_ref[...] = wg_ref[...].astype(BF16)
        wub_ref[...] = wu_ref[...].astype(BF16)

    h = h_ref[...]
    act_ref[...] = (_silu(_dot(h, wgb_ref[...])) * _dot(h, wub_ref[...])).astype(act_ref.dtype)


def _ffn_up(hn, w_in_all, layer, tm=1024, tf=512):
    m, d = hn.shape
    tm = min(tm, m)
    nf = D_FF // tf
    return pl.pallas_call(
        _ffn_up_kernel,
        grid=(nf, m // tm),
        in_specs=[pl.BlockSpec((tm, d), lambda j, i: (i, 0)),
                  pl.BlockSpec((None, d, tf), lambda j, i: (layer, 0, j)),
                  pl.BlockSpec((None, d, tf), lambda j, i: (layer, 0, j + nf))],
        out_specs=pl.BlockSpec((tm, tf), lambda j, i: (i, j)),
        out_shape=jax.ShapeDtypeStruct((m, D_FF), BF16),
        scratch_shapes=[pltpu.VMEM((d, tf), BF16), pltpu.VMEM((d, tf), BF16)],
        compiler_params=_params("parallel", "arbitrary"),
        name="ffn_up",
    )(hn, w_in_all, w_in_all)


def _ffn_down_kernel(act_ref, wo_ref, x_ref, post_ref, pre_ref, xo_ref, hn_ref):
    for rows in _row_parts(x_ref.shape[0], 256):
        xn = x_ref[rows, :] + _rms(_dot(act_ref[rows, :], wo_ref[...]), post_ref[...])
        xo_ref[rows, :] = xn
        hn_ref[rows, :] = _rms(xn, pre_ref[...]).astype(hn_ref.dtype)


def _ffn_down(act, w_out, x, post_w, pre_w, tm=512):
    m, d = x.shape
    tm = min(tm, m)
    row = lambda i: (i, 0)
    fixed = lambda i: (0, 0)
    return pl.pallas_call(
        _ffn_down_kernel,
        grid=(m // tm,),
        in_specs=[pl.BlockSpec((tm, D_FF), row),
                  pl.BlockSpec((D_FF, d), fixed, pipeline_mode=pl.Buffered(1)),
                  pl.BlockSpec((tm, d), row), pl.BlockSpec((1, d), fixed), pl.BlockSpec((1, d), fixed)],
        out_specs=[pl.BlockSpec((tm, d), row), pl.BlockSpec((tm, d), row)],
        out_shape=[jax.ShapeDtypeStruct((m, d), F32), jax.ShapeDtypeStruct((m, d), BF16)],
        compiler_params=_params("parallel"),
        name="ffn_down",
    )(act, w_out, x, post_w.reshape(1, d), pre_w.reshape(1, d))


def _hgrn_kernel(q_ref, f_ref, i_ref, g_ref, lbraw_ref, nw_ref, s0_ref, o_ref, sout_ref, st_ref,
                 *, layer, sc, n_sub, bb):
    ci = pl.program_id(1)

    @pl.when(ci == 0)
    def _():
        for b in range(bb):
            for h in range(H_A):
                st_ref[b, h] = s0_ref[b, h].T

    raw = lbraw_ref[...]
    e = jnp.exp(raw - jnp.max(raw, axis=0, keepdims=True))
    pr = e / jnp.sum(e, axis=0, keepdims=True)
    lb_all = jnp.zeros((1, D_A), F32)
    for r in range(1, layer + 1):
        lb_all = lb_all + pr[r:r + 1, :]
    nw_all = nw_ref[...]
    tri = _lower_tri(sc, strict=False).astype(F32)
    row = lax.broadcasted_iota(jnp.int32, (sc, LANES), 0)
    sub = 8
    lane8 = lax.broadcasted_iota(jnp.int32, (sub, LANES), 1)

    oml = 1.0 - lb_all
    heads = [slice(h * DK_A, (h + 1) * DK_A) for h in range(H_A)]

    def body(idx, carry):
        b = idx // n_sub
        rows = pl.ds(pl.multiple_of((idx % n_sub) * sc, sc), sc)
        zf = f_ref[b, rows, :]
        iv = i_ref[b, rows, :]
        fg = lb_all + oml * _sigmoid(zf)
        logf = jnp.log(jnp.maximum(fg, TINY))
        kk = oml * _sigmoid(-zf)
        qs = _silu(q_ref[b, rows, :])
        bc = _dot_mask_lhs(tri, logf)
        blast = bc[sc - 1:sc, :]
        qe = (qs * jnp.exp(bc)).astype(BF16)
        kh = (kk * jnp.exp(blast - bc)).astype(BF16)
        ivb = iv.astype(BF16)
        keep = jnp.exp(blast)
        sts = [st_ref[b, h] for h in range(H_A)]
        o_inter = [_dot_nt(qe[:, sl], sts[h].astype(BF16)) for h, sl in enumerate(heads)]
        upd = [_dot_tn(ivb[:, sl], kh[:, sl]) for sl in heads]
        for h, sl in enumerate(heads):
            st_ref[b, h] = sts[h] * keep[:, sl] + upd[h]
        atts = []
        for h, sl in enumerate(heads):
            bc_h, kk_h, qs_h = bc[:, sl], kk[:, sl], qs[:, sl]
            att_t = [jnp.zeros((sub, LANES), F32)] * (sc // sub)
            for t in range(sc):
                n = (t // sub + 1) * sub
                dec = jnp.exp(bc_h[t:t + 1, :] - bc_h[:n])
                pair = jnp.where(row[:n] <= t, kk_h[:n] * dec * qs_h[t:t + 1, :], 0.0)
                a_col = jnp.sum(pair, axis=-1, keepdims=True)
                for k in range(n // sub):
                    att_t[k] = jnp.where(lane8 == t, a_col[k * sub:(k + 1) * sub], att_t[k])
            atts.append(jnp.concatenate(att_t, axis=0)[:, :sc].astype(BF16))
        o_intra = [_dot_tn(atts[h], ivb[:, sl]) for h, sl in enumerate(heads)]
        outs = []
        for h in range(H_A):
            o = o_intra[h] + o_inter[h]
            outs.append(o * lax.rsqrt(jnp.mean(o * o, axis=-1, keepdims=True) + RMS_EPS))
        o = jnp.concatenate(outs, axis=-1) * nw_all
        o_ref[b, rows, :] = (o * _silu(g_ref[b, rows, :])).astype(o_ref.dtype)
        return carry

    lax.fori_loop(0, bb * n_sub, body, 0, unroll=math.gcd(bb * n_sub, 4))

    @pl.when(ci == pl.num_programs(1) - 1)
    def _():
        for b in range(bb):
            for h in range(H_A):
                sout_ref[b, h] = st_ref[b, h].T


def _stacked_state_call(kernel_fn, prev, state_out_index, in_specs, args, **kw):
    if prev is None:
        return pl.pallas_call(kernel_fn, in_specs=in_specs, **kw)(*args)

    def with_prev(prev_ref, *refs):
        del prev_ref
        kernel_fn(*refs)

    return pl.pallas_call(with_prev, in_specs=[pl.BlockSpec(memory_space=pl.ANY)] + in_specs,
                          input_output_aliases={0: state_out_index}, **kw)(prev, *args)


def _hgrn(z_a, lb_raw, norm_w, s_all, prev, layer, bb, tc, sc):
    bsz, t, _ = z_a.shape
    col = lambda k: (lambda b, c: (b, c, k))
    fixed2 = lambda b, c: (0, 0)
    state = pl.BlockSpec((None, bb, H_A, DK_A, DV_A), lambda b, c: (layer, b, 0, 0, 0))
    return _stacked_state_call(
        functools.partial(_hgrn_kernel, layer=layer, sc=sc, n_sub=tc // sc, bb=bb), prev, 1,
        [pl.BlockSpec((bb, tc, D_A), col(0)), pl.BlockSpec((bb, tc, D_A), col(1)),
         pl.BlockSpec((bb, tc, D_A), col(2)), pl.BlockSpec((bb, tc, D_A), col(3)),
         pl.BlockSpec((N_EVEN, D_A), fixed2), pl.BlockSpec((1, D_A), fixed2), state],
        (z_a, z_a, z_a, z_a, lb_raw, norm_w.reshape(1, D_A), s_all),
        grid=(bsz // bb, t // tc),
        out_specs=[pl.BlockSpec((bb, tc, D_A), lambda b, c: (b, c, 0)), state],
        out_shape=[jax.ShapeDtypeStruct((bsz, t, D_A), BF16),
                   jax.ShapeDtypeStruct(s_all.shape, F32)],
        scratch_shapes=[pltpu.VMEM((bb, H_A, DV_A, DK_A), F32)],
        compiler_params=_params("parallel", "arbitrary"),
        name="hgrn2",
    )


def _head_sums(x):
    rows = x.shape[0]
    seg = (lax.broadcasted_iota(jnp.int32, (LANES, LANES), 0) // N_B
           == lax.broadcasted_iota(jnp.int32, (LANES, LANES), 1) // N_B).astype(F32)
    n_tiles = D_B // LANES
    xs = jnp.concatenate([x[:, i * LANES:(i + 1) * LANES] for i in range(n_tiles)], axis=0)
    s = _dot_mask_rhs(xs, seg, pieces=2)
    return jnp.concatenate([s[i * rows:(i + 1) * rows] for i in range(n_tiles)], axis=1)


def _rwkv_kernel(z_ref, sh0_ref, s0_ref, mu_ref, w0_ref, w2_ref, a0_ref, a2_ref, g2_ref, kk_ref, ka_ref,
                 rk_ref, lnw_ref, lnb_ref, o_ref, sout_ref, shout_ref, s_ref, prev_ref, *, c, nb):
    ci = pl.program_id(1)
    rows = nb * c
    dz = D_RWKV_PROJ

    @pl.when(ci == 0)
    def _():
        s_ref[...] = s0_ref[...]
        prev_ref[...] = sh0_ref[...]

    blocks = [slice(b * c, (b + 1) * c) for b in range(nb)]
    last = [slice((b + 1) * c - 1, (b + 1) * c) for b in range(nb)]

    def per_block(rows_of):
        n = rows_of[0].shape[-1]
        blk = lax.broadcasted_iota(jnp.int32, (rows, n), 0) // c
        out = jnp.broadcast_to(rows_of[0], (rows, n))
        for b in range(1, nb):
            out = jnp.where(blk == b, rows_of[b], out)
        return out

    zb = z_ref[...].reshape(rows, dz)
    row = lax.broadcasted_iota(jnp.int32, (rows, dz), 0)
    prev = jnp.where(row % c == 0, per_block([prev_ref[b] for b in range(nb)]), pltpu.roll(zb, 1, axis=0))
    for b in range(nb):
        prev_ref[b] = zb[last[b], :]
    zs = zb + (prev - zb) * mu_ref[...]
    r = zs[:, 0:D_B]
    kb = zs[:, D_B:2 * D_B]
    v = zs[:, 2 * D_B:3 * D_B]
    o0 = 3 * D_B
    wl = zs[:, o0:o0 + R_W]
    al = zs[:, o0 + R_W:o0 + R_W + R_A]
    gl = zs[:, o0 + R_W + R_A:o0 + R_W + R_A + R_G]
    wlin = w0_ref[...] + _dot(jnp.tanh(wl).astype(BF16), w2_ref[...])
    w_log = -math.exp(-0.5) * _sigmoid(wlin)
    a = _sigmoid(a0_ref[...] + _dot(al.astype(BF16), a2_ref[...]))
    gate = _dot(_sigmoid(gl).astype(BF16), g2_ref[...])
    kkr = kb * kk_ref[...]
    kmod = kb * (1.0 + (a - 1.0) * ka_ref[...])

    ri = lax.broadcasted_iota(jnp.int32, (rows, rows), 0)
    cj = lax.broadcasted_iota(jnp.int32, (rows, rows), 1)
    tri = jnp.where(ri // c == cj // c, (ri >= cj).astype(F32), 0.0)
    bc = _dot_mask_lhs(tri, w_log)
    b_last = [bc[l, :] for l in last]
    blast = per_block(b_last)
    gam = jnp.exp(bc)
    inv_gam = jnp.exp(-bc)
    gam_prev = jnp.exp(bc - w_log)
    to_end = jnp.exp(blast - bc)
    gam_last = [jnp.exp(x) for x in b_last]
    kkn = kkr / jnp.maximum(jnp.sqrt(_head_sums(kkr * kkr)), 1e-12)
    beta = kkn * a
    xa = -kkn * gam_prev
    xr = r * gam
    yb = beta * inv_gam
    yk = kmod * inv_gam
    eb = beta * to_end
    ek = kmod * to_end
    lhs = [jnp.concatenate([xa[bl], xr[bl]], axis=0).astype(BF16) for bl in blocks]
    rhs = [jnp.concatenate([yb[bl], yk[bl]], axis=0).astype(BF16) for bl in blocks]
    end = [jnp.concatenate([eb[bl], ek[bl]], axis=0).astype(BF16) for bl in blocks]
    vb = [v[bl].astype(BF16) for bl in blocks]

    ri2 = lax.broadcasted_iota(jnp.int32, (2 * c, 2 * c), 0)
    cj2 = lax.broadcasted_iota(jnp.int32, (2 * c, 2 * c), 1)
    keep = (ri2 % c + ri2 // c) > (cj2 % c)
    n_steps = max(1, int(math.log2(c)))
    heads = [slice(h * N_B, (h + 1) * N_B) for h in range(H_B)]
    pairs = [(b, h) for b in range(nb) for h in range(H_B)]
    idx = range(len(pairs))
    s0 = [s_ref[b, h] for b, h in pairs]
    res = [_dot_nt(lhs[b][:, heads[h]], jnp.concatenate([rhs[b][:, heads[h]], s0[i].astype(BF16)], axis=0))
           for i, (b, h) in enumerate(pairs)]
    att = [jnp.where(keep, res[i][:, :2 * c], 0.0) for i in idx]
    from_s0 = [res[i][:, 2 * c:] for i in idx]
    u = [from_s0[i][:c] + _dot(att[i][:c, c:].astype(BF16), vb[b][:, heads[h]]) for i, (b, h) in enumerate(pairs)]
    a_ab = [att[i][:c, :c] for i in idx]
    tq = lax.broadcasted_iota(jnp.int32, (c, c), 0)
    sq = lax.broadcasted_iota(jnp.int32, (c, c), 1)

    def joins(m):
        return ((tq // m - sq // m) * 2 + (tq // m) % 2) == 3

    a_k = [jnp.where(joins(1), a_ab[i], 0.0) for i in idx]
    u = [u[i] + _dot(a_k[i].astype(BF16), u[i].astype(BF16)) for i in idx]
    tinv = [jnp.where(tq == sq, 1.0, a_k[i]) for i in idx]
    for level in range(1, n_steps):
        a_k = [jnp.where(joins(1 << level), a_ab[i], 0.0).astype(BF16) for i in idx]
        tb = [tinv[i].astype(BF16) for i in idx]
        if level + 1 < n_steps:
            ut = [jnp.concatenate([u[i], tinv[i]], axis=1) for i in idx]
            x = [_dot(a_k[i], ut[i].astype(BF16)) for i in idx]
            ut = [ut[i] + _dot(tb[i], x[i].astype(BF16)) for i in idx]
            u = [ut[i][:, :N_B] for i in idx]
            tinv = [ut[i][:, N_B:] for i in idx]
        else:
            x = [_dot(a_k[i], u[i].astype(BF16)) for i in idx]
            u = [u[i] + _dot(tb[i], x[i].astype(BF16)) for i in idx]
    uv = [jnp.concatenate([u[i].astype(BF16), vb[b][:, heads[h]]], axis=0) for i, (b, h) in enumerate(pairs)]
    ys = [from_s0[i][c:] + _dot(att[i][c:].astype(BF16), uv[i]) for i in idx]
    s_new = [_dot_tn(uv[i], end[b][:, heads[h]]) for i, (b, h) in enumerate(pairs)]
    for i, (b, h) in enumerate(pairs):
        s_ref[b, h] = s0[i] * gam_last[b][:, heads[h]] + s_new[i]

    y = jnp.concatenate([jnp.concatenate(ys[b * H_B:(b + 1) * H_B], axis=-1) for b in range(nb)], axis=0)
    yc = y - _head_sums(y) * (1.0 / N_B)
    var = _head_sums(yc * yc) * (1.0 / N_B)
    yn = yc * lax.rsqrt(var + GN_EPS) * lnw_ref[...] + lnb_ref[...]
    out = (yn + _head_sums(r * kmod * rk_ref[...]) * v) * gate
    o_ref[...] = out.reshape(nb, c, D_B).astype(o_ref.dtype)

    @pl.when(ci == pl.num_programs(1) - 1)
    def _():
        sout_ref[...] = s_ref[...]
        for b in range(nb):
            shout_ref[b] = zb[last[b], :]


def _rwkv(z_b, s_all, prev, shift0, p, j, c, nb):
    bsz, t, dz = z_b.shape
    vec = lambda x: x.reshape(1, -1)
    fixed2 = lambda b, ci: (0, 0)
    vspec = lambda n: pl.BlockSpec((1, n), fixed2)
    state = pl.BlockSpec((None, nb, H_B, N_B, N_B), lambda b, ci: (j, b, 0, 0, 0))
    shift = pl.BlockSpec((nb, 1, dz), lambda b, ci: (b, 0, 0))
    return _stacked_state_call(
        functools.partial(_rwkv_kernel, c=c, nb=nb), prev, 1,
        [pl.BlockSpec((nb, c, dz), lambda b, ci: (b, ci, 0)), shift, state,
         vspec(dz), vspec(D_B),
         pl.BlockSpec((R_W, D_B), fixed2), vspec(D_B),
         pl.BlockSpec((R_A, D_B), fixed2), pl.BlockSpec((R_G, D_B), fixed2),
         vspec(D_B), vspec(D_B), vspec(D_B), vspec(D_B), vspec(D_B)],
        (z_b, shift0.reshape(bsz, 1, dz), s_all, vec(p['rwkv_mu'][j]), vec(p['rwkv_w0'][j]),
         p['rwkv_w2'][j].astype(BF16), vec(p['rwkv_a0'][j]), p['rwkv_a2'][j].astype(BF16),
         p['rwkv_g2'][j].astype(BF16), vec(p['rwkv_k_k'][j]), vec(p['rwkv_k_a'][j]),
         vec(p['rwkv_r_k'][j]), vec(p['rwkv_ln_w'][j]), vec(p['rwkv_ln_b'][j])),
        grid=(bsz // nb, t // c),
        out_specs=[pl.BlockSpec((nb, c, D_B), lambda b, ci: (b, ci, 0)), state, shift],
        out_shape=[jax.ShapeDtypeStruct((bsz, t, D_B), BF16),
                   jax.ShapeDtypeStruct(s_all.shape, F32),
                   jax.ShapeDtypeStruct((bsz, 1, dz), F32)],
        scratch_shapes=[pltpu.VMEM((nb, H_B, N_B, N_B), F32), pltpu.VMEM((nb, 1, dz), F32)],
        compiler_params=_params("parallel", "arbitrary"),
        name="rwkv7",
    )


def _rope_kernel(ang_ref, sin_ref, cos_ref, *, tt, pos0):
    row = lax.broadcasted_iota(jnp.int32, (tt, DK_C), 0)
    ang = (pos0 + pl.program_id(0) * tt + row).astype(F32) * ang_ref[...]
    sin_ref[...] = jnp.sin(ang)
    cos_ref[...] = jnp.cos(ang)


def _rope_table(pos0, t):
    angle = 1.0 / (10000.0 ** jnp.linspace(0.0, 1.0, DK_C // 2, dtype=F32))
    angle = jnp.repeat(angle, 2).reshape(1, DK_C)
    tt = min(t, 256)
    out = pl.BlockSpec((tt, DK_C), lambda i: (i, 0))
    return pl.pallas_call(
        functools.partial(_rope_kernel, tt=tt, pos0=pos0),
        grid=(t // tt,),
        in_specs=[pl.BlockSpec((1, DK_C), lambda i: (0, 0))],
        out_specs=[out, out],
        out_shape=[jax.ShapeDtypeStruct((t, DK_C), F32)] * 2,
        compiler_params=_params("parallel"),
        name="rope_table",
    )(angle)


def _ret_kernel(q_ref, k_ref, v_ref, g_ref, sin_ref, cos_ref, s0_ref, o_ref, sout_ref, s_ref, *, c, nb):
    ci = pl.program_id(1)

    @pl.when(ci == 0)
    def _():
        s_ref[...] = s0_ref[...]

    sin = jnp.concatenate([sin_ref[...]] * H_C, axis=1)
    cos = jnp.concatenate([cos_ref[...]] * H_C, axis=1)
    even = (lax.broadcasted_iota(jnp.int32, (c, D_C), 1) % 2) == 0

    def rope(x):
        rot = jnp.where(even, -pltpu.roll(x, D_C - 1, axis=1), pltpu.roll(x, 1, axis=1))
        return x * cos + rot * sin

    ti = lax.broadcasted_iota(jnp.int32, (c, c), 0)
    si = lax.broadcasted_iota(jnp.int32, (c, c), 1)
    causal = ti >= si
    dist = jnp.where(causal, (ti - si).astype(F32), 0.0)
    tcol = lax.broadcasted_iota(jnp.int32, (c, 1), 0).astype(F32)
    heads = [slice(h * DK_C, (h + 1) * DK_C) for h in range(H_C)]
    log_gamma = [math.log1p(-2.0 ** (-5.0 - h)) for h in range(H_C)]
    dec = [jnp.where(causal, jnp.exp(dist * lg), 0.0) for lg in log_gamma]
    from_start = [jnp.exp((tcol + 1.0) * lg) for lg in log_gamma]
    to_end = [jnp.exp((float(c) - 1.0 - tcol) * lg) for lg in log_gamma]
    keep = [math.exp(float(c) * lg) for lg in log_gamma]

    pairs = [(b, h) for b in range(nb) for h in range(H_C)]
    qs = [rope(q_ref[b]) for b in range(nb)]
    ks = [rope(k_ref[b]) * (DK_C ** -0.5) for b in range(nb)]
    vb = [v_ref[b].astype(BF16) for b in range(nb)]
    qb = [q.astype(BF16) for q in qs]
    kb = [k.astype(BF16) for k in ks]
    s_old = [s_ref[b, h] for b, h in pairs]
    att = [_dot_nt(qb[b][:, heads[h]], kb[b][:, heads[h]]) for b, h in pairs]
    o_st = [_dot((qs[b][:, heads[h]] * from_start[h]).astype(BF16), s_old[i].astype(BF16))
            for i, (b, h) in enumerate(pairs)]
    o_in = [_dot((att[i] * dec[h]).astype(BF16), vb[b][:, heads[h]]) for i, (b, h) in enumerate(pairs)]
    upd = [_dot_tn((ks[b][:, heads[h]] * to_end[h]).astype(BF16), vb[b][:, heads[h]]) for b, h in pairs]
    s_new = [keep[h] * s_old[i] + upd[i] for i, (b, h) in enumerate(pairs)]
    for i, (b, h) in enumerate(pairs):
        s_ref[b, h] = s_new[i]
    for b in range(nb):
        outs = []
        for h in range(H_C):
            o = o_in[b * H_C + h] + o_st[b * H_C + h]
            outs.append(o * lax.rsqrt(jnp.mean(o * o, axis=-1, keepdims=True) + RMS_EPS))
        o_ref[b] = (jnp.concatenate(outs, axis=-1) * _silu(g_ref[b])).astype(o_ref.dtype)

    @pl.when(ci == pl.num_programs(1) - 1)
    def _():
        for i, (b, h) in enumerate(pairs):
            sout_ref[b, h] = s_new[i]


def _retention(z, s_all, prev, j, sin, cos, c, nb):
    bsz, t, _ = z.shape
    col = lambda k: (lambda b, ci: (b, ci, k))
    state = pl.BlockSpec((None, nb, H_C, DK_C, DV_C), lambda b, ci: (j, b, 0, 0, 0))
    table = pl.BlockSpec((c, DK_C), lambda b, ci: (ci, 0))
    return _stacked_state_call(
        functools.partial(_ret_kernel, c=c, nb=nb), prev, 1,
        [pl.BlockSpec((nb, c, D_C), col(0)), pl.BlockSpec((nb, c, D_C), col(1)),
         pl.BlockSpec((nb, c, D_C), col(2)), pl.BlockSpec((nb, c, D_C), col(3)),
         table, table, state],
        (z, z, z, z, sin, cos, s_all),
        grid=(bsz // nb, t // c),
        out_specs=[pl.BlockSpec((nb, c, D_C), lambda b, ci: (b, ci, 0)), state],
        out_shape=[jax.ShapeDtypeStruct((bsz, t, D_C), BF16), jax.ShapeDtypeStruct(s_all.shape, F32)],
        scratch_shapes=[pltpu.VMEM((nb, H_C, DK_C, DV_C), F32)],
        compiler_params=_params("parallel", "arbitrary"),
        name="retention",
    )


def _rglru_kernel(y_ref, x_ref, conv0_ref, h0_ref, cw_ref, cb_ref, wa_ref, wx_ref, ba_ref, bx_ref, lam_ref,
                  o_ref, convout_ref, hout_ref, tail_ref, h_ref, *, tc, nb, pos0):
    ci = pl.program_id(1)
    rows = nb * tc
    sub = 8

    @pl.when(ci == 0)
    def _():
        tail_ref[...] = conv0_ref[...]
        h_ref[...] = h0_ref[...]

    x = x_ref[...].reshape(rows, D_D)
    row = lax.broadcasted_iota(jnp.int32, (rows, D_D), 0)
    t_in = row % tc
    tails = [tail_ref[b] for b in range(nb)]

    def shifted(d):
        r = pltpu.roll(x, d, axis=0)
        for m in range(d):
            k = CONV_W - 1 - d + m
            for b in range(nb):
                r = jnp.where(row == b * tc + m, tails[b][k:k + 1, :], r)
        return r

    cw = cw_ref[...]
    xc = cb_ref[...] + cw[CONV_W - 1:CONV_W, :] * x
    for d in range(1, CONV_W):
        xc = xc + cw[CONV_W - 1 - d:CONV_W - d, :] * shifted(d)
    new_tails = [x[(b + 1) * tc - (CONV_W - 1):(b + 1) * tc, :] for b in range(nb)]
    for b in range(nb):
        tail_ref[b] = new_tails[b]

    ra, rx = [], []
    for hd in range(H_D):
        xh = xc[:, hd * BW_D:(hd + 1) * BW_D].astype(BF16)
        ra.append(_dot(xh, wa_ref[hd]))
        rx.append(_dot(xh, wx_ref[hd]))
    rg = _sigmoid(jnp.concatenate(ra, axis=-1) + ba_ref[...])
    ig = _sigmoid(jnp.concatenate(rx, axis=-1) + bx_ref[...])
    nlam = -lam_ref[...]
    softplus = jnp.maximum(nlam, 0.0) + jnp.log1p(jnp.exp(-jnp.abs(nlam)))
    log_a = -LRU_C * rg * softplus
    a = jnp.exp(log_a)
    mult = jnp.sqrt(-jnp.tanh(log_a) * (a * a + 1.0))
    mult = jnp.where(pos0 + ci * tc + t_in == 0, 1.0, mult)
    bt = mult * ig * xc

    in_sub = row % sub
    d = 1
    while d < sub:
        m = in_sub >= d
        bt = jnp.where(m, a * pltpu.roll(bt, d, axis=0) + bt, bt)
        a = jnp.where(m, a * pltpu.roll(a, d, axis=0), a)
        d *= 2
    hs = []
    for b in range(nb):
        carry = h_ref[b]
        for k in range(tc // sub):
            sl = slice(b * tc + k * sub, b * tc + (k + 1) * sub)
            h = bt[sl] + a[sl] * carry
            hs.append(h)
            carry = h[sub - 1:sub, :]
        h_ref[b] = carry
    hseq = jnp.concatenate(hs, axis=0)
    y = y_ref[...].reshape(rows, D_D)
    gelu = 0.5 * y * (1.0 + jnp.tanh(math.sqrt(2.0 / math.pi) * (y + 0.044715 * (y * y * y))))
    o_ref[...] = (hseq * gelu).reshape(nb, tc, D_D).astype(o_ref.dtype)

    @pl.when(ci == pl.num_programs(1) - 1)
    def _():
        for b in range(nb):
            convout_ref[b] = new_tails[b]
        hout_ref[...] = h_ref[...]


def _rglru(z, conv0, h0, p, j, pos0, tc, nb):
    bsz, t, _ = z.shape
    vec = lambda x: x.reshape(1, -1)
    fixed2 = lambda b, ci: (0, 0)
    fixed3 = lambda b, ci: (0, 0, 0)
    vspec = pl.BlockSpec((1, D_D), fixed2)
    ycol = 4 * D_C // D_D
    conv = pl.BlockSpec((nb, CONV_W - 1, D_D), lambda b, ci: (b, 0, 0))
    hid = pl.BlockSpec((nb, 1, D_D), lambda b, ci: (b, 0, 0))
    return pl.pallas_call(
        functools.partial(_rglru_kernel, tc=tc, nb=nb, pos0=pos0),
        grid=(bsz // nb, t // tc),
        in_specs=[pl.BlockSpec((nb, tc, D_D), lambda b, ci: (b, ci, ycol)),
                  pl.BlockSpec((nb, tc, D_D), lambda b, ci: (b, ci, ycol + 1)),
                  conv, hid,
                  pl.BlockSpec((CONV_W, D_D), fixed2), vspec,
                  pl.BlockSpec((H_D, BW_D, BW_D), fixed3), pl.BlockSpec((H_D, BW_D, BW_D), fixed3),
                  vspec, vspec, vspec],
        out_specs=[pl.BlockSpec((nb, tc, D_D), lambda b, ci: (b, ci, 0)), conv, hid],
        out_shape=[jax.ShapeDtypeStruct((bsz, t, D_D), BF16),
                   jax.ShapeDtypeStruct((bsz, CONV_W - 1, D_D), F32),
                   jax.ShapeDtypeStruct((bsz, 1, D_D), F32)],
        scratch_shapes=[pltpu.VMEM((nb, CONV_W - 1, D_D), F32), pltpu.VMEM((nb, 1, D_D), F32)],
        compiler_params=_params("parallel", "arbitrary"),
        name="rglru",
    )(z, z, conv0, h0.reshape(bsz, 1, D_D), p['conv_w'][j], vec(p['conv_b'][j]),
      p['rglru_wa'][j].astype(BF16), p['rglru_wx'][j].astype(BF16),
      vec(p['rglru_ba'][j]), vec(p['rglru_bx'][j]), vec(p['rglru_lambda'][j]))


def _tiles(bsz, t):
    if t >= 256:
        return dict(hgrn=(1, 256, 16), rwkv=(64, math.gcd(bsz, 2)), ret=(256, 1), lru=(256, 1))
    nb = math.gcd(bsz, 4)
    return dict(hgrn=(nb, t, min(t, 8)), rwkv=(t, nb), ret=(t, nb), lru=(t, nb))


def _run_trunk(x, pos0, st_hgrn, st_rwkv, st_shift, st_ret, st_conv, st_h, p, wb):
    bsz, t, d = x.shape
    m = bsz * t
    tl = _tiles(bsz, t)
    x2 = x.reshape(m, d)
    hn = _rmsnorm_bf16(x2, p['norm_mix_pre'][0], tm=min(512, m))
    n_hgrn, n_rwkv, n_ret = None, None, None
    n_shift, n_conv, n_h = [], [], []
    sin, cos = _rope_table(pos0, t)
    for l in range(DEPTH):
        j = l // 2
        if l % 2 == 0:
            z_a = _in_proj(hn, p['w_in_even'], j, 4 * D_A, tn=1024, tm=1024).reshape(bsz, t, 4 * D_A)
            z_b = _in_proj(hn, wb['w_in_rwkv'], j, D_RWKV_PROJ, tn=D_RWKV_PROJ // 2, tm=512)
            z_b = z_b.reshape(bsz, t, D_RWKV_PROJ)
            bb, tc, sc = tl['hgrn']
            o_a, n_hgrn = _hgrn(z_a, p['hgrn_lb_raw'], p['hgrn_norm_w'][j], st_hgrn, n_hgrn, j, bb, tc, sc)
            o_b, n_rwkv, sh = _rwkv(z_b, st_rwkv, n_rwkv, st_shift[j], p, j, *tl['rwkv'])
            n_shift.append(sh.reshape(bsz, D_RWKV_PROJ))
            w_out = wb['w_out_even'][j]
        else:
            z = _in_proj(hn, p['w_in_odd'], j, 4 * D_C + 2 * D_D, tn=1024, tm=1024)
            z = z.reshape(bsz, t, 4 * D_C + 2 * D_D)
            o_a, n_ret = _retention(z, st_ret, n_ret, j, sin, cos, *tl['ret'])
            o_b, scv, shh = _rglru(z, st_conv[j], st_h[j], p, j, pos0, *tl['lru'])
            n_conv.append(scv)
            n_h.append(shh.reshape(bsz, D_D))
            w_out = wb['w_out_odd'][j]
        x2, hn = _outproj(o_a.reshape(m, -1), o_b.reshape(m, -1), w_out, x2,
                          p['norm_mix_post'][l], p['norm_ffn_pre'][l], tm=min(512, m))
        next_pre = p['norm_mix_pre'][(l + 1) % DEPTH]
        act = _ffn_up(hn, p['w_ffn_in'], l)
        x2, hn = _ffn_down(act, wb['w_ffn_out'][l], x2, p['norm_ffn_post'][l], next_pre)
    return x2.reshape(bsz, t, d), (n_hgrn, n_rwkv, jnp.stack(n_shift),
                                   n_ret, jnp.stack(n_conv), jnp.stack(n_h))


def _zero_state(s, batch):
    return jnp.zeros((s.shape[0], batch) + s.shape[2:], s.dtype)


def kernel(x_prompt, x_sample, state_hgrn, state_rwkv, state_rwkv_shift, state_ret, state_rglru_conv, state_rglru_h, norm_mix_pre, norm_mix_post, norm_ffn_pre, norm_ffn_post, w_in_even, w_out_even, hgrn_lb_raw, hgrn_norm_w, rwkv_mu, rwkv_w0, rwkv_w2, rwkv_a0, rwkv_a2, rwkv_g2, rwkv_k_k, rwkv_k_a, rwkv_r_k, rwkv_ln_w, rwkv_ln_b, w_in_odd, w_out_odd, conv_w, conv_b, rglru_wa, rglru_ba, rglru_wx, rglru_bx, rglru_lambda, w_ffn_in, w_ffn_out):
    p = dict(norm_mix_pre=norm_mix_pre, norm_mix_post=norm_mix_post, norm_ffn_pre=norm_ffn_pre,
             norm_ffn_post=norm_ffn_post, hgrn_lb_raw=hgrn_lb_raw, hgrn_norm_w=hgrn_norm_w,
             rwkv_mu=rwkv_mu, rwkv_w0=rwkv_w0, rwkv_w2=rwkv_w2, rwkv_a0=rwkv_a0, rwkv_a2=rwkv_a2,
             rwkv_g2=rwkv_g2, rwkv_k_k=rwkv_k_k, rwkv_k_a=rwkv_k_a, rwkv_r_k=rwkv_r_k,
             rwkv_ln_w=rwkv_ln_w, rwkv_ln_b=rwkv_ln_b, conv_w=conv_w, conv_b=conv_b,
             rglru_wa=rglru_wa, rglru_ba=rglru_ba, rglru_wx=rglru_wx, rglru_bx=rglru_bx,
             rglru_lambda=rglru_lambda, w_in_even=w_in_even, w_in_odd=w_in_odd, w_ffn_in=w_ffn_in)
    wb = dict(w_in_rwkv=w_in_even[:, :, 4 * D_A:], w_out_even=w_out_even.astype(BF16),
              w_out_odd=w_out_odd.astype(BF16), w_ffn_out=w_ffn_out.astype(BF16))
    bp = x_prompt.shape[0]
    y_prompt, st_p = _run_trunk(x_prompt, 0,
                                _zero_state(state_hgrn, bp), _zero_state(state_rwkv, bp),
                                _zero_state(state_rwkv_shift, bp), _zero_state(state_ret, bp),
                                _zero_state(state_rglru_conv, bp), _zero_state(state_rglru_h, bp), p, wb)
    y_sample, st_s = _run_trunk(x_sample, PAST_LEN, state_hgrn, state_rwkv, state_rwkv_shift,
                                state_ret, state_rglru_conv, state_rglru_h, p, wb)
    hgrn_p, rwkv_p, shift_p, ret_p, conv_p, h_p = st_p
    hgrn_s, rwkv_s, shift_s, ret_s, conv_s, h_s = st_s
    return (y_prompt, y_sample, hgrn_p, hgrn_s, rwkv_p, rwkv_s, shift_p, shift_s,
            ret_p, ret_s, conv_p, conv_s, h_p, h_s)
```

```python
import functools
import math

import jax
import jax.numpy as jnp
from jax import lax
from jax.experimental import pallas as pl
from jax.experimental.pallas import tpu as pltpu

F32 = jnp.float32
BF16 = jnp.bfloat16

D_MODEL = 2048
DEPTH = 4
N_EVEN = (DEPTH + 1) // 2
N_ODD = DEPTH // 2
D_A = D_MODEL // 2
H_A = 8
DK_A = 128
DV_A = 128
D_B = D_MODEL // 2
N_B = 64
H_B = D_B // N_B
R_W = 64
R_A = 64
R_G = 128
D_RWKV_PROJ = 3 * D_B + R_W + R_A + R_G
D_C = D_MODEL // 2
H_C = 4
DK_C = D_C // H_C
DV_C = D_C // H_C
D_D = D_MODEL // 2
H_D = 4
BW_D = D_D // H_D
CONV_W = 4
LRU_C = 8.0
D_FF = ((8 * D_MODEL + 3 * 256 - 1) // (3 * 256)) * 256
RMS_EPS = 1e-6
GN_EPS = 64e-5
TINY = 1e-30
PAST_LEN = 16384

VMEM_LIMIT_BYTES = 60 * 1024 * 1024
LANES = 128


def _params(*sem):
    return pltpu.CompilerParams(dimension_semantics=sem, vmem_limit_bytes=VMEM_LIMIT_BYTES)


def _rms(x, w):
    return x * lax.rsqrt(jnp.mean(x * x, axis=-1, keepdims=True) + RMS_EPS) * w


def _sigmoid(x):
    return jax.nn.sigmoid(x)


def _silu(x):
    return x * jax.nn.sigmoid(x)


def _dot(a, b):
    return jnp.dot(a, b, preferred_element_type=F32)


def _dot_nt(a, b):
    return lax.dot_general(a, b, (((1,), (1,)), ((), ())), preferred_element_type=F32)


def _dot_tn(a, b):
    return lax.dot_general(a, b, (((0,), (0,)), ((), ())), preferred_element_type=F32)


def _split(x, pieces):
    out = []
    for _ in range(pieces - 1):
        part = x.astype(BF16)
        out.append(part)
        x = x - part.astype(F32)
    out.append(x.astype(BF16))
    return out


def _dot_mask_lhs(mask, x):
    m = mask.astype(BF16)
    return sum(_dot(m, part) for part in _split(x, 3))


def _dot_mask_rhs(x, mask, pieces):
    m = mask.astype(BF16)
    return sum(_dot(part, m) for part in _split(x, pieces))


def _row_parts(n, size):
    size = min(size, n)
    return [slice(i, i + size) for i in range(0, n, size)]


def _lower_tri(n, strict):
    r = lax.broadcasted_iota(jnp.int32, (n, n), 0)
    c = lax.broadcasted_iota(jnp.int32, (n, n), 1)
    return (r > c) if strict else (r >= c)


def _rmsnorm_kernel(x_ref, w_ref, o_ref):
    o_ref[...] = _rms(x_ref[...], w_ref[...]).astype(o_ref.dtype)


def _rmsnorm_bf16(x, w, tm=512):
    m, d = x.shape
    return pl.pallas_call(
        _rmsnorm_kernel,
        grid=(m // tm,),
        in_specs=[pl.BlockSpec((tm, d), lambda i: (i, 0)), pl.BlockSpec((1, d), lambda i: (0, 0))],
        out_specs=pl.BlockSpec((tm, d), lambda i: (i, 0)),
        out_shape=jax.ShapeDtypeStruct((m, d), BF16),
        compiler_params=_params("parallel"),
        name="rmsnorm",
    )(x, w.reshape(1, d))


def _in_proj_kernel(a_ref, w_ref, o_ref, wb_ref):
    @pl.when(pl.program_id(1) == 0)
    def _():
        wb_ref[...] = w_ref[...].astype(BF16)

    o_ref[...] = _dot(a_ref[...], wb_ref[...])


def _in_proj(a, w_all, layer, n, tn, tm):
    m, k = a.shape
    tm = min(tm, m)
    return pl.pallas_call(
        _in_proj_kernel,
        grid=(n // tn, m // tm),
        in_specs=[pl.BlockSpec((tm, k), lambda j, i: (i, 0)),
                  pl.BlockSpec((None, k, tn), lambda j, i: (layer, 0, j))],
        out_specs=pl.BlockSpec((tm, tn), lambda j, i: (i, j)),
        out_shape=jax.ShapeDtypeStruct((m, n), F32),
        scratch_shapes=[pltpu.VMEM((k, tn), BF16)],
        compiler_params=_params("parallel", "arbitrary"),
        name="in_proj",
    )(a, w_all)


def _outproj_kernel(oa_ref, ob_ref, wa_ref, wb_ref, x_ref, post_ref, pre_ref, xo_ref, hn_ref):
    for rows in _row_parts(x_ref.shape[0], 256):
        mix = _dot(oa_ref[rows, :], wa_ref[...]) + _dot(ob_ref[rows, :], wb_ref[...])
        xn = x_ref[rows, :] + _rms(mix, post_ref[...])
        xo_ref[rows, :] = xn
        hn_ref[rows, :] = _rms(xn, pre_ref[...]).astype(hn_ref.dtype)


def _outproj(oa, ob, w_out, x, post_w, pre_w, tm=512):
    m, d = x.shape
    ka, kb = oa.shape[1], ob.shape[1]
    row = lambda i: (i, 0)
    fixed = lambda i: (0, 0)
    return pl.pallas_call(
        _outproj_kernel,
        grid=(m // tm,),
        in_specs=[pl.BlockSpec((tm, ka), row), pl.BlockSpec((tm, kb), row),
                  pl.BlockSpec((ka, d), fixed), pl.BlockSpec((kb, d), lambda i: (1, 0)),
                  pl.BlockSpec((tm, d), row), pl.BlockSpec((1, d), fixed), pl.BlockSpec((1, d), fixed)],
        out_specs=[pl.BlockSpec((tm, d), row), pl.BlockSpec((tm, d), row)],
        out_shape=[jax.ShapeDtypeStruct((m, d), F32), jax.ShapeDtypeStruct((m, d), BF16)],
        compiler_params=_params("parallel"),
        name="out_proj",
    )(oa, ob, w_out, w_out, x, post_w.reshape(1, d), pre_w.reshape(1, d))


def _ffn_up_kernel(h_ref, wg_ref, wu_ref, act_ref, wgb_ref, wub_ref):
    @pl.when(pl.program_id(1) == 0)
    def _():
        wgb_ref[...] = wg_ref[...].astype(BF16)
        wub_ref[...] = wu_ref[...].astype(BF16)

    h = h_ref[...]
    act_ref[...] = (_silu(_dot(h, wgb_ref[...])) * _dot(h, wub_ref[...])).astype(act_ref.dtype)


def _ffn_up(hn, w_in_all, layer, tm=1024, tf=512):
    m, d = hn.shape
    tm = min(tm, m)
    nf = D_FF // tf
    return pl.pallas_call(
        _ffn_up_kernel,
        grid=(nf, m // tm),
        in_specs=[pl.BlockSpec((tm, d), lambda j, i: (i, 0)),
                  pl.BlockSpec((None, d, tf), lambda j, i: (layer, 0, j)),
                  pl.BlockSpec((None, d, tf), lambda j, i: (layer, 0, j + nf))],
        out_specs=pl.BlockSpec((tm, tf), lambda j, i: (i, j)),
        out_shape=jax.ShapeDtypeStruct((m, D_FF), BF16),
        scratch_shapes=[pltpu.VMEM((d, tf), BF16), pltpu.VMEM((d, tf), BF16)],
        compiler_params=_params("parallel", "arbitrary"),
        name="ffn_up",
    )(hn, w_in_all, w_in_all)


def _ffn_down_kernel(act_ref, wo_ref, x_ref, post_ref, pre_ref, xo_ref, hn_ref):
    for rows in _row_parts(x_ref.shape[0], 256):
        xn = x_ref[rows, :] + _rms(_dot(act_ref[rows, :], wo_ref[...]), post_ref[...])
        xo_ref[rows, :] = xn
        hn_ref[rows, :] = _rms(xn, pre_ref[...]).astype(hn_ref.dtype)


def _ffn_down(act, w_out, x, post_w, pre_w, tm=512):
    m, d = x.shape
    tm = min(tm, m)
    row = lambda i: (i, 0)
    fixed = lambda i: (0, 0)
    return pl.pallas_call(
        _ffn_down_kernel,
        grid=(m // tm,),
        in_specs=[pl.BlockSpec((tm, D_FF), row),
                  pl.BlockSpec((D_FF, d), fixed, pipeline_mode=pl.Buffered(1)),
                  pl.BlockSpec((tm, d), row), pl.BlockSpec((1, d), fixed), pl.BlockSpec((1, d), fixed)],
        out_specs=[pl.BlockSpec((tm, d), row), pl.BlockSpec((tm, d), row)],
        out_shape=[jax.ShapeDtypeStruct((m, d), F32), jax.ShapeDtypeStruct((m, d), BF16)],
        compiler_params=_params("parallel"),
        name="ffn_down",
    )(act, w_out, x, post_w.reshape(1, d), pre_w.reshape(1, d))


def _hgrn_kernel(q_ref, f_ref, i_ref, g_ref, lbraw_ref, nw_ref, s0_ref, o_ref, sout_ref, st_ref,
                 *, layer, sc, n_sub, bb):
    ci = pl.program_id(1)

    @pl.when(ci == 0)
    def _():
        for b in range(bb):
            for h in range(H_A):
                st_ref[b, h] = s0_ref[b, h].T

    raw = lbraw_ref[...]
    e = jnp.exp(raw - jnp.max(raw, axis=0, keepdims=True))
    pr = e / jnp.sum(e, axis=0, keepdims=True)
    lb_all = jnp.zeros((1, D_A), F32)
    for r in range(1, layer + 1):
        lb_all = lb_all + pr[r:r + 1, :]
    nw_all = nw_ref[...]
    tri = _lower_tri(sc, strict=False).astype(F32)
    row = lax.broadcasted_iota(jnp.int32, (sc, LANES), 0)
    sub = 8
    lane8 = lax.broadcasted_iota(jnp.int32, (sub, LANES), 1)

    oml = 1.0 - lb_all
    heads = [slice(h * DK_A, (h + 1) * DK_A) for h in range(H_A)]

    def body(idx, carry):
        b = idx // n_sub
        rows = pl.ds(pl.multiple_of((idx % n_sub) * sc, sc), sc)
        zf = f_ref[b, rows, :]
        iv = i_ref[b, rows, :]
        fg = lb_all + oml * _sigmoid(zf)
        logf = jnp.log(jnp.maximum(fg, TINY))
        kk = oml * _sigmoid(-zf)
        qs = _silu(q_ref[b, rows, :])
        bc = _dot_mask_lhs(tri, logf)
        blast = bc[sc - 1:sc, :]
        qe = (qs * jnp.exp(bc)).astype(BF16)
        kh = (kk * jnp.exp(blast - bc)).astype(BF16)
        ivb = iv.astype(BF16)
        keep = jnp.exp(blast)
        sts = [st_ref[b, h] for h in range(H_A)]
        o_inter = [_dot_nt(qe[:, sl], sts[h].astype(BF16)) for h, sl in enumerate(heads)]
        upd = [_dot_tn(ivb[:, sl], kh[:, sl]) for sl in heads]
        for h, sl in enumerate(heads):
            st_ref[b, h] = sts[h] * keep[:, sl] + upd[h]
        atts = []
        for h, sl in enumerate(heads):
            bc_h, kk_h, qs_h = bc[:, sl], kk[:, sl], qs[:, sl]
            att_t = [jnp.zeros((sub, LANES), F32)] * (sc // sub)
            for t in range(sc):
                n = (t // sub + 1) * sub
                dec = jnp.exp(bc_h[t:t + 1, :] - bc_h[:n])
                pair = jnp.where(row[:n] <= t, kk_h[:n] * dec * qs_h[t:t + 1, :], 0.0)
                a_col = jnp.sum(pair, axis=-1, keepdims=True)
                for k in range(n // sub):
                    att_t[k] = jnp.where(lane8 == t, a_col[k * sub:(k + 1) * sub], att_t[k])
            atts.append(jnp.concatenate(att_t, axis=0)[:, :sc].astype(BF16))
        o_intra = [_dot_tn(atts[h], ivb[:, sl]) for h, sl in enumerate(heads)]
        outs = []
        for h in range(H_A):
            o = o_intra[h] + o_inter[h]
            outs.append(o * lax.rsqrt(jnp.mean(o * o, axis=-1, keepdims=True) + RMS_EPS))
        o = jnp.concatenate(outs, axis=-1) * nw_all
        o_ref[b, rows, :] = (o * _silu(g_ref[b, rows, :])).astype(o_ref.dtype)
        return carry

    lax.fori_loop(0, bb * n_sub, body, 0, unroll=math.gcd(bb * n_sub, 4))

    @pl.when(ci == pl.num_programs(1) - 1)
    def _():
        for b in range(bb):
            for h in range(H_A):
                sout_ref[b, h] = st_ref[b, h].T


def _stacked_state_call(kernel_fn, prev, state_out_index, in_specs, args, **kw):
    if prev is None:
        return pl.pallas_call(kernel_fn, in_specs=in_specs, **kw)(*args)

    def with_prev(prev_ref, *refs):
        del prev_ref
        kernel_fn(*refs)

    return pl.pallas_call(with_prev, in_specs=[pl.BlockSpec(memory_space=pl.ANY)] + in_specs,
                          input_output_aliases={0: state_out_index}, **kw)(prev, *args)


def _hgrn(z_a, lb_raw, norm_w, s_all, prev, layer, bb, tc, sc):
    bsz, t, _ = z_a.shape
    col = lambda k: (lambda b, c: (b, c, k))
    fixed2 = lambda b, c: (0, 0)
    state = pl.BlockSpec((None, bb, H_A, DK_A, DV_A), lambda b, c: (layer, b, 0, 0, 0))
    return _stacked_state_call(
        functools.partial(_hgrn_kernel, layer=layer, sc=sc, n_sub=tc // sc, bb=bb), prev, 1,
        [pl.BlockSpec((bb, tc, D_A), col(0)), pl.BlockSpec((bb, tc, D_A), col(1)),
         pl.BlockSpec((bb, tc, D_A), col(2)), pl.BlockSpec((bb, tc, D_A), col(3)),
         pl.BlockSpec((N_EVEN, D_A), fixed2), pl.BlockSpec((1, D_A), fixed2), state],
        (z_a, z_a, z_a, z_a, lb_raw, norm_w.reshape(1, D_A), s_all),
        grid=(bsz // bb, t // tc),
        out_specs=[pl.BlockSpec((bb, tc, D_A), lambda b, c: (b, c, 0)), state],
        out_shape=[jax.ShapeDtypeStruct((bsz, t, D_A), BF16),
                   jax.ShapeDtypeStruct(s_all.shape, F32)],
        scratch_shapes=[pltpu.VMEM((bb, H_A, DV_A, DK_A), F32)],
        compiler_params=_params("parallel", "arbitrary"),
        name="hgrn2",
    )


def _head_sums(x):
    rows = x.shape[0]
    seg = (lax.broadcasted_iota(jnp.int32, (LANES, LANES), 0) // N_B
           == lax.broadcasted_iota(jnp.int32, (LANES, LANES), 1) // N_B).astype(F32)
    n_tiles = D_B // LANES
    xs = jnp.concatenate([x[:, i * LANES:(i + 1) * LANES] for i in range(n_tiles)], axis=0)
    s = _dot_mask_rhs(xs, seg, pieces=2)
    return jnp.concatenate([s[i * rows:(i + 1) * rows] for i in range(n_tiles)], axis=1)


def _rwkv_kernel(z_ref, sh0_ref, s0_ref, mu_ref, w0_ref, w2_ref, a0_ref, a2_ref, g2_ref, kk_ref, ka_ref,
                 rk_ref, lnw_ref, lnb_ref, o_ref, sout_ref, shout_ref, s_ref, prev_ref, *, c, nb):
    ci = pl.program_id(1)
    rows = nb * c
    dz = D_RWKV_PROJ

    @pl.when(ci == 0)
    def _():
        s_ref[...] = s0_ref[...]
        prev_ref[...] = sh0_ref[...]

    blocks = [slice(b * c, (b + 1) * c) for b in range(nb)]
    last = [slice((b + 1) * c - 1, (b + 1) * c) for b in range(nb)]

    def per_block(rows_of):
        n = rows_of[0].shape[-1]
        blk = lax.broadcasted_iota(jnp.int32, (rows, n), 0) // c
        out = jnp.broadcast_to(rows_of[0], (rows, n))
        for b in range(1, nb):
            out = jnp.where(blk == b, rows_of[b], out)
        return out

    zb = z_ref[...].reshape(rows, dz)
    row = lax.broadcasted_iota(jnp.int32, (rows, dz), 0)
    prev = jnp.where(row % c == 0, per_block([prev_ref[b] for b in range(nb)]), pltpu.roll(zb, 1, axis=0))
    for b in range(nb):
        prev_ref[b] = zb[last[b], :]
    zs = zb + (prev - zb) * mu_ref[...]
    r = zs[:, 0:D_B]
    kb = zs[:, D_B:2 * D_B]
    v = zs[:, 2 * D_B:3 * D_B]
    o0 = 3 * D_B
    wl = zs[:, o0:o0 + R_W]
    al = zs[:, o0 + R_W:o0 + R_W + R_A]
    gl = zs[:, o0 + R_W + R_A:o0 + R_W + R_A + R_G]
    wlin = w0_ref[...] + _dot(jnp.tanh(wl).astype(BF16), w2_ref[...])
    w_log = -math.exp(-0.5) * _sigmoid(wlin)
    a = _sigmoid(a0_ref[...] + _dot(al.astype(BF16), a2_ref[...]))
    gate = _dot(_sigmoid(gl).astype(BF16), g2_ref[...])
    kkr = kb * kk_ref[...]
    kmod = kb * (1.0 + (a - 1.0) * ka_ref[...])

    ri = lax.broadcasted_iota(jnp.int32, (rows, rows), 0)
    cj = lax.broadcasted_iota(jnp.int32, (rows, rows), 1)
    tri = jnp.where(ri // c == cj // c, (ri >= cj).astype(F32), 0.0)
    bc = _dot_mask_lhs(tri, w_log)
    b_last = [bc[l, :] for l in last]
    blast = per_block(b_last)
    gam = jnp.exp(bc)
    inv_gam = jnp.exp(-bc)
    gam_prev = jnp.exp(bc - w_log)
    to_end = jnp.exp(blast - bc)
    gam_last = [jnp.exp(x) for x in b_last]
    kkn = kkr / jnp.maximum(jnp.sqrt(_head_sums(kkr * kkr)), 1e-12)
    beta = kkn * a
    xa = -kkn * gam_prev
    xr = r * gam
    yb = beta * inv_gam
    yk = kmod * inv_gam
    eb = beta * to_end
    ek = kmod * to_end
    lhs = [jnp.concatenate([xa[bl], xr[bl]], axis=0).astype(BF16) for bl in blocks]
    rhs = [jnp.concatenate([yb[bl], yk[bl]], axis=0).astype(BF16) for bl in blocks]
    end = [jnp.concatenate([eb[bl], ek[bl]], axis=0).astype(BF16) for bl in blocks]
    vb = [v[bl].astype(BF16) for bl in blocks]

    ri2 = lax.broadcasted_iota(jnp.int32, (2 * c, 2 * c), 0)
    cj2 = lax.broadcasted_iota(jnp.int32, (2 * c, 2 * c), 1)
    keep = (ri2 % c + ri2 // c) > (cj2 % c)
    n_steps = max(1, int(math.log2(c)))
    heads = [slice(h * N_B, (h + 1) * N_B) for h in range(H_B)]
    pairs = [(b, h) for b in range(nb) for h in range(H_B)]
    idx = range(len(pairs))
    s0 = [s_ref[b, h] for b, h in pairs]
    res = [_dot_nt(lhs[b][:, heads[h]], jnp.concatenate([rhs[b][:, heads[h]], s0[i].astype(BF16)], axis=0))
           for i, (b, h) in enumerate(pairs)]
    att = [jnp.where(keep, res[i][:, :2 * c], 0.0) for i in idx]
    from_s0 = [res[i][:, 2 * c:] for i in idx]
    u = [from_s0[i][:c] + _dot(att[i][:c, c:].astype(BF16), vb[b][:, heads[h]]) for i, (b, h) in enumerate(pairs)]
    a_ab = [att[i][:c, :c] for i in idx]
    tq = lax.broadcasted_iota(jnp.int32, (c, c), 0)
    sq = lax.broadcasted_iota(jnp.int32, (c, c), 1)

    def joins(m):
        return ((tq // m - sq // m) * 2 + (tq // m) % 2) == 3

    a_k = [jnp.where(joins(1), a_ab[i], 0.0) for i in idx]
    u = [u[i] + _dot(a_k[i].astype(BF16), u[i].astype(BF16)) for i in idx]
    tinv = [jnp.where(tq == sq, 1.0, a_k[i]) for i in idx]
    for level in range(1, n_steps):
        a_k = [jnp.where(joins(1 << level), a_ab[i], 0.0).astype(BF16) for i in idx]
        tb = [tinv[i].astype(BF16) for i in idx]
        if level + 1 < n_steps:
            ut = [jnp.concatenate([u[i], tinv[i]], axis=1) for i in idx]
            x = [_dot(a_k[i], ut[i].astype(BF16)) for i in idx]
            ut = [ut[i] + _dot(tb[i], x[i].astype(BF16)) for i in idx]
            u = [ut[i][:, :N_B] for i in idx]
            tinv = [ut[i][:, N_B:] for i in idx]
        else:
            x = [_dot(a_k[i], u[i].astype(BF16)) for i in idx]
            u = [u[i] + _dot(tb[i], x[i].astype(BF16)) for i in idx]
    uv = [jnp.concatenate([u[i].astype(BF16), vb[b][:, heads[h]]], axis=0) for i, (b, h) in enumerate(pairs)]
    ys = [from_s0[i][c:] + _dot(att[i][c:].astype(BF16), uv[i]) for i in idx]
    s_new = [_dot_tn(uv[i], end[b][:, heads[h]]) for i, (b, h) in enumerate(pairs)]
    for i, (b, h) in enumerate(pairs):
        s_ref[b, h] = s0[i] * gam_last[b][:, heads[h]] + s_new[i]

    y = jnp.concatenate([jnp.concatenate(ys[b * H_B:(b + 1) * H_B], axis=-1) for b in range(nb)], axis=0)
    yc = y - _head_sums(y) * (1.0 / N_B)
    var = _head_sums(yc * yc) * (1.0 / N_B)
    yn = yc * lax.rsqrt(var + GN_EPS) * lnw_ref[...] + lnb_ref[...]
    out = (yn + _head_sums(r * kmod * rk_ref[...]) * v) * gate
    o_ref[...] = out.reshape(nb, c, D_B).astype(o_ref.dtype)

    @pl.when(ci == pl.num_programs(1) - 1)
    def _():
        sout_ref[...] = s_ref[...]
        for b in range(nb):
            shout_ref[b] = zb[last[b], :]


def _rwkv(z_b, s_all, prev, shift0, p, j, c, nb):
    bsz, t, dz = z_b.shape
    vec = lambda x: x.reshape(1, -1)
    fixed2 = lambda b, ci: (0, 0)
    vspec = lambda n: pl.BlockSpec((1, n), fixed2)
    state = pl.BlockSpec((None, nb, H_B, N_B, N_B), lambda b, ci: (j, b, 0, 0, 0))
    shift = pl.BlockSpec((nb, 1, dz), lambda b, ci: (b, 0, 0))
    return _stacked_state_call(
        functools.partial(_rwkv_kernel, c=c, nb=nb), prev, 1,
        [pl.BlockSpec((nb, c, dz), lambda b, ci: (b, ci, 0)), shift, state,
         vspec(dz), vspec(D_B),
         pl.BlockSpec((R_W, D_B), fixed2), vspec(D_B),
         pl.BlockSpec((R_A, D_B), fixed2), pl.BlockSpec((R_G, D_B), fixed2),
         vspec(D_B), vspec(D_B), vspec(D_B), vspec(D_B), vspec(D_B)],
        (z_b, shift0.reshape(bsz, 1, dz), s_all, vec(p['rwkv_mu'][j]), vec(p['rwkv_w0'][j]),
         p['rwkv_w2'][j].astype(BF16), vec(p['rwkv_a0'][j]), p['rwkv_a2'][j].astype(BF16),
         p['rwkv_g2'][j].astype(BF16), vec(p['rwkv_k_k'][j]), vec(p['rwkv_k_a'][j]),
         vec(p['rwkv_r_k'][j]), vec(p['rwkv_ln_w'][j]), vec(p['rwkv_ln_b'][j])),
        grid=(bsz // nb, t // c),
        out_specs=[pl.BlockSpec((nb, c, D_B), lambda b, ci: (b, ci, 0)), state, shift],
        out_shape=[jax.ShapeDtypeStruct((bsz, t, D_B), BF16),
                   jax.ShapeDtypeStruct(s_all.shape, F32),
                   jax.ShapeDtypeStruct((bsz, 1, dz), F32)],
        scratch_shapes=[pltpu.VMEM((nb, H_B, N_B, N_B), F32), pltpu.VMEM((nb, 1, dz), F32)],
        compiler_params=_params("parallel", "arbitrary"),
        name="rwkv7",
    )


def _rope_kernel(ang_ref, sin_ref, cos_ref, *, tt, pos0):
    row = lax.broadcasted_iota(jnp.int32, (tt, DK_C), 0)
    ang = (pos0 + pl.program_id(0) * tt + row).astype(F32) * ang_ref[...]
    sin_ref[...] = jnp.sin(ang)
    cos_ref[...] = jnp.cos(ang)


def _rope_table(pos0, t):
    angle = 1.0 / (10000.0 ** jnp.linspace(0.0, 1.0, DK_C // 2, dtype=F32))
    angle = jnp.repeat(angle, 2).reshape(1, DK_C)
    tt = min(t, 256)
    out = pl.BlockSpec((tt, DK_C), lambda i: (i, 0))
    return pl.pallas_call(
        functools.partial(_rope_kernel, tt=tt, pos0=pos0),
        grid=(t // tt,),
        in_specs=[pl.BlockSpec((1, DK_C), lambda i: (0, 0))],
        out_specs=[out, out],
        out_shape=[jax.ShapeDtypeStruct((t, DK_C), F32)] * 2,
        compiler_params=_params("parallel"),
        name="rope_table",
    )(angle)


def _ret_kernel(q_ref, k_ref, v_ref, g_ref, sin_ref, cos_ref, s0_ref, o_ref, sout_ref, s_ref, *, c, nb):
    ci = pl.program_id(1)

    @pl.when(ci == 0)
    def _():
        s_ref[...] = s0_ref[...]

    sin = jnp.concatenate([sin_ref[...]] * H_C, axis=1)
    cos = jnp.concatenate([cos_ref[...]] * H_C, axis=1)
    even = (lax.broadcasted_iota(jnp.int32, (c, D_C), 1) % 2) == 0

    def rope(x):
        rot = jnp.where(even, -pltpu.roll(x, D_C - 1, axis=1), pltpu.roll(x, 1, axis=1))
        return x * cos + rot * sin

    ti = lax.broadcasted_iota(jnp.int32, (c, c), 0)
    si = lax.broadcasted_iota(jnp.int32, (c, c), 1)
    causal = ti >= si
    dist = jnp.where(causal, (ti - si).astype(F32), 0.0)
    tcol = lax.broadcasted_iota(jnp.int32, (c, 1), 0).astype(F32)
    heads = [slice(h * DK_C, (h + 1) * DK_C) for h in range(H_C)]
    log_gamma = [math.log1p(-2.0 ** (-5.0 - h)) for h in range(H_C)]
    dec = [jnp.where(causal, jnp.exp(dist * lg), 0.0) for lg in log_gamma]
    from_start = [jnp.exp((tcol + 1.0) * lg) for lg in log_gamma]
    to_end = [jnp.exp((float(c) - 1.0 - tcol) * lg) for lg in log_gamma]
    keep = [math.exp(float(c) * lg) for lg in log_gamma]

    pairs = [(b, h) for b in range(nb) for h in range(H_C)]
    qs = [rope(q_ref[b]) for b in range(nb)]
    ks = [rope(k_ref[b]) * (DK_C ** -0.5) for b in range(nb)]
    vb = [v_ref[b].astype(BF16) for b in range(nb)]
    qb = [q.astype(BF16) for q in qs]
    kb = [k.astype(BF16) for k in ks]
    s_old = [s_ref[b, h] for b, h in pairs]
    att = [_dot_nt(qb[b][:, heads[h]], kb[b][:, heads[h]]) for b, h in pairs]
    o_st = [_dot((qs[b][:, heads[h]] * from_start[h]).astype(BF16), s_old[i].astype(BF16))
            for i, (b, h) in enumerate(pairs)]
    o_in = [_dot((att[i] * dec[h]).astype(BF16), vb[b][:, heads[h]]) for i, (b, h) in enumerate(pairs)]
    upd = [_dot_tn((ks[b][:, heads[h]] * to_end[h]).astype(BF16), vb[b][:, heads[h]]) for b, h in pairs]
    s_new = [keep[h] * s_old[i] + upd[i] for i, (b, h) in enumerate(pairs)]
    for i, (b, h) in enumerate(pairs):
        s_ref[b, h] = s_new[i]
    for b in range(nb):
        outs = []
        for h in range(H_C):
            o = o_in[b * H_C + h] + o_st[b * H_C + h]
            outs.append(o * lax.rsqrt(jnp.mean(o * o, axis=-1, keepdims=True) + RMS_EPS))
        o_ref[b] = (jnp.concatenate(outs, axis=-1) * _silu(g_ref[b])).astype(o_ref.dtype)

    @pl.when(ci == pl.num_programs(1) - 1)
    def _():
        for i, (b, h) in enumerate(pairs):
            sout_ref[b, h] = s_new[i]


def _retention(z, s_all, prev, j, sin, cos, c, nb):
    bsz, t, _ = z.shape
    col = lambda k: (lambda b, ci: (b, ci, k))
    state = pl.BlockSpec((None, nb, H_C, DK_C, DV_C), lambda b, ci: (j, b, 0, 0, 0))
    table = pl.BlockSpec((c, DK_C), lambda b, ci: (ci, 0))
    return _stacked_state_call(
        functools.partial(_ret_kernel, c=c, nb=nb), prev, 1,
        [pl.BlockSpec((nb, c, D_C), col(0)), pl.BlockSpec((nb, c, D_C), col(1)),
         pl.BlockSpec((nb, c, D_C), col(2)), pl.BlockSpec((nb, c, D_C), col(3)),
         table, table, state],
        (z, z, z, z, sin, cos, s_all),
        grid=(bsz // nb, t // c),
        out_specs=[pl.BlockSpec((nb, c, D_C), lambda b, ci: (b, ci, 0)), state],
        out_shape=[jax.ShapeDtypeStruct((bsz, t, D_C), BF16), jax.ShapeDtypeStruct(s_all.shape, F32)],
        scratch_shapes=[pltpu.VMEM((nb, H_C, DK_C, DV_C), F32)],
        compiler_params=_params("parallel", "arbitrary"),
        name="retention",
    )


def _rglru_kernel(y_ref, x_ref, conv0_ref, h0_ref, cw_ref, cb_ref, wa_ref, wx_ref, ba_ref, bx_ref, lam_ref,
                  o_ref, convout_ref, hout_ref, tail_ref, h_ref, *, tc, nb, pos0):
    ci = pl.program_id(1)
    rows = nb * tc
    sub = 8

    @pl.when(ci == 0)
    def _():
        tail_ref[...] = conv0_ref[...]
        h_ref[...] = h0_ref[...]

    x = x_ref[...].reshape(rows, D_D)
    row = lax.broadcasted_iota(jnp.int32, (rows, D_D), 0)
    t_in = row % tc
    tails = [tail_ref[b] for b in range(nb)]

    def shifted(d):
        r = pltpu.roll(x, d, axis=0)
        for m in range(d):
            k = CONV_W - 1 - d + m
            for b in range(nb):
                r = jnp.where(row == b * tc + m, tails[b][k:k + 1, :], r)
        return r

    cw = cw_ref[...]
    xc = cb_ref[...] + cw[CONV_W - 1:CONV_W, :] * x
    for d in range(1, CONV_W):
        xc = xc + cw[CONV_W - 1 - d:CONV_W - d, :] * shifted(d)
    new_tails = [x[(b + 1) * tc - (CONV_W - 1):(b + 1) * tc, :] for b in range(nb)]
    for b in range(nb):
        tail_ref[b] = new_tails[b]

    ra, rx = [], []
    for hd in range(H_D):
        xh = xc[:, hd * BW_D:(hd + 1) * BW_D].astype(BF16)
        ra.append(_dot(xh, wa_ref[hd]))
        rx.append(_dot(xh, wx_ref[hd]))
    rg = _sigmoid(jnp.concatenate(ra, axis=-1) + ba_ref[...])
    ig = _sigmoid(jnp.concatenate(rx, axis=-1) + bx_ref[...])
    nlam = -lam_ref[...]
    softplus = jnp.maximum(nlam, 0.0) + jnp.log1p(jnp.exp(-jnp.abs(nlam)))
    log_a = -LRU_C * rg * softplus
    a = jnp.exp(log_a)
    mult = jnp.sqrt(-jnp.tanh(log_a) * (a * a + 1.0))
    mult = jnp.where(pos0 + ci * tc + t_in == 0, 1.0, mult)
    bt = mult * ig * xc

    in_sub = row % sub
    d = 1
    while d < sub:
        m = in_sub >= d
        bt = jnp.where(m, a * pltpu.roll(bt, d, axis=0) + bt, bt)
        a = jnp.where(m, a * pltpu.roll(a, d, axis=0), a)
        d *= 2
    hs = []
    for b in range(nb):
        carry = h_ref[b]
        for k in range(tc // sub):
            sl = slice(b * tc + k * sub, b * tc + (k + 1) * sub)
            h = bt[sl] + a[sl] * carry
            hs.append(h)
            carry = h[sub - 1:sub, :]
        h_ref[b] = carry
    hseq = jnp.concatenate(hs, axis=0)
    y = y_ref[...].reshape(rows, D_D)
    gelu = 0.5 * y * (1.0 + jnp.tanh(math.sqrt(2.0 / math.pi) * (y + 0.044715 * (y * y * y))))
    o_ref[...] = (hseq * gelu).reshape(nb, tc, D_D).astype(o_ref.dtype)

    @pl.when(ci == pl.num_programs(1) - 1)
    def _():
        for b in range(nb):
            convout_ref[b] = new_tails[b]
        hout_ref[...] = h_ref[...]


def _rglru(z, conv0, h0, p, j, pos0, tc, nb):
    bsz, t, _ = z.shape
    vec = lambda x: x.reshape(1, -1)
    fixed2 = lambda b, ci: (0, 0)
    fixed3 = lambda b, ci: (0, 0, 0)
    vspec = pl.BlockSpec((1, D_D), fixed2)
    ycol = 4 * D_C // D_D
    conv = pl.BlockSpec((nb, CONV_W - 1, D_D), lambda b, ci: (b, 0, 0))
    hid = pl.BlockSpec((nb, 1, D_D), lambda b, ci: (b, 0, 0))
    return pl.pallas_call(
        functools.partial(_rglru_kernel, tc=tc, nb=nb, pos0=pos0),
        grid=(bsz // nb, t // tc),
        in_specs=[pl.BlockSpec((nb, tc, D_D), lambda b, ci: (b, ci, ycol)),
                  pl.BlockSpec((nb, tc, D_D), lambda b, ci: (b, ci, ycol + 1)),
                  conv, hid,
                  pl.BlockSpec((CONV_W, D_D), fixed2), vspec,
                  pl.BlockSpec((H_D, BW_D, BW_D), fixed3), pl.BlockSpec((H_D, BW_D, BW_D), fixed3),
                  vspec, vspec, vspec],
        out_specs=[pl.BlockSpec((nb, tc, D_D), lambda b, ci: (b, ci, 0)), conv, hid],
        out_shape=[jax.ShapeDtypeStruct((bsz, t, D_D), BF16),
                   jax.ShapeDtypeStruct((bsz, CONV_W - 1, D_D), F32),
                   jax.ShapeDtypeStruct((bsz, 1, D_D), F32)],
        scratch_shapes=[pltpu.VMEM((nb, CONV_W - 1, D_D), F32), pltpu.VMEM((nb, 1, D_D), F32)],
        compiler_params=_params("parallel", "arbitrary"),
        name="rglru",
    )(z, z, conv0, h0.reshape(bsz, 1, D_D), p['conv_w'][j], vec(p['conv_b'][j]),
      p['rglru_wa'][j].astype(BF16), p['rglru_wx'][j].astype(BF16),
      vec(p['rglru_ba'][j]), vec(p['rglru_bx'][j]), vec(p['rglru_lambda'][j]))


def _tiles(bsz, t):
    if t >= 256:
        return dict(hgrn=(1, 256, 16), rwkv=(64, math.gcd(bsz, 2)), ret=(256, 1), lru=(256, 1))
    nb = math.gcd(bsz, 4)
    return dict(hgrn=(nb, t, min(t, 8)), rwkv=(t, nb), ret=(t, nb), lru=(t, nb))


def _run_trunk(x, pos0, st_hgrn, st_rwkv, st_shift, st_ret, st_conv, st_h, p, wb):
    bsz, t, d = x.shape
    m = bsz * t
    tl = _tiles(bsz, t)
    x2 = x.reshape(m, d)
    hn = _rmsnorm_bf16(x2, p['norm_mix_pre'][0], tm=min(512, m))
    n_hgrn, n_rwkv, n_ret = None, None, None
    n_shift, n_conv, n_h = [], [], []
    sin, cos = _rope_table(pos0, t)
    for l in range(DEPTH):
        j = l // 2
        if l % 2 == 0:
            z_a = _in_proj(hn, p['w_in_even'], j, 4 * D_A, tn=1024, tm=1024).reshape(bsz, t, 4 * D_A)
            z_b = _in_proj(hn, wb['w_in_rwkv'], j, D_RWKV_PROJ, tn=D_RWKV_PROJ // 2, tm=1024)
            z_b = z_b.reshape(bsz, t, D_RWKV_PROJ)
            bb, tc, sc = tl['hgrn']
            o_a, n_hgrn = _hgrn(z_a, p['hgrn_lb_raw'], p['hgrn_norm_w'][j], st_hgrn, n_hgrn, j, bb, tc, sc)
            o_b, n_rwkv, sh = _rwkv(z_b, st_rwkv, n_rwkv, st_shift[j], p, j, *tl['rwkv'])
            n_shift.append(sh.reshape(bsz, D_RWKV_PROJ))
            w_out = wb['w_out_even'][j]
        else:
            z = _in_proj(hn, p['w_in_odd'], j, 4 * D_C + 2 * D_D, tn=1024, tm=1024)
            z = z.reshape(bsz, t, 4 * D_C + 2 * D_D)
            o_a, n_ret = _retention(z, st_ret, n_ret, j, sin, cos, *tl['ret'])
            o_b, scv, shh = _rglru(z, st_conv[j], st_h[j], p, j, pos0, *tl['lru'])
            n_conv.append(scv)
            n_h.append(shh.reshape(bsz, D_D))
            w_out = wb['w_out_odd'][j]
        x2, hn = _outproj(o_a.reshape(m, -1), o_b.reshape(m, -1), w_out, x2,
                          p['norm_mix_post'][l], p['norm_ffn_pre'][l], tm=min(512, m))
        next_pre = p['norm_mix_pre'][(l + 1) % DEPTH]
        act = _ffn_up(hn, p['w_ffn_in'], l)
        x2, hn = _ffn_down(act, wb['w_ffn_out'][l], x2, p['norm_ffn_post'][l], next_pre)
    return x2.reshape(bsz, t, d), (n_hgrn, n_rwkv, jnp.stack(n_shift),
                                   n_ret, jnp.stack(n_conv), jnp.stack(n_h))


def _zero_state(s, batch):
    return jnp.zeros((s.shape[0], batch) + s.shape[2:], s.dtype)


def kernel(x_prompt, x_sample, state_hgrn, state_rwkv, state_rwkv_shift, state_ret, state_rglru_conv, state_rglru_h, norm_mix_pre, norm_mix_post, norm_ffn_pre, norm_ffn_post, w_in_even, w_out_even, hgrn_lb_raw, hgrn_norm_w, rwkv_mu, rwkv_w0, rwkv_w2, rwkv_a0, rwkv_a2, rwkv_g2, rwkv_k_k, rwkv_k_a, rwkv_r_k, rwkv_ln_w, rwkv_ln_b, w_in_odd, w_out_odd, conv_w, conv_b, rglru_wa, rglru_ba, rglru_wx, rglru_bx, rglru_lambda, w_ffn_in, w_ffn_out):
    p = dict(norm_mix_pre=norm_mix_pre, norm_mix_post=norm_mix_post, norm_ffn_pre=norm_ffn_pre,
             norm_ffn_post=norm_ffn_post, hgrn_lb_raw=hgrn_lb_raw, hgrn_norm_w=hgrn_norm_w,
             rwkv_mu=rwkv_mu, rwkv_w0=rwkv_w0, rwkv_w2=rwkv_w2, rwkv_a0=rwkv_a0, rwkv_a2=rwkv_a2,
             rwkv_g2=rwkv_g2, rwkv_k_k=rwkv_k_k, rwkv_k_a=rwkv_k_a, rwkv_r_k=rwkv_r_k,
             rwkv_ln_w=rwkv_ln_w, rwkv_ln_b=rwkv_ln_b, conv_w=conv_w, conv_b=conv_b,
             rglru_wa=rglru_wa, rglru_ba=rglru_ba, rglru_wx=rglru_wx, rglru_bx=rglru_bx,
             rglru_lambda=rglru_lambda, w_in_even=w_in_even, w_in_odd=w_in_odd, w_ffn_in=w_ffn_in)
    wb = dict(w_in_rwkv=w_in_even[:, :, 4 * D_A:], w_out_even=w_out_even.astype(BF16),
              w_out_odd=w_out_odd.astype(BF16), w_ffn_out=w_ffn_out.astype(BF16))
    bp = x_prompt.shape[0]
    y_prompt, st_p = _run_trunk(x_prompt, 0,
                                _zero_state(state_hgrn, bp), _zero_state(state_rwkv, bp),
                                _zero_state(state_rwkv_shift, bp), _zero_state(state_ret, bp),
                                _zero_state(state_rglru_conv, bp), _zero_state(state_rglru_h, bp), p, wb)
    y_sample, st_s = _run_trunk(x_sample, PAST_LEN, state_hgrn, state_rwkv, state_rwkv_shift,
                                state_ret, state_rglru_conv, state_rglru_h, p, wb)
    hgrn_p, rwkv_p, shift_p, ret_p, conv_p, h_p = st_p
    hgrn_s, rwkv_s, shift_s, ret_s, conv_s, h_s = st_s
    return (y_prompt, y_sample, hgrn_p, hgrn_s, rwkv_p, rwkv_s, shift_p, shift_s,
            ret_p, ret_s, conv_p, conv_s, h_p, h_s)
```
